```python
import jax, jax.numpy as jnp
from jax import lax
import numpy as np

D_MODEL = 2048
BATCH = 2
SEQ = 16384
DEPTH = 4
DEC_BATCH = 16
DEC_SEQ = 64
PAST_LEN = 1024

CHUNK = 64
N_A_LAYERS = DEPTH // 2
N_B_LAYERS = DEPTH - N_A_LAYERS
EXPAND = 128
H_A = D_MODEL // EXPAND
K_A = EXPAND
V_A = D_MODEL // H_A
H_B = 16
D_HEAD_B = D_MODEL // H_B
QBLOCK = 128
D_PLE = 256
FORGET_BIAS_INIT = 3.0
EPS = 1e-6

kernel_name = "yoco_hgrn2_fox_stream_step"


def rmsnorm(x, g):
    xf = x.astype(jnp.float32)
    y = xf * lax.rsqrt(jnp.mean(xf * xf, axis=-1, keepdims=True) + EPS)
    return (y * g.astype(jnp.float32)).astype(x.dtype)


def hgrn2_chunk(S, inp):
    q, k, g, v = inp
    L = q.shape[2]
    G = jnp.cumsum(g, axis=2)
    causal = jnp.tril(jnp.ones((L, L), dtype=bool))
    diff = G[:, :, :, None, :] - G[:, :, None, :, :]
    decay = jnp.exp(jnp.where(causal[:, :, None], diff, -jnp.inf))
    A = jnp.einsum('bhtk,bhsk,bhtsk->bhts', q, k, decay)
    o = jnp.einsum('bhts,bhsv->bhtv', A, v) + jnp.einsum('bhtk,bhkv->bhtv', q * jnp.exp(G), S)
    G_last = G[:, :, -1:, :]
    S_new = jnp.exp(G_last[:, :, 0, :])[..., None] * S + jnp.einsum(
        'bhsk,bhsv->bhkv', k * jnp.exp(G_last - G), v)
    return S_new, o


def hgrn2_layer(x, S0, g_norm, w_in, lb, g_out, w_out):
    B, T, _ = x.shape
    z = rmsnorm(x, g_norm) @ w_in
    q, fl, i, gate = jnp.split(z, 4, axis=-1)
    lbf = lb.astype(jnp.float32)
    f = lbf + (1.0 - lbf) * jax.nn.sigmoid(fl.astype(jnp.float32))
    g = jnp.log(f)
    k = 1.0 - f
    q = jax.nn.silu(q.astype(jnp.float32))
    L = min(CHUNK, T)
    nc = T // L

    def blocks(a, d):
        a = a.astype(jnp.float32).reshape(B, nc, L, H_A, d)
        return jnp.transpose(a, (1, 0, 3, 2, 4))

    S_fin, o = lax.scan(hgrn2_chunk, S0.astype(jnp.float32),
                        (blocks(q, K_A), blocks(k, K_A), blocks(g, K_A), blocks(i, V_A)))
    o = jnp.transpose(o, (1, 0, 3, 2, 4)).reshape(B, T, H_A, V_A)
    o = o * lax.rsqrt(jnp.mean(o * o, axis=-1, keepdims=True) + EPS)
    o = o.reshape(B, T, D_MODEL) * g_out.astype(jnp.float32)
    y = (o * jax.nn.silu(gate.astype(jnp.float32))).astype(x.dtype) @ w_out
    return x + y, S_fin


def shared_kv(x, g_kv, w_kv, b_f):
    B, T, _ = x.shape
    z = rmsnorm(x, g_kv) @ w_kv
    k = z[..., :D_MODEL].reshape(B, T, H_B, D_HEAD_B)
    v = z[..., D_MODEL:2 * D_MODEL].reshape(B, T, H_B, D_HEAD_B)
    logf = jax.nn.log_sigmoid(z[..., 2 * D_MODEL:].astype(jnp.float32) + b_f.astype(jnp.float32))
    return k, v, logf


def fox_attention(q, k_all, v_all, dq, dkT, q_off):
    B, T, H, Dh = q.shape
    Lk = k_all.shape[1]
    k_pos = jnp.arange(Lk)
    blk = min(QBLOCK, T)
    nb = T // blk
    qb = jnp.moveaxis(q.reshape(B, nb, blk, H, Dh), 1, 0)
    dqb = jnp.moveaxis(dq.reshape(B, nb, blk, H), 1, 0)
    pos = (q_off + jnp.arange(T)).reshape(nb, blk)
    scale = Dh ** -0.5

    def one_block(args):
        qi, dqi, pi = args
        s = jnp.einsum('bqhd,bkhd->bhqk', qi, k_all, preferred_element_type=jnp.float32) * scale
        s = s + jnp.transpose(dqi, (0, 2, 1))[..., None] - dkT[:, :, None, :]
        mask = k_pos[None, :] <= pi[:, None]
        s = jnp.where(mask, s, -jnp.inf)
        pr = jax.nn.softmax(s, axis=-1)
        return jnp.einsum('bhqk,bkhd->bqhd', pr.astype(v_all.dtype), v_all)

    o = lax.map(one_block, (qb, dqb, pos))
    return jnp.moveaxis(o, 0, 1).reshape(B, T, H, Dh)


def fox_layer(x, k_all, v_all, dq, dkT, q_off, g_norm, w_in, w_out):
    B, T, _ = x.shape
    z = rmsnorm(x, g_norm) @ w_in
    q, gate = jnp.split(z, 2, axis=-1)
    o = fox_attention(q.reshape(B, T, H_B, D_HEAD_B), k_all, v_all, dq, dkT, q_off)
    o = o.reshape(B, T, D_MODEL).astype(jnp.float32) * jax.nn.silu(gate.astype(jnp.float32))
    return x + o.astype(x.dtype) @ w_out


def ple_add(x, p_i, w_pin, g_pg, w_pg):
    e = (p_i @ w_pin).astype(jnp.float32)
    gate = jax.nn.sigmoid((rmsnorm(x, g_pg) @ w_pg).astype(jnp.float32))
    return x + (gate * e).astype(x.dtype)


def trunk(x, p, hgrn_s0, past_k, past_v, past_logf,
          g_norm_a, w_in_a, lb_logits, g_out_a, w_out_a, g_kv, w_kv, b_f,
          g_norm_b, w_in_b, w_out_b, w_ple_in, g_ple, w_ple_gate, g_final):
    lbs = jnp.cumsum(jax.nn.softmax(lb_logits.astype(jnp.float32), axis=0), axis=0)
    lbs = lbs - lbs[:1]
    hgrn_states = []
    for layer in range(DEPTH):
        if layer < N_A_LAYERS:
            x, s = hgrn2_layer(x, hgrn_s0[layer], g_norm_a[layer], w_in_a[layer], lbs[layer],
                               g_out_a[layer], w_out_a[layer])
            hgrn_states.append(s)
        else:
            j = layer - N_A_LAYERS
            x = fox_layer(x, k_all, v_all, dq, dkT, q_off, g_norm_b[j], w_in_b[j], w_out_b[j])
        x = ple_add(x, p[layer], w_ple_in[layer], g_ple[layer], w_ple_gate[layer])
        if layer == N_A_LAYERS - 1:
            k_new, v_new, logf_new = shared_kv(x, g_kv, w_kv, b_f)
            if past_k is None:
                k_all, v_all, logf_all, q_off = k_new, v_new, logf_new, 0
            else:
                k_all = jnp.concatenate([past_k.astype(k_new.dtype), k_new], axis=1)
                v_all = jnp.concatenate([past_v.astype(v_new.dtype), v_new], axis=1)
                logf_all = jnp.concatenate([past_logf.astype(jnp.float32), logf_new], axis=1)
                q_off = past_k.shape[1]
            dcum = jnp.cumsum(logf_all, axis=1)
            dq = dcum[:, q_off:]
            dkT = jnp.transpose(dcum, (0, 2, 1))
    y = rmsnorm(x, g_final)
    return y, jnp.stack(hgrn_states), k_new, v_new, logf_new


def setup_inputs(seed: int = 0) -> dict:
    key = jax.random.key(seed)
    ks = jax.random.split(key, 32)
    f32 = jnp.float32

    def nrm(k, shape, scale):
        return jax.random.normal(k, shape, f32) * scale

    D = D_MODEL
    return {
        "x_prompt": nrm(ks[0], (BATCH, SEQ, D), 1.0),
        "x_sample": nrm(ks[1], (DEC_BATCH, DEC_SEQ, D), 1.0),
        "state_hgrn": nrm(ks[2], (N_A_LAYERS, DEC_BATCH, H_A, K_A, V_A), 0.5),
        "cache_k": nrm(ks[3], (DEC_BATCH, PAST_LEN, H_B, D_HEAD_B), 1.0),
        "cache_v": nrm(ks[4], (DEC_BATCH, PAST_LEN, H_B, D_HEAD_B), 1.0),
        "cache_logf": jax.nn.log_sigmoid(FORGET_BIAS_INIT + nrm(ks[5], (DEC_BATCH, PAST_LEN, H_B), 1.0)),
        "p_prompt": nrm(ks[6], (DEPTH, BATCH, SEQ, D_PLE), 1.0),
        "p_sample": nrm(ks[7], (DEPTH, DEC_BATCH, DEC_SEQ, D_PLE), 1.0),
        "g_norm_a": 1.0 + nrm(ks[8], (N_A_LAYERS, D), 0.01),
        "w_in_a": nrm(ks[9], (N_A_LAYERS, D, 4 * D), D ** -0.5),
        "lb_logits": nrm(ks[10], (N_A_LAYERS, D), 1.0),
        "g_out_a": 1.0 + nrm(ks[11], (N_A_LAYERS, D), 0.01),
        "w_out_a": nrm(ks[12], (N_A_LAYERS, D, D), D ** -0.5),
        "g_kv": 1.0 + nrm(ks[13], (D,), 0.01),
        "w_kv": nrm(ks[14], (D, 2 * D + H_B), D ** -0.5),
        "b_f": FORGET_BIAS_INIT + nrm(ks[15], (H_B,), 0.1),
        "g_norm_b": 1.0 + nrm(ks[16], (N_B_LAYERS, D), 0.01),
        "w_in_b": nrm(ks[17], (N_B_LAYERS, D, 2 * D), D ** -0.5),
        "w_out_b": nrm(ks[18], (N_B_LAYERS, D, D), D ** -0.5),
        "w_ple_in": nrm(ks[19], (DEPTH, D_PLE, D), D_PLE ** -0.5),
        "g_ple": 1.0 + nrm(ks[20], (DEPTH, D), 0.01),
        "w_ple_gate": nrm(ks[21], (DEPTH, D, D), D ** -0.5),
        "g_final": 1.0 + nrm(ks[22], (D,), 0.01),
    }


def reference(x_prompt, x_sample, state_hgrn, cache_k, cache_v, cache_logf, p_prompt, p_sample,
              g_norm_a, w_in_a, lb_logits, g_out_a, w_out_a, g_kv, w_kv, b_f,
              g_norm_b, w_in_b, w_out_b, w_ple_in, g_ple, w_ple_gate, g_final):
    s0_prompt = jnp.zeros((N_A_LAYERS, x_prompt.shape[0], H_A, K_A, V_A), jnp.float32)
    y_prompt, state_hgrn_prompt, k_prompt, v_prompt, logf_prompt = trunk(
        x_prompt, p_prompt, s0_prompt, None, None, None,
        g_norm_a, w_in_a, lb_logits, g_out_a, w_out_a, g_kv, w_kv, b_f,
        g_norm_b, w_in_b, w_out_b, w_ple_in, g_ple, w_ple_gate, g_final)
    y_sample, state_hgrn_sample, k_sample, v_sample, logf_sample = trunk(
        x_sample, p_sample, state_hgrn, cache_k, cache_v, cache_logf,
        g_norm_a, w_in_a, lb_logits, g_out_a, w_out_a, g_kv, w_kv, b_f,
        g_norm_b, w_in_b, w_out_b, w_ple_in, g_ple, w_ple_gate, g_final)
    return (y_prompt, y_sample, state_hgrn_prompt, state_hgrn_sample,
            k_prompt, v_prompt, logf_prompt, k_sample, v_sample, logf_sample)
```

```python
import functools

import jax
import jax.numpy as jnp
import numpy as np
from jax import lax
from jax.experimental import pallas as pl
from jax.experimental.pallas import tpu as pltpu

F32 = jnp.float32
BF16 = jnp.bfloat16

EPS = 1e-6
HEAD = 128
LANES = 128
CHUNK = 64
N_LEVELS = 6
MASK_VALUE = -1e30
VMEM_LIMIT = 56 * 1024 * 1024

_NT = (((1,), (1,)), ((), ()))


def _params(sem, vmem=VMEM_LIMIT):
    return pltpu.CompilerParams(dimension_semantics=sem, vmem_limit_bytes=vmem)


def _sigmoid(x):
    return 1.0 / (1.0 + jnp.exp(-x))


def _pick(n, prefs):
    for p in prefs:
        if n % p == 0:
            return p
    return n


def _nmm_kernel(*refs, log_sigmoid, with_bf16, row_chunk):
    if log_sigmoid:
        x_ref, g_ref, w_ref, b_ref = refs[:4]
        rest = refs[4:]
    else:
        x_ref, g_ref, w_ref = refs[:3]
        b_ref = None
        rest = refs[3:]
    if with_bf16:
        o_ref, ob_ref, xn_ref = rest
    else:
        o_ref, xn_ref = rest
        ob_ref = None
    tm = x_ref.shape[0]

    @pl.when(pl.program_id(1) == 0)
    def _():
        for r in range(0, tm, row_chunk):
            x = x_ref[r:r + row_chunk, :]
            ms = jnp.mean(x * x, axis=-1, keepdims=True)
            xn_ref[r:r + row_chunk, :] = (x * lax.rsqrt(ms + EPS) * g_ref[...]).astype(BF16)

    z = jnp.dot(xn_ref[...], w_ref[...], preferred_element_type=F32)
    if log_sigmoid:
        z = z + b_ref[...]
        z = jnp.minimum(z, 0.0) - jnp.log(1.0 + jnp.exp(-jnp.abs(z)))
    o_ref[...] = z
    if with_bf16:
        ob_ref[...] = z.astype(BF16)


def _norm_matmul(x, g, w, bias=None, with_bf16=False):
    T, D = x.shape
    N = w.shape[1]
    tm = _pick(T, (1024, 512, 256, 128, 64))
    tn = _pick(N, (512, 256, 128))
    row_chunk = min(tm, 256)
    in_specs = [
        pl.BlockSpec((tm, D), lambda i, j: (i, 0)),
        pl.BlockSpec((1, D), lambda i, j: (0, 0)),
        pl.BlockSpec((D, tn), lambda i, j: (0, j)),
    ]
    args = [x, g.reshape(1, D).astype(F32), w]
    if bias is not None:
        in_specs.append(pl.BlockSpec((1, tn), lambda i, j: (0, j)))
        args.append(bias.reshape(1, N).astype(F32))
    out_shape = [jax.ShapeDtypeStruct((T, N), F32)]
    out_specs = [pl.BlockSpec((tm, tn), lambda i, j: (i, j))]
    if with_bf16:
        out_shape.append(jax.ShapeDtypeStruct((T, N), BF16))
        out_specs.append(pl.BlockSpec((tm, tn), lambda i, j: (i, j)))
    out = pl.pallas_call(
        functools.partial(_nmm_kernel, log_sigmoid=bias is not None, with_bf16=with_bf16,
                          row_chunk=row_chunk),
        grid=(T // tm, N // tn),
        in_specs=in_specs,
        out_specs=out_specs,
        out_shape=out_shape,
        scratch_shapes=[pltpu.VMEM((tm, D), BF16)],
        compiler_params=_params(("parallel", "arbitrary")),
        name="norm_matmul",
    )(*args)
    return out if with_bf16 else out[0]


def _hgrn_constants():
    C = CHUNK
    t = np.arange(C)[:, None]
    u = np.arange(C)[None, :]
    blocks = [(u <= t), (u > t)]
    masks = []
    for lvl in range(N_LEVELS):
        h = C >> (lvl + 1)
        mid = (t // (2 * h)) * (2 * h) + h
        second = t >= mid
        blocks.append(np.where(second, (u >= mid) & (u <= t), (u > t) & (u < mid)))
        mid_s = (u // (2 * h)) * (2 * h) + h
        masks.append((t // (2 * h) == u // (2 * h)) & (t >= mid) & (u < mid_s))
    masks.append(t == u)
    sm = np.concatenate(blocks, axis=0).astype(np.float32)
    mk = np.stack(masks, axis=0).astype(np.float32)
    return jnp.asarray(sm, BF16), jnp.asarray(mk, F32)


def _hgrn_kernel(zq_ref, zf_ref, zi_ref, zg_ref, lb_ref, go_ref, s0_ref, sm_ref, mk_ref,
                 o_ref, sout_ref, st_ref, *, n_chunks):
    C = CHUNK
    tb = pl.program_id(2)

    @pl.when(tb == 0)
    def _():
        st_ref[...] = s0_ref[...].T

    lb = lb_ref[...]
    go = go_ref[...]

    def chunk(c, carry):
        r0 = pl.multiple_of(c * C, C)
        zq = zq_ref[pl.ds(r0, C), :]
        zf = zf_ref[pl.ds(r0, C), :]
        v = zi_ref[pl.ds(r0, C), :]
        zg = zg_ref[pl.ds(r0, C), :]
        q = zq * _sigmoid(zq)
        f = lb + (1.0 - lb) * _sigmoid(zf)
        g = jnp.log(f)
        k = 1.0 - f
        g_hi = g.astype(BF16)
        g_lo = (g - g_hi.astype(F32)).astype(BF16)
        d2 = jnp.dot(sm_ref[...], jnp.concatenate([g_hi, g_lo], axis=1),
                     preferred_element_type=F32)
        e = jnp.exp(d2[:, :HEAD] + d2[:, HEAD:])
        qb = q.astype(BF16)
        kb = k.astype(BF16)
        vb = v.astype(BF16)
        a = mk_ref[N_LEVELS] * lax.dot_general(qb, kb, _NT, preferred_element_type=F32)
        for lvl in range(N_LEVELS):
            el = e[(2 + lvl) * C:(3 + lvl) * C]
            ql = (q * el).astype(BF16)
            kl = (k * el).astype(BF16)
            a = a + mk_ref[lvl] * lax.dot_general(ql, kl, _NT, preferred_element_type=F32)
        st = st_ref[...]
        qg = (q * e[0:C]).astype(BF16)
        o = jnp.dot(a.astype(BF16), vb, preferred_element_type=F32)
        o = o + lax.dot_general(qg, st.astype(BF16), _NT, preferred_element_type=F32)
        kg = (k * e[C:2 * C]).astype(BF16)
        st_ref[...] = st * e[C - 1:C, :] + jnp.dot(v.T.astype(BF16), kg,
                                                   preferred_element_type=F32)
        ms = jnp.mean(o * o, axis=-1, keepdims=True)
        og = o * lax.rsqrt(ms + EPS) * go * (zg * _sigmoid(zg))
        o_ref[pl.ds(r0, C), :] = og.astype(o_ref.dtype)
        return carry

    lax.fori_loop(0, n_chunks, chunk, 0)

    @pl.when(tb == pl.num_programs(2) - 1)
    def _():
        sout_ref[...] = st_ref[...].T


def _hgrn_mix(z, lb, g_out, s0, B, T):
    D = z.shape[1] // 4
    H = D // HEAD
    tb = _pick(T, (512, 256, 128, 64))
    nt = T // tb
    sm, mk = _hgrn_constants()

    def zspec(part):
        return pl.BlockSpec((tb, HEAD), lambda b, h, t: (b * nt + t, part * H + h))

    head_vec = pl.BlockSpec((1, HEAD), lambda b, h, t: (0, h))
    state = pl.BlockSpec((None, None, HEAD, HEAD), lambda b, h, t: (b, h, 0, 0))
    og, s_fin = pl.pallas_call(
        functools.partial(_hgrn_kernel, n_chunks=tb // CHUNK),
        grid=(B, H, nt),
        in_specs=[zspec(0), zspec(1), zspec(2), zspec(3), head_vec, head_vec, state,
                  pl.BlockSpec(sm.shape, lambda b, h, t: (0, 0)),
                  pl.BlockSpec(mk.shape, lambda b, h, t: (0, 0, 0))],
        out_specs=[pl.BlockSpec((tb, HEAD), lambda b, h, t: (b * nt + t, h)), state],
        out_shape=[jax.ShapeDtypeStruct((B * T, D), BF16),
                   jax.ShapeDtypeStruct((B, H, HEAD, HEAD), F32)],
        scratch_shapes=[pltpu.VMEM((HEAD, HEAD), F32)],
        compiler_params=_params(("parallel", "parallel", "arbitrary")),
        name="hgrn_mix",
    )(z, z, z, z, lb.reshape(1, D), g_out.reshape(1, D).astype(F32), s0, sm, mk)
    return og, s_fin


def _out_ple_kernel(og_ref, x_ref, p_ref, wo_ref, gpg_ref, wpg_ref, wpin_ref, gfin_ref, o_ref,
                    *, final, col_chunk):
    D = x_ref.shape[1]
    y = x_ref[...] + jnp.dot(og_ref[...], wo_ref[...], preferred_element_type=F32)
    ms = jnp.mean(y * y, axis=-1, keepdims=True)
    yn = (y * lax.rsqrt(ms + EPS) * gpg_ref[...]).astype(BF16)
    pb = p_ref[...].astype(BF16)
    parts = []
    for c0 in range(0, D, col_chunk):
        gate = _sigmoid(jnp.dot(yn, wpg_ref[:, c0:c0 + col_chunk], preferred_element_type=F32))
        emb = jnp.dot(pb, wpin_ref[:, c0:c0 + col_chunk], preferred_element_type=F32)
        parts.append(y[:, c0:c0 + col_chunk] + gate * emb)
    if final:
        ms2 = sum(jnp.sum(t * t, axis=-1, keepdims=True) for t in parts) * (1.0 / D)
        inv = lax.rsqrt(ms2 + EPS)
        for n, c0 in enumerate(range(0, D, col_chunk)):
            o_ref[:, c0:c0 + col_chunk] = parts[n] * inv * gfin_ref[:, c0:c0 + col_chunk]
    else:
        for n, c0 in enumerate(range(0, D, col_chunk)):
            o_ref[:, c0:c0 + col_chunk] = parts[n]


def _out_ple(og, x, p, w_out, g_pg, w_pg, w_pin, g_final, final):
    T, D = x.shape
    P = p.shape[1]
    tm = _pick(T, (256, 128, 64))
    col_chunk = min(D, 512)
    rows = lambda i: (i, 0)
    fixed = lambda i: (0, 0)
    once = pl.Buffered(1)
    return pl.pallas_call(
        functools.partial(_out_ple_kernel, final=final, col_chunk=col_chunk),
        grid=(T // tm,),
        in_specs=[pl.BlockSpec((tm, D), rows), pl.BlockSpec((tm, D), rows),
                  pl.BlockSpec((tm, P), rows),
                  pl.BlockSpec((D, D), fixed, pipeline_mode=once),
                  pl.BlockSpec((1, D), fixed),
                  pl.BlockSpec((D, D), fixed, pipeline_mode=once),
                  pl.BlockSpec((P, D), fixed, pipeline_mode=once),
                  pl.BlockSpec((1, D), fixed)],
        out_specs=pl.BlockSpec((tm, D), rows),
        out_shape=jax.ShapeDtypeStruct((T, D), F32),
        compiler_params=_params(("parallel",)),
        name="out_ple",
    )(og, x, p, w_out, g_pg.reshape(1, D).astype(F32), w_pg, w_pin,
      g_final.reshape(1, D).astype(F32))


def _cumsum_kernel(x_ref, u_ref, o_ref, carry_ref, *, n_sub):
    @pl.when(pl.program_id(0) == 0)
    def _():
        carry_ref[...] = jnp.zeros_like(carry_ref)

    u = u_ref[...]
    c = carry_ref[:, 0:1]
    for s in range(n_sub):
        x = x_ref[:, s * LANES:(s + 1) * LANES]
        hi = x.astype(BF16)
        r1 = x - hi.astype(F32)
        mid = r1.astype(BF16)
        lo = (r1 - mid.astype(F32)).astype(BF16)
        cs = (jnp.dot(hi, u, preferred_element_type=F32)
              + jnp.dot(mid, u, preferred_element_type=F32)
              + jnp.dot(lo, u, preferred_element_type=F32)) + c
        o_ref[:, s * LANES:(s + 1) * LANES] = cs
        c = cs[:, LANES - 1:LANES]
    carry_ref[...] = jnp.broadcast_to(c, carry_ref.shape)


def _cumsum_last(x):
    R, L = x.shape
    cb = _pick(L, (2048, 1024, 512, 256, 128))
    u = jnp.asarray(np.triu(np.ones((LANES, LANES), np.float32)), BF16)
    return pl.pallas_call(
        functools.partial(_cumsum_kernel, n_sub=cb // LANES),
        grid=(L // cb,),
        in_specs=[pl.BlockSpec((R, cb), lambda j: (0, j)),
                  pl.BlockSpec((LANES, LANES), lambda j: (0, 0))],
        out_specs=pl.BlockSpec((R, cb), lambda j: (0, j)),
        out_shape=jax.ShapeDtypeStruct((R, L), F32),
        scratch_shapes=[pltpu.VMEM((R, LANES), F32)],
        compiler_params=_params(("arbitrary",)),
        name="cumsum",
    )(x, u)


def _fox_kernel(q_ref, gate_ref, k_ref, v_ref, dq_ref, dk_ref, o_ref, *, tq, tk, q_off, scale):
    i = pl.program_id(2)
    q = q_ref[...].astype(BF16)
    dq = dq_ref[...]
    first_pos = q_off + i * tq
    n_full = (first_pos + 1) // tk
    n_all = (first_pos + tq - 1) // tk + 1

    def step(j, carry, masked):
        m, l, acc = carry
        start = pl.multiple_of(j * tk, tk)
        k = k_ref[pl.ds(start, tk), :]
        v = v_ref[pl.ds(start, tk), :]
        s = lax.dot_general(q, k, _NT, preferred_element_type=F32) * scale
        s = s + dq - dk_ref[:, pl.ds(start, tk)]
        if masked:
            kpos = start + lax.broadcasted_iota(jnp.int32, (tq, tk), 1)
            qpos = first_pos + lax.broadcasted_iota(jnp.int32, (tq, tk), 0)
            s = jnp.where(kpos <= qpos, s, MASK_VALUE)
        m_new = jnp.maximum(m, jnp.max(s, axis=-1, keepdims=True))
        alpha = jnp.exp(m - m_new)
        p = jnp.exp(s - m_new)
        l = alpha * l + jnp.sum(p, axis=-1, keepdims=True)
        acc = alpha * acc + jnp.dot(p.astype(BF16), v, preferred_element_type=F32)
        return m_new, l, acc

    init = (jnp.full((tq, 1), MASK_VALUE, F32), jnp.zeros((tq, 1), F32),
            jnp.zeros((tq, HEAD), F32))
    carry = lax.fori_loop(0, n_full, functools.partial(step, masked=False), init)
    _, l, acc = lax.fori_loop(n_full, n_all, functools.partial(step, masked=True), carry)
    gate = gate_ref[...]
    o = acc * (1.0 / l)
    o_ref[...] = (o * (gate * _sigmoid(gate))).astype(o_ref.dtype)


def _fox_mix(z, k_all, v_all, dq, dk, B, T, q_off):
    D = z.shape[1] // 2
    H = D // HEAD
    Lk = k_all.shape[1]
    tq = _pick(T, (512, 256, 128, 64))
    tk = _pick(Lk, (512, 256, 128))
    nq = T // tq
    kv = pl.BlockSpec((None, Lk, HEAD), lambda b, h, i: (b, 0, h))
    return pl.pallas_call(
        functools.partial(_fox_kernel, tq=tq, tk=tk, q_off=q_off, scale=HEAD ** -0.5),
        grid=(B, H, nq),
        in_specs=[pl.BlockSpec((tq, HEAD), lambda b, h, i: (b * nq + i, h)),
                  pl.BlockSpec((tq, HEAD), lambda b, h, i: (b * nq + i, H + h)),
                  kv, kv,
                  pl.BlockSpec((None, None, tq, 1), lambda b, h, i: (b, h, i, 0)),
                  pl.BlockSpec((None, None, 1, Lk), lambda b, h, i: (b, h, 0, 0))],
        out_specs=pl.BlockSpec((tq, HEAD), lambda b, h, i: (b * nq + i, h)),
        out_shape=jax.ShapeDtypeStruct((B * T, D), BF16),
        compiler_params=_params(("parallel", "parallel", "arbitrary")),
        name="fox_mix",
    )(z, z, k_all, v_all, dq, dk)


def _trunk(x, p, s0, past, W):
    B, T, D = x.shape
    H = D // HEAD
    depth = p.shape[0]
    n_a = W["w_in_a"].shape[0]
    xf = x.reshape(B * T, D)
    pf = p.reshape(depth, B * T, p.shape[-1])
    states = []
    for layer in range(depth):
        final = layer == depth - 1
        if layer < n_a:
            z = _norm_matmul(xf, W["g_norm_a"][layer], W["w_in_a"][layer])
            og, s_fin = _hgrn_mix(z, W["lbs"][layer], W["g_out_a"][layer], s0[layer], B, T)
            states.append(s_fin)
            w_out = W["w_out_a"][layer]
        else:
            j = layer - n_a
            z = _norm_matmul(xf, W["g_norm_b"][j], W["w_in_b"][j])
            og = _fox_mix(z, k_all, v_all, dq, dk, B, T, q_off)
            w_out = W["w_out_b"][j]
        xf = _out_ple(og, xf, pf[layer], w_out, W["g_ple"][layer], W["w_ple_gate"][layer],
                      W["w_ple_in"][layer], W["g_final"], final)
        if layer == n_a - 1:
            k_new, k_bf = _norm_matmul(xf, W["g_kv"], W["w_k"], with_bf16=True)
            v_new, v_bf = _norm_matmul(xf, W["g_kv"], W["w_v"], with_bf16=True)
            logf_new = _norm_matmul(xf, W["g_kv"], W["w_f"], bias=W["b_f"])[:, :H]
            logf_new = logf_new.reshape(B, T, H)
            k_bf = k_bf.reshape(B, T, D)
            v_bf = v_bf.reshape(B, T, D)
            if past is None:
                k_all, v_all, logf_all, q_off = k_bf, v_bf, logf_new, 0
            else:
                past_k, past_v, past_logf = past
                q_off = past_k.shape[1]
                k_all = jnp.concatenate([past_k.reshape(B, q_off, D).astype(BF16), k_bf], axis=1)
                v_all = jnp.concatenate([past_v.reshape(B, q_off, D).astype(BF16), v_bf], axis=1)
                logf_all = jnp.concatenate([past_logf.astype(F32), logf_new], axis=1)
            Lk = k_all.shape[1]
            pad = (-Lk) % LANES
            if pad:
                k_all = jnp.pad(k_all, ((0, 0), (0, pad), (0, 0)))
                v_all = jnp.pad(v_all, ((0, 0), (0, pad), (0, 0)))
                logf_all = jnp.pad(logf_all, ((0, 0), (0, pad), (0, 0)))
            rows = jnp.transpose(logf_all, (0, 2, 1)).reshape(B * H, Lk + pad)
            dcum = _cumsum_last(rows).reshape(B, H, Lk + pad)
            dk = dcum[:, :, None, :]
            dq = dcum[:, :, q_off:q_off + T, None]
    return (xf.reshape(B, T, D), jnp.stack(states), k_new.reshape(B, T, H, HEAD),
            v_new.reshape(B, T, H, HEAD), logf_new)


def kernel(x_prompt, x_sample, state_hgrn, cache_k, cache_v, cache_logf, p_prompt, p_sample,
           g_norm_a, w_in_a, lb_logits, g_out_a, w_out_a, g_kv, w_kv, b_f,
           g_norm_b, w_in_b, w_out_b, w_ple_in, g_ple, w_ple_gate, g_final):
    D = x_prompt.shape[-1]
    H = D // HEAD
    n_a = w_in_a.shape[0]
    lbs = jnp.cumsum(jax.nn.softmax(lb_logits.astype(F32), axis=0), axis=0)
    lbs = lbs - lbs[:1]
    w_f = jnp.pad(w_kv[:, 2 * D:], ((0, 0), (0, LANES - H)))
    W = dict(
        g_norm_a=g_norm_a, w_in_a=w_in_a.astype(BF16), lbs=lbs, g_out_a=g_out_a,
        w_out_a=w_out_a.astype(BF16), g_kv=g_kv, w_k=w_kv[:, :D].astype(BF16),
        w_v=w_kv[:, D:2 * D].astype(BF16), w_f=w_f.astype(BF16),
        b_f=jnp.pad(b_f, (0, LANES - H)), g_norm_b=g_norm_b, w_in_b=w_in_b.astype(BF16),
        w_out_b=w_out_b.astype(BF16), w_ple_in=w_ple_in.astype(BF16), g_ple=g_ple,
        w_ple_gate=w_ple_gate.astype(BF16), g_final=g_final)
    s0_prompt = jnp.zeros((n_a, x_prompt.shape[0], H, HEAD, HEAD), F32)
    y_p, st_p, k_p, v_p, f_p = _trunk(x_prompt, p_prompt, s0_prompt, None, W)
    y_s, st_s, k_s, v_s, f_s = _trunk(x_sample, p_sample, state_hgrn,
                                      (cache_k, cache_v, cache_logf), W)
    return (y_p, y_s, st_p, st_s, k_p, v_p, f_p, k_s, v_s, f_s)
```

```python
import functools

import jax
import jax.numpy as jnp
import numpy as np
from jax import lax
from jax.experimental import pallas as pl
from jax.experimental.pallas import tpu as pltpu

F32 = jnp.float32
BF16 = jnp.bfloat16

EPS = 1e-6
HEAD = 128
LANES = 128
CHUNK = 64
N_LEVELS = 6
MASK_VALUE = -1e30
VMEM_LIMIT = 56 * 1024 * 1024

_NT = (((1,), (1,)), ((), ()))


def _params(sem, vmem=VMEM_LIMIT):
    return pltpu.CompilerParams(dimension_semantics=sem, vmem_limit_bytes=vmem)


def _sigmoid(x):
    return 1.0 / (1.0 + jnp.exp(-x))


def _pick(n, prefs):
    for p in prefs:
        if n % p == 0:
            return p
    return n


def _nmm_kernel(*refs, log_sigmoid, with_bf16, row_chunk):
    if log_sigmoid:
        x_ref, g_ref, w_ref, b_ref = refs[:4]
        rest = refs[4:]
    else:
        x_ref, g_ref, w_ref = refs[:3]
        b_ref = None
        rest = refs[3:]
    if with_bf16:
        o_ref, ob_ref, xn_ref = rest
    else:
        o_ref, xn_ref = rest
        ob_ref = None
    tm = x_ref.shape[0]

    @pl.when(pl.program_id(1) == 0)
    def _():
        for r in range(0, tm, row_chunk):
            x = x_ref[r:r + row_chunk, :]
            ms = jnp.mean(x * x, axis=-1, keepdims=True)
            xn_ref[r:r + row_chunk, :] = (x * lax.rsqrt(ms + EPS) * g_ref[...]).astype(BF16)

    z = jnp.dot(xn_ref[...], w_ref[...], preferred_element_type=F32)
    if log_sigmoid:
        z = z + b_ref[...]
        z = jnp.minimum(z, 0.0) - jnp.log(1.0 + jnp.exp(-jnp.abs(z)))
    o_ref[...] = z
    if with_bf16:
        ob_ref[...] = z.astype(BF16)


def _norm_matmul(x, g, w, bias=None, with_bf16=False):
    T, D = x.shape
    N = w.shape[1]
    tm = _pick(T, (1024, 512, 256, 128, 64))
    tn = _pick(N, (512, 256, 128))
    row_chunk = min(tm, 256)
    in_specs = [
        pl.BlockSpec((tm, D), lambda i, j: (i, 0)),
        pl.BlockSpec((1, D), lambda i, j: (0, 0)),
        pl.BlockSpec((D, tn), lambda i, j: (0, j)),
    ]
    args = [x, g.reshape(1, D).astype(F32), w]
    if bias is not None:
        in_specs.append(pl.BlockSpec((1, tn), lambda i, j: (0, j)))
        args.append(bias.reshape(1, N).astype(F32))
    out_shape = [jax.ShapeDtypeStruct((T, N), F32)]
    out_specs = [pl.BlockSpec((tm, tn), lambda i, j: (i, j))]
    if with_bf16:
        out_shape.append(jax.ShapeDtypeStruct((T, N), BF16))
        out_specs.append(pl.BlockSpec((tm, tn), lambda i, j: (i, j)))
    out = pl.pallas_call(
        functools.partial(_nmm_kernel, log_sigmoid=bias is not None, with_bf16=with_bf16,
                          row_chunk=row_chunk),
        grid=(T // tm, N // tn),
        in_specs=in_specs,
        out_specs=out_specs,
        out_shape=out_shape,
        scratch_shapes=[pltpu.VMEM((tm, D), BF16)],
        compiler_params=_params(("parallel", "arbitrary")),
        name="norm_matmul",
    )(*args)
    return out if with_bf16 else out[0]


def _hgrn_constants():
    C = CHUNK
    t = np.arange(C)[:, None]
    u = np.arange(C)[None, :]
    blocks = [(u <= t), (u > t)]
    masks = []
    for lvl in range(N_LEVELS):
        h = C >> (lvl + 1)
        mid = (t // (2 * h)) * (2 * h) + h
        second = t >= mid
        blocks.append(np.where(second, (u >= mid) & (u <= t), (u > t) & (u < mid)))
        mid_s = (u // (2 * h)) * (2 * h) + h
        masks.append((t // (2 * h) == u // (2 * h)) & (t >= mid) & (u < mid_s))
    masks.append(t == u)
    sm = np.concatenate(blocks, axis=0).astype(np.float32)
    mk = np.stack(masks, axis=0).astype(np.float32)
    return jnp.asarray(sm, BF16), jnp.asarray(mk, F32)


def _hgrn_kernel(zq_ref, zf_ref, zi_ref, zg_ref, lb_ref, go_ref, s0_ref, sm_ref, mk_ref,
                 o_ref, sout_ref, st_ref, *, n_chunks):
    C = CHUNK
    tb = pl.program_id(2)

    @pl.when(tb == 0)
    def _():
        st_ref[...] = s0_ref[...].T

    lb = lb_ref[...]
    go = go_ref[...]

    def chunk(c, carry):
        r0 = pl.multiple_of(c * C, C)
        zq = zq_ref[pl.ds(r0, C), :]
        zf = zf_ref[pl.ds(r0, C), :]
        v = zi_ref[pl.ds(r0, C), :]
        zg = zg_ref[pl.ds(r0, C), :]
        q = zq * _sigmoid(zq)
        f = lb + (1.0 - lb) * _sigmoid(zf)
        g = jnp.log(f)
        k = 1.0 - f
        g_hi = g.astype(BF16)
        g_lo = (g - g_hi.astype(F32)).astype(BF16)
        d2 = jnp.dot(sm_ref[...], jnp.concatenate([g_hi, g_lo], axis=1),
                     preferred_element_type=F32)
        e = jnp.exp(d2[:, :HEAD] + d2[:, HEAD:])
        qb = q.astype(BF16)
        kb = k.astype(BF16)
        vb = v.astype(BF16)
        a = mk_ref[N_LEVELS] * lax.dot_general(qb, kb, _NT, preferred_element_type=F32)
        for lvl in range(N_LEVELS):
            el = e[(2 + lvl) * C:(3 + lvl) * C]
            ql = (q * el).astype(BF16)
            kl = (k * el).astype(BF16)
            a = a + mk_ref[lvl] * lax.dot_general(ql, kl, _NT, preferred_element_type=F32)
        st = st_ref[...]
        qg = (q * e[0:C]).astype(BF16)
        o = jnp.dot(a.astype(BF16), vb, preferred_element_type=F32)
        o = o + lax.dot_general(qg, st.astype(BF16), _NT, preferred_element_type=F32)
        kg = (k * e[C:2 * C]).astype(BF16)
        st_ref[...] = st * e[C - 1:C, :] + jnp.dot(v.T.astype(BF16), kg,
                                                   preferred_element_type=F32)
        ms = jnp.mean(o * o, axis=-1, keepdims=True)
        og = o * lax.rsqrt(ms + EPS) * go * (zg * _sigmoid(zg))
        o_ref[pl.ds(r0, C), :] = og.astype(o_ref.dtype)
        return carry

    lax.fori_loop(0, n_chunks, chunk, 0)

    @pl.when(tb == pl.num_programs(2) - 1)
    def _():
        sout_ref[...] = st_ref[...].T


def _hgrn_mix(z, lb, g_out, s0, B, T):
    D = z.shape[1] // 4
    H = D // HEAD
    tb = _pick(T, (512, 256, 128, 64))
    nt = T // tb
    sm, mk = _hgrn_constants()

    def zspec(part):
        return pl.BlockSpec((tb, HEAD), lambda b, h, t: (b * nt + t, part * H + h))

    head_vec = pl.BlockSpec((1, HEAD), lambda b, h, t: (0, h))
    state = pl.BlockSpec((None, None, HEAD, HEAD), lambda b, h, t: (b, h, 0, 0))
    og, s_fin = pl.pallas_call(
        functools.partial(_hgrn_kernel, n_chunks=tb // CHUNK),
        grid=(B, H, nt),
        in_specs=[zspec(0), zspec(1), zspec(2), zspec(3), head_vec, head_vec, state,
                  pl.BlockSpec(sm.shape, lambda b, h, t: (0, 0)),
                  pl.BlockSpec(mk.shape, lambda b, h, t: (0, 0, 0))],
        out_specs=[pl.BlockSpec((tb, HEAD), lambda b, h, t: (b * nt + t, h)), state],
        out_shape=[jax.ShapeDtypeStruct((B * T, D), BF16),
                   jax.ShapeDtypeStruct((B, H, HEAD, HEAD), F32)],
        scratch_shapes=[pltpu.VMEM((HEAD, HEAD), F32)],
        compiler_params=_params(("parallel", "parallel", "arbitrary")),
        name="hgrn_mix",
    )(z, z, z, z, lb.reshape(1, D), g_out.reshape(1, D).astype(F32), s0, sm, mk)
    return og, s_fin


def _out_ple_kernel(og_ref, x_ref, p_ref, wo_ref, gpg_ref, wpg_ref, wpin_ref, gfin_ref, o_ref,
                    *, final, col_chunk):
    D = x_ref.shape[1]
    y = x_ref[...] + jnp.dot(og_ref[...], wo_ref[...], preferred_element_type=F32)
    ms = jnp.mean(y * y, axis=-1, keepdims=True)
    yn = (y * lax.rsqrt(ms + EPS) * gpg_ref[...]).astype(BF16)
    pb = p_ref[...].astype(BF16)
    parts = []
    for c0 in range(0, D, col_chunk):
        gate = _sigmoid(jnp.dot(yn, wpg_ref[:, c0:c0 + col_chunk], preferred_element_type=F32))
        emb = jnp.dot(pb, wpin_ref[:, c0:c0 + col_chunk], preferred_element_type=F32)
        parts.append(y[:, c0:c0 + col_chunk] + gate * emb)
    if final:
        ms2 = sum(jnp.sum(t * t, axis=-1, keepdims=True) for t in parts) * (1.0 / D)
        inv = lax.rsqrt(ms2 + EPS)
        for n, c0 in enumerate(range(0, D, col_chunk)):
            o_ref[:, c0:c0 + col_chunk] = parts[n] * inv * gfin_ref[:, c0:c0 + col_chunk]
    else:
        for n, c0 in enumerate(range(0, D, col_chunk)):
            o_ref[:, c0:c0 + col_chunk] = parts[n]


def _out_ple(og, x, p, w_out, g_pg, w_pg, w_pin, g_final, final):
    T, D = x.shape
    P = p.shape[1]
    tm = _pick(T, (256, 128, 64))
    col_chunk = min(D, 512)
    rows = lambda i: (i, 0)
    fixed = lambda i: (0, 0)
    once = pl.Buffered(1)
    return pl.pallas_call(
        functools.partial(_out_ple_kernel, final=final, col_chunk=col_chunk),
        grid=(T // tm,),
        in_specs=[pl.BlockSpec((tm, D), rows), pl.BlockSpec((tm, D), rows),
                  pl.BlockSpec((tm, P), rows),
                  pl.BlockSpec((D, D), fixed, pipeline_mode=once),
                  pl.BlockSpec((1, D), fixed),
                  pl.BlockSpec((D, D), fixed, pipeline_mode=once),
                  pl.BlockSpec((P, D), fixed, pipeline_mode=once),
                  pl.BlockSpec((1, D), fixed)],
        out_specs=pl.BlockSpec((tm, D), rows),
        out_shape=jax.ShapeDtypeStruct((T, D), F32),
        compiler_params=_params(("parallel",)),
        name="out_ple",
    )(og, x, p, w_out, g_pg.reshape(1, D).astype(F32), w_pg, w_pin,
      g_final.reshape(1, D).astype(F32))


def _cumsum_kernel(x_ref, u_ref, o_ref, carry_ref, *, n_sub):
    @pl.when(pl.program_id(0) == 0)
    def _():
        carry_ref[...] = jnp.zeros_like(carry_ref)

    u = u_ref[...]
    c = carry_ref[:, 0:1]
    for s in range(n_sub):
        x = x_ref[:, s * LANES:(s + 1) * LANES]
        hi = x.astype(BF16)
        r1 = x - hi.astype(F32)
        mid = r1.astype(BF16)
        lo = (r1 - mid.astype(F32)).astype(BF16)
        cs = (jnp.dot(hi, u, preferred_element_type=F32)
              + jnp.dot(mid, u, preferred_element_type=F32)
              + jnp.dot(lo, u, preferred_element_type=F32)) + c
        o_ref[:, s * LANES:(s + 1) * LANES] = cs
        c = cs[:, LANES - 1:LANES]
    carry_ref[...] = jnp.broadcast_to(c, carry_ref.shape)


def _cumsum_last(x):
    R, L = x.shape
    cb = _pick(L, (2048, 1024, 512, 256, 128))
    u = jnp.asarray(np.triu(np.ones((LANES, LANES), np.float32)), BF16)
    return pl.pallas_call(
        functools.partial(_cumsum_kernel, n_sub=cb // LANES),
        grid=(L // cb,),
        in_specs=[pl.BlockSpec((R, cb), lambda j: (0, j)),
                  pl.BlockSpec((LANES, LANES), lambda j: (0, 0))],
        out_specs=pl.BlockSpec((R, cb), lambda j: (0, j)),
        out_shape=jax.ShapeDtypeStruct((R, L), F32),
        scratch_shapes=[pltpu.VMEM((R, LANES), F32)],
        compiler_params=_params(("arbitrary",)),
        name="cumsum",
    )(x, u)


def _fox_kernel(q_ref, gate_ref, k_ref, v_ref, dq_ref, dk_ref, o_ref, *, tq, tk, q_off, scale):
    i = pl.program_id(2)
    q = q_ref[...].astype(BF16)
    dq = dq_ref[...]
    first_pos = q_off + i * tq
    n_full = (first_pos + 1) // tk
    n_all = (first_pos + tq - 1) // tk + 1

    def step(j, carry, masked):
        m, l, acc = carry
        start = pl.multiple_of(j * tk, tk)
        k = k_ref[pl.ds(start, tk), :]
        v = v_ref[pl.ds(start, tk), :]
        s = lax.dot_general(q, k, _NT, preferred_element_type=F32) * scale
        s = s + dq - dk_ref[:, pl.ds(start, tk)]
        if masked:
            kpos = start + lax.broadcasted_iota(jnp.int32, (tq, tk), 1)
            qpos = first_pos + lax.broadcasted_iota(jnp.int32, (tq, tk), 0)
            s = jnp.where(kpos <= qpos, s, MASK_VALUE)
        m_new = jnp.maximum(m, jnp.max(s, axis=-1, keepdims=True))
        alpha = jnp.exp(m - m_new)
        p = jnp.exp(s - m_new)
        l = alpha * l + jnp.sum(p, axis=-1, keepdims=True)
        acc = alpha * acc + jnp.dot(p.astype(BF16), v, preferred_element_type=F32)
        return m_new, l, acc

    init = (jnp.full((tq, 1), MASK_VALUE, F32), jnp.zeros((tq, 1), F32),
            jnp.zeros((tq, HEAD), F32))
    carry = lax.fori_loop(0, n_full, functools.partial(step, masked=False), init)
    _, l, acc = lax.fori_loop(n_full, n_all, functools.partial(step, masked=True), carry)
    gate = gate_ref[...]
    o = acc * (1.0 / l)
    o_ref[...] = (o * (gate * _sigmoid(gate))).astype(o_ref.dtype)


def _fox_mix(z, k_all, v_all, dq, dk, B, T, q_off):
    D = z.shape[1] // 2
    H = D // HEAD
    Lk = k_all.shape[1]
    tq = _pick(T, (512, 256, 128, 64))
    tk = Lk if Lk <= 2048 else _pick(Lk, (512, 256, 128))
    nq = T // tq
    kv = pl.BlockSpec((None, Lk, HEAD), lambda b, h, i: (b, 0, h))
    return pl.pallas_call(
        functools.partial(_fox_kernel, tq=tq, tk=tk, q_off=q_off, scale=HEAD ** -0.5),
        grid=(B, H, nq),
        in_specs=[pl.BlockSpec((tq, HEAD), lambda b, h, i: (b * nq + i, h)),
                  pl.BlockSpec((tq, HEAD), lambda b, h, i: (b * nq + i, H + h)),
                  kv, kv,
                  pl.BlockSpec((None, None, tq, 1), lambda b, h, i: (b, h, i, 0)),
                  pl.BlockSpec((None, None, 1, Lk), lambda b, h, i: (b, h, 0, 0))],
        out_specs=pl.BlockSpec((tq, HEAD), lambda b, h, i: (b * nq + i, h)),
        out_shape=jax.ShapeDtypeStruct((B * T, D), BF16),
        compiler_params=_params(("parallel", "parallel", "arbitrary")),
        name="fox_mix",
    )(z, z, k_all, v_all, dq, dk)


FOX_BOUND_MARGIN = 1.02
FOX_BOUND_LIMIT = 32.0
FOX_FAST_TILE = 1024
AUG = 6


def _top16(x):
    bits = lax.bitcast_convert_type(x, jnp.uint32) & jnp.uint32(0xFFFF0000)
    return lax.bitcast_convert_type(bits, F32)


def _split3(x):
    hi = _top16(x)
    r1 = x - hi
    mid = _top16(r1)
    lo = r1 - mid
    return hi.astype(BF16), mid.astype(BF16), lo.astype(BF16)


def _sq_sums_kernel(x_ref, e_ref, o_ref):
    x = x_ref[...]
    o_ref[...] = jnp.dot((x * x).astype(BF16), e_ref[...], preferred_element_type=F32)


def _head_sq_sums(x, D):
    T = x.shape[0]
    H = D // HEAD
    tm = _pick(T, (512, 256, 128, 64))
    e = np.zeros((D, LANES), np.float32)
    e[np.arange(D), np.arange(D) // HEAD] = 1.0
    return pl.pallas_call(
        _sq_sums_kernel,
        grid=(T // tm,),
        in_specs=[pl.BlockSpec((tm, D), lambda i: (i, 0)),
                  pl.BlockSpec((D, LANES), lambda i: (0, 0))],
        out_specs=pl.BlockSpec((tm, LANES), lambda i: (i, 0)),
        out_shape=jax.ShapeDtypeStruct((T, LANES), F32),
        compiler_params=_params(("parallel",)),
        name="head_sq_sums",
    )(x, jnp.asarray(e, BF16))[:, :H]


def _fox_fast_kernel(q_ref, c_ref, gate_ref, ka_ref, v_ref, o_ref, qa_ref, *, tile, scale):
    i = pl.program_id(2)
    qa_ref[:, :HEAD] = (q_ref[...] * scale).astype(BF16)
    lane = lax.broadcasted_iota(jnp.int32, (tile, HEAD), 1)
    hi, mid, lo = _split3(c_ref[...])
    aug = jnp.where(lane == 0, hi.astype(F32), jnp.where(lane == 1, mid.astype(F32),
          jnp.where(lane == 2, lo.astype(F32), jnp.where(lane < AUG, 1.0, 0.0))))
    qa_ref[:, HEAD:] = aug.astype(BF16)

    def scores(j, masked):
        start = pl.multiple_of(j * tile, tile)
        s = lax.dot_general(qa_ref[...], ka_ref[pl.ds(start, tile), :], _NT,
                            preferred_element_type=F32)
        if masked:
            kpos = lax.broadcasted_iota(jnp.int32, (tile, tile), 1)
            qpos = lax.broadcasted_iota(jnp.int32, (tile, tile), 0)
            s = jnp.where(kpos <= qpos, s, MASK_VALUE)
        return s

    def consume(s, j, lp, acc):
        start = pl.multiple_of(j * tile, tile)
        p = jnp.exp(s)
        for g in range(tile // LANES):
            lp = lp + p[:, g * LANES:(g + 1) * LANES]
        acc = acc + jnp.dot(p.astype(BF16), v_ref[pl.ds(start, tile), :],
                            preferred_element_type=F32)
        return lp, acc

    def pair(j0, carry, masked_b):
        lp, acc = carry
        sa = scores(j0, False)
        sb = scores(j0 + 1, masked_b)
        lp, acc = consume(sa, j0, lp, acc)
        return consume(sb, j0 + 1, lp, acc)

    init = (jnp.zeros((tile, LANES), F32), jnp.zeros((tile, HEAD), F32))
    carry = lax.fori_loop(0, i // 2, lambda t, c: pair(2 * t, c, False), init)
    lp, acc = lax.cond(i % 2 == 1, lambda c: pair(i - 1, c, True),
                       lambda c: consume(scores(i, True), i, *c), carry)
    gate = gate_ref[...]
    o = acc * (1.0 / jnp.sum(lp, axis=-1, keepdims=True))
    o_ref[...] = (o * (gate * _sigmoid(gate))).astype(o_ref.dtype)


def _fox_mix_fast(z, c, k_aug, v_all, B, T):
    D = z.shape[1] // 2
    H = D // HEAD
    tile = FOX_FAST_TILE
    nq = T // tile
    return pl.pallas_call(
        functools.partial(_fox_fast_kernel, tile=tile, scale=HEAD ** -0.5),
        grid=(B, H, nq),
        in_specs=[pl.BlockSpec((tile, HEAD), lambda b, h, i: (b * nq + i, h)),
                  pl.BlockSpec((None, None, tile, 1), lambda b, h, i: (b, h, i, 0)),
                  pl.BlockSpec((tile, HEAD), lambda b, h, i: (b * nq + i, H + h)),
                  pl.BlockSpec((None, T, 2 * HEAD), lambda b, h, i: (b, 0, h)),
                  pl.BlockSpec((None, T, HEAD), lambda b, h, i: (b, 0, h))],
        out_specs=pl.BlockSpec((tile, HEAD), lambda b, h, i: (b * nq + i, h)),
        out_shape=jax.ShapeDtypeStruct((B * T, D), BF16),
        scratch_shapes=[pltpu.VMEM((tile, 2 * HEAD), BF16)],
        compiler_params=_params(("parallel", "parallel", "arbitrary")),
        name="fox_mix_fast",
    )(z, c, z, k_aug, v_all)


def _augment_keys(k_bf, dcum, B, Lk, H):
    hi, mid, lo = _split3(-jnp.transpose(dcum, (0, 2, 1)))
    one = jnp.ones_like(hi)
    aug = jnp.stack([one, one, one, hi, mid, lo], axis=-1)
    aug = jnp.pad(aug, ((0, 0), (0, 0), (0, 0), (0, HEAD - AUG)))
    return jnp.concatenate([k_bf.reshape(B, Lk, H, HEAD), aug], axis=-1).reshape(B, Lk, 2 * H * HEAD)


def _trunk(x, p, s0, past, W):
    B, T, D = x.shape
    H = D // HEAD
    depth = p.shape[0]
    n_a = W["w_in_a"].shape[0]
    xf = x.reshape(B * T, D)
    pf = p.reshape(depth, B * T, p.shape[-1])
    states = []
    for layer in range(depth):
        final = layer == depth - 1
        if layer < n_a:
            z = _norm_matmul(xf, W["g_norm_a"][layer], W["w_in_a"][layer])
            og, s_fin = _hgrn_mix(z, W["lbs"][layer], W["g_out_a"][layer], s0[layer], B, T)
            states.append(s_fin)
            w_out = W["w_out_a"][layer]
        else:
            j = layer - n_a
            z = _norm_matmul(xf, W["g_norm_b"][j], W["w_in_b"][j])
            if k_aug is None:
                og = _fox_mix(z, k_all, v_all, dq, dk, B, T, q_off)
            else:
                qn = jnp.sqrt(_head_sq_sums(z, D)).reshape(B, T, H)
                bound = (FOX_BOUND_MARGIN * HEAD ** -0.5) * jnp.transpose(qn, (0, 2, 1)) * kmax
                c = dq - bound[..., None]
                og = lax.cond(jnp.max(bound) <= FOX_BOUND_LIMIT,
                              lambda: _fox_mix_fast(z, c, k_aug, v_all, B, T),
                              lambda: _fox_mix(z, k_all, v_all, dq, dk, B, T, q_off))
            w_out = W["w_out_b"][j]
        xf = _out_ple(og, xf, pf[layer], w_out, W["g_ple"][layer], W["w_ple_gate"][layer],
                      W["w_ple_in"][layer], W["g_final"], final)
        if layer == n_a - 1:
            k_new, k_bf = _norm_matmul(xf, W["g_kv"], W["w_k"], with_bf16=True)
            v_new, v_bf = _norm_matmul(xf, W["g_kv"], W["w_v"], with_bf16=True)
            logf_new = _norm_matmul(xf, W["g_kv"], W["w_f"], bias=W["b_f"])[:, :H]
            logf_new = logf_new.reshape(B, T, H)
            k_bf = k_bf.reshape(B, T, D)
            v_bf = v_bf.reshape(B, T, D)
            if past is None:
                k_all, v_all, logf_all, q_off = k_bf, v_bf, logf_new, 0
            else:
                past_k, past_v, past_logf = past
                q_off = past_k.shape[1]
                k_all = jnp.concatenate([past_k.reshape(B, q_off, D).astype(BF16), k_bf], axis=1)
                v_all = jnp.concatenate([past_v.reshape(B, q_off, D).astype(BF16), v_bf], axis=1)
                logf_all = jnp.concatenate([past_logf.astype(F32), logf_new], axis=1)
            Lk = k_all.shape[1]
            pad = (-Lk) % LANES
            if pad:
                k_all = jnp.pad(k_all, ((0, 0), (0, pad), (0, 0)))
                v_all = jnp.pad(v_all, ((0, 0), (0, pad), (0, 0)))
                logf_all = jnp.pad(logf_all, ((0, 0), (0, pad), (0, 0)))
            rows = jnp.transpose(logf_all, (0, 2, 1)).reshape(B * H, Lk + pad)
            dcum = _cumsum_last(rows).reshape(B, H, Lk + pad)
            dk = dcum[:, :, None, :]
            dq = dcum[:, :, q_off:q_off + T, None]
            k_aug = None
            if past is None and T % FOX_FAST_TILE == 0:
                kn = jnp.sqrt(_head_sq_sums(k_new, D)).reshape(B, T, H)
                kmax = jnp.max(kn, axis=1)[:, :, None]
                k_aug = _augment_keys(k_bf, dcum, B, T, H)
    return (xf.reshape(B, T, D), jnp.stack(states), k_new.reshape(B, T, H, HEAD),
            v_new.reshape(B, T, H, HEAD), logf_new)


def kernel(x_prompt, x_sample, state_hgrn, cache_k, cache_v, cache_logf, p_prompt, p_sample,
           g_norm_a, w_in_a, lb_logits, g_out_a, w_out_a, g_kv, w_kv, b_f,
           g_norm_b, w_in_b, w_out_b, w_ple_in, g_ple, w_ple_gate, g_final):
    D = x_prompt.shape[-1]
    H = D // HEAD
    n_a = w_in_a.shape[0]
    lbs = jnp.cumsum(jax.nn.softmax(lb_logits.astype(F32), axis=0), axis=0)
    lbs = lbs - lbs[:1]
    w_f = jnp.pad(w_kv[:, 2 * D:], ((0, 0), (0, LANES - H)))
    W = dict(
        g_norm_a=g_norm_a, w_in_a=w_in_a.astype(BF16), lbs=lbs, g_out_a=g_out_a,
        w_out_a=w_out_a.astype(BF16), g_kv=g_kv, w_k=w_kv[:, :D].astype(BF16),
        w_v=w_kv[:, D:2 * D].astype(BF16), w_f=w_f.astype(BF16),
        b_f=jnp.pad(b_f, (0, LANES - H)), g_norm_b=g_norm_b, w_in_b=w_in_b.astype(BF16),
        w_out_b=w_out_b.astype(BF16), w_ple_in=w_ple_in.astype(BF16), g_ple=g_ple,
        w_ple_gate=w_ple_gate.astype(BF16), g_final=g_final)
    s0_prompt = jnp.zeros((n_a, x_prompt.shape[0], H, HEAD, HEAD), F32)
    y_p, st_p, k_p, v_p, f_p = _trunk(x_prompt, p_prompt, s0_prompt, None, W)
    y_s, st_s, k_s, v_s, f_s = _trunk(x_sample, p_sample, state_hgrn,
                                      (cache_k, cache_v, cache_logf), W)
    return (y_p, y_s, st_p, st_s, k_p, v_p, f_p, k_s, v_s, f_s)
```

```python
import functools

import jax
import jax.numpy as jnp
import numpy as np
from jax import lax
from jax.experimental import pallas as pl
from jax.experimental.pallas import tpu as pltpu

F32 = jnp.float32
BF16 = jnp.bfloat16

EPS = 1e-6
HEAD = 128
LANES = 128
CHUNK = 64
N_LEVELS = 6
HGRN_HEADS = 8
MASK_VALUE = -1e30
VMEM_LIMIT = 56 * 1024 * 1024

_NT = (((1,), (1,)), ((), ()))


def _params(sem, vmem=VMEM_LIMIT):
    return pltpu.CompilerParams(dimension_semantics=sem, vmem_limit_bytes=vmem)


def _sigmoid(x):
    return 1.0 / (1.0 + jnp.exp(-x))


def _pick(n, prefs):
    for p in prefs:
        if n % p == 0:
            return p
    return n


def _nmm_kernel(*refs, log_sigmoid, with_bf16, row_chunk):
    if log_sigmoid:
        x_ref, g_ref, w_ref, b_ref = refs[:4]
        rest = refs[4:]
    else:
        x_ref, g_ref, w_ref = refs[:3]
        b_ref = None
        rest = refs[3:]
    if with_bf16:
        o_ref, ob_ref, xn_ref = rest
    else:
        o_ref, xn_ref = rest
        ob_ref = None
    tm = x_ref.shape[0]

    @pl.when(pl.program_id(1) == 0)
    def _():
        for r in range(0, tm, row_chunk):
            x = x_ref[r:r + row_chunk, :]
            ms = jnp.mean(x * x, axis=-1, keepdims=True)
            xn_ref[r:r + row_chunk, :] = (x * lax.rsqrt(ms + EPS) * g_ref[...]).astype(BF16)

    z = jnp.dot(xn_ref[...], w_ref[...], preferred_element_type=F32)
    if log_sigmoid:
        z = z + b_ref[...]
        z = jnp.minimum(z, 0.0) - jnp.log(1.0 + jnp.exp(-jnp.abs(z)))
    o_ref[...] = z
    if with_bf16:
        ob_ref[...] = z.astype(BF16)


def _norm_matmul(x, g, w, bias=None, with_bf16=False):
    T, D = x.shape
    N = w.shape[1]
    tm = _pick(T, (1024, 512, 256, 128, 64))
    tn = _pick(N, (512, 256, 128))
    row_chunk = min(tm, 256)
    in_specs = [
        pl.BlockSpec((tm, D), lambda i, j: (i, 0)),
        pl.BlockSpec((1, D), lambda i, j: (0, 0)),
        pl.BlockSpec((D, tn), lambda i, j: (0, j)),
    ]
    args = [x, g.reshape(1, D).astype(F32), w]
    if bias is not None:
        in_specs.append(pl.BlockSpec((1, tn), lambda i, j: (0, j)))
        args.append(bias.reshape(1, N).astype(F32))
    out_shape = [jax.ShapeDtypeStruct((T, N), F32)]
    out_specs = [pl.BlockSpec((tm, tn), lambda i, j: (i, j))]
    if with_bf16:
        out_shape.append(jax.ShapeDtypeStruct((T, N), BF16))
        out_specs.append(pl.BlockSpec((tm, tn), lambda i, j: (i, j)))
    out = pl.pallas_call(
        functools.partial(_nmm_kernel, log_sigmoid=bias is not None, with_bf16=with_bf16,
                          row_chunk=row_chunk),
        grid=(T // tm, N // tn),
        in_specs=in_specs,
        out_specs=out_specs,
        out_shape=out_shape,
        scratch_shapes=[pltpu.VMEM((tm, D), BF16)],
        compiler_params=_params(("parallel", "arbitrary")),
        name="norm_matmul",
    )(*args)
    return out if with_bf16 else out[0]


def _hgrn_constants():
    C = CHUNK
    t = np.arange(C)[:, None]
    u = np.arange(C)[None, :]
    blocks = [(u <= t), (u > t)]
    masks = []
    for lvl in range(N_LEVELS):
        h = C >> (lvl + 1)
        mid = (t // (2 * h)) * (2 * h) + h
        second = t >= mid
        blocks.append(np.where(second, (u >= mid) & (u <= t), (u > t) & (u < mid)))
        mid_s = (u // (2 * h)) * (2 * h) + h
        masks.append((t // (2 * h) == u // (2 * h)) & (t >= mid) & (u < mid_s))
    masks.append(t == u)
    sm = np.concatenate(blocks, axis=0).astype(np.float32)
    mk = np.stack(masks, axis=0).astype(np.float32)
    return jnp.asarray(sm, BF16), jnp.asarray(mk, F32)


def _hgrn_kernel(zq_ref, zf_ref, zi_ref, zg_ref, lb_ref, go_ref, s0_ref, sm_ref, mk_ref,
                 o_ref, sout_ref, st_ref, *, n_chunks, n_heads):
    C = CHUNK
    tb = pl.program_id(2)

    @pl.when(tb == 0)
    def _():
        for hh in range(n_heads):
            st_ref[hh] = s0_ref[hh].T

    def chunk(c, carry):
        r0 = pl.multiple_of(c * C, C)
        heads = range(n_heads)
        cols = [slice(hh * HEAD, (hh + 1) * HEAD) for hh in heads]
        q, k, v, d2 = [], [], [], []
        for hh in heads:
            lb = lb_ref[:, cols[hh]]
            zq = zq_ref[pl.ds(r0, C), cols[hh]]
            zf = zf_ref[pl.ds(r0, C), cols[hh]]
            q.append(zq * _sigmoid(zq))
            f = lb + (1.0 - lb) * _sigmoid(zf)
            g = jnp.log(f)
            k.append(1.0 - f)
            v.append(zi_ref[pl.ds(r0, C), cols[hh]])
            g_hi = g.astype(BF16)
            g_lo = (g - g_hi.astype(F32)).astype(BF16)
            d2.append(jnp.dot(sm_ref[...], jnp.concatenate([g_hi, g_lo], axis=1),
                              preferred_element_type=F32))
        e = [jnp.exp(d[:, :HEAD] + d[:, HEAD:]) for d in d2]
        a = []
        for hh in heads:
            parts = [lax.dot_general(q[hh].astype(BF16), k[hh].astype(BF16), _NT,
                                     preferred_element_type=F32)]
            for lvl in range(N_LEVELS):
                el = e[hh][(2 + lvl) * C:(3 + lvl) * C]
                parts.append(lax.dot_general((q[hh] * el).astype(BF16), (k[hh] * el).astype(BF16),
                                             _NT, preferred_element_type=F32))
            acc = mk_ref[N_LEVELS] * parts[0]
            for lvl in range(N_LEVELS):
                acc = acc + mk_ref[lvl] * parts[1 + lvl]
            a.append(acc)
        o = []
        for hh in heads:
            st = st_ref[hh]
            qg = (q[hh] * e[hh][0:C]).astype(BF16)
            kg = (k[hh] * e[hh][C:2 * C]).astype(BF16)
            vb = v[hh].astype(BF16)
            oh = jnp.dot(a[hh].astype(BF16), vb, preferred_element_type=F32)
            o.append(oh + lax.dot_general(qg, st.astype(BF16), _NT, preferred_element_type=F32))
            st_ref[hh] = st * e[hh][C - 1:C, :] + jnp.dot(v[hh].T.astype(BF16), kg,
                                                          preferred_element_type=F32)
        for hh in heads:
            zg = zg_ref[pl.ds(r0, C), cols[hh]]
            ms = jnp.mean(o[hh] * o[hh], axis=-1, keepdims=True)
            og = o[hh] * lax.rsqrt(ms + EPS) * go_ref[:, cols[hh]] * (zg * _sigmoid(zg))
            o_ref[pl.ds(r0, C), cols[hh]] = og.astype(o_ref.dtype)
        return carry

    lax.fori_loop(0, n_chunks, chunk, 0)

    @pl.when(tb == pl.num_programs(2) - 1)
    def _():
        for hh in range(n_heads):
            sout_ref[hh] = st_ref[hh].T


def _hgrn_mix(z, lb, g_out, s0, B, T):
    D = z.shape[1] // 4
    H = D // HEAD
    hb = _pick(H, (HGRN_HEADS,))
    tb = _pick(T, (512, 256, 128, 64))
    nt = T // tb
    nh = H // hb
    sm, mk = _hgrn_constants()

    def zspec(part):
        return pl.BlockSpec((tb, hb * HEAD), lambda b, h, t: (b * nt + t, part * nh + h))

    head_vec = pl.BlockSpec((1, hb * HEAD), lambda b, h, t: (0, h))
    state = pl.BlockSpec((None, hb, HEAD, HEAD), lambda b, h, t: (b, h, 0, 0))
    og, s_fin = pl.pallas_call(
        functools.partial(_hgrn_kernel, n_chunks=tb // CHUNK, n_heads=hb),
        grid=(B, nh, nt),
        in_specs=[zspec(0), zspec(1), zspec(2), zspec(3), head_vec, head_vec, state,
                  pl.BlockSpec(sm.shape, lambda b, h, t: (0, 0)),
                  pl.BlockSpec(mk.shape, lambda b, h, t: (0, 0, 0))],
        out_specs=[pl.BlockSpec((tb, hb * HEAD), lambda b, h, t: (b * nt + t, h)), state],
        out_shape=[jax.ShapeDtypeStruct((B * T, D), BF16),
                   jax.ShapeDtypeStruct((B, H, HEAD, HEAD), F32)],
        scratch_shapes=[pltpu.VMEM((hb, HEAD, HEAD), F32)],
        compiler_params=_params(("parallel", "parallel", "arbitrary")),
        name="hgrn_mix",
    )(z, z, z, z, lb.reshape(1, D), g_out.reshape(1, D).astype(F32), s0, sm, mk)
    return og, s_fin


def _out_ple_kernel(og_ref, x_ref, p_ref, wo_ref, gpg_ref, wpg_ref, wpin_ref, gfin_ref, o_ref,
                    *, final, col_chunk):
    D = x_ref.shape[1]
    y = x_ref[...] + jnp.dot(og_ref[...], wo_ref[...], preferred_element_type=F32)
    ms = jnp.mean(y * y, axis=-1, keepdims=True)
    yn = (y * lax.rsqrt(ms + EPS) * gpg_ref[...]).astype(BF16)
    pb = p_ref[...].astype(BF16)
    parts = []
    for c0 in range(0, D, col_chunk):
        gate = _sigmoid(jnp.dot(yn, wpg_ref[:, c0:c0 + col_chunk], preferred_element_type=F32))
        emb = jnp.dot(pb, wpin_ref[:, c0:c0 + col_chunk], preferred_element_type=F32)
        parts.append(y[:, c0:c0 + col_chunk] + gate * emb)
    if final:
        ms2 = sum(jnp.sum(t * t, axis=-1, keepdims=True) for t in parts) * (1.0 / D)
        inv = lax.rsqrt(ms2 + EPS)
        for n, c0 in enumerate(range(0, D, col_chunk)):
            o_ref[:, c0:c0 + col_chunk] = parts[n] * inv * gfin_ref[:, c0:c0 + col_chunk]
    else:
        for n, c0 in enumerate(range(0, D, col_chunk)):
            o_ref[:, c0:c0 + col_chunk] = parts[n]


def _out_ple(og, x, p, w_out, g_pg, w_pg, w_pin, g_final, final):
    T, D = x.shape
    P = p.shape[1]
    tm = _pick(T, (256, 128, 64))
    col_chunk = min(D, 512)
    rows = lambda i: (i, 0)
    fixed = lambda i: (0, 0)
    once = pl.Buffered(1)
    return pl.pallas_call(
        functools.partial(_out_ple_kernel, final=final, col_chunk=col_chunk),
        grid=(T // tm,),
        in_specs=[pl.BlockSpec((tm, D), rows), pl.BlockSpec((tm, D), rows),
                  pl.BlockSpec((tm, P), rows),
                  pl.BlockSpec((D, D), fixed, pipeline_mode=once),
                  pl.BlockSpec((1, D), fixed),
                  pl.BlockSpec((D, D), fixed, pipeline_mode=once),
                  pl.BlockSpec((P, D), fixed, pipeline_mode=once),
                  pl.BlockSpec((1, D), fixed)],
        out_specs=pl.BlockSpec((tm, D), rows),
        out_shape=jax.ShapeDtypeStruct((T, D), F32),
        compiler_params=_params(("parallel",)),
        name="out_ple",
    )(og, x, p, w_out, g_pg.reshape(1, D).astype(F32), w_pg, w_pin,
      g_final.reshape(1, D).astype(F32))


def _cumsum_kernel(x_ref, u_ref, o_ref, carry_ref, *, n_sub):
    @pl.when(pl.program_id(0) == 0)
    def _():
        carry_ref[...] = jnp.zeros_like(carry_ref)

    u = u_ref[...]
    c = carry_ref[:, 0:1]
    for s in range(n_sub):
        x = x_ref[:, s * LANES:(s + 1) * LANES]
        hi = x.astype(BF16)
        r1 = x - hi.astype(F32)
        mid = r1.astype(BF16)
        lo = (r1 - mid.astype(F32)).astype(BF16)
        cs = (jnp.dot(hi, u, preferred_element_type=F32)
              + jnp.dot(mid, u, preferred_element_type=F32)
              + jnp.dot(lo, u, preferred_element_type=F32)) + c
        o_ref[:, s * LANES:(s + 1) * LANES] = cs
        c = cs[:, LANES - 1:LANES]
    carry_ref[...] = jnp.broadcast_to(c, carry_ref.shape)


def _cumsum_last(x):
    R, L = x.shape
    cb = _pick(L, (2048, 1024, 512, 256, 128))
    u = jnp.asarray(np.triu(np.ones((LANES, LANES), np.float32)), BF16)
    return pl.pallas_call(
        functools.partial(_cumsum_kernel, n_sub=cb // LANES),
        grid=(L // cb,),
        in_specs=[pl.BlockSpec((R, cb), lambda j: (0, j)),
                  pl.BlockSpec((LANES, LANES), lambda j: (0, 0))],
        out_specs=pl.BlockSpec((R, cb), lambda j: (0, j)),
        out_shape=jax.ShapeDtypeStruct((R, L), F32),
        scratch_shapes=[pltpu.VMEM((R, LANES), F32)],
        compiler_params=_params(("arbitrary",)),
        name="cumsum",
    )(x, u)


def _fox_kernel(q_ref, gate_ref, k_ref, v_ref, dq_ref, dk_ref, o_ref, *, tq, tk, q_off, scale):
    i = pl.program_id(2)
    q = q_ref[...].astype(BF16)
    dq = dq_ref[...]
    first_pos = q_off + i * tq
    n_full = (first_pos + 1) // tk
    n_all = (first_pos + tq - 1) // tk + 1

    def step(j, carry, masked):
        m, l, acc = carry
        start = pl.multiple_of(j * tk, tk)
        k = k_ref[pl.ds(start, tk), :]
        v = v_ref[pl.ds(start, tk), :]
        s = lax.dot_general(q, k, _NT, preferred_element_type=F32) * scale
        s = s + dq - dk_ref[:, pl.ds(start, tk)]
        if masked:
            kpos = start + lax.broadcasted_iota(jnp.int32, (tq, tk), 1)
            qpos = first_pos + lax.broadcasted_iota(jnp.int32, (tq, tk), 0)
            s = jnp.where(kpos <= qpos, s, MASK_VALUE)
        m_new = jnp.maximum(m, jnp.max(s, axis=-1, keepdims=True))
        alpha = jnp.exp(m - m_new)
        p = jnp.exp(s - m_new)
        l = alpha * l + jnp.sum(p, axis=-1, keepdims=True)
        acc = alpha * acc + jnp.dot(p.astype(BF16), v, preferred_element_type=F32)
        return m_new, l, acc

    init = (jnp.full((tq, 1), MASK_VALUE, F32), jnp.zeros((tq, 1), F32),
            jnp.zeros((tq, HEAD), F32))
    carry = lax.fori_loop(0, n_full, functools.partial(step, masked=False), init)
    _, l, acc = lax.fori_loop(n_full, n_all, functools.partial(step, masked=True), carry)
    gate = gate_ref[...]
    o = acc * (1.0 / l)
    o_ref[...] = (o * (gate * _sigmoid(gate))).astype(o_ref.dtype)


def _fox_mix(z, k_all, v_all, dq, dk, B, T, q_off):
    D = z.shape[1] // 2
    H = D // HEAD
    Lk = k_all.shape[1]
    tq = _pick(T, (512, 256, 128, 64))
    tk = Lk if Lk <= 2048 else _pick(Lk, (512, 256, 128))
    nq = T // tq
    kv = pl.BlockSpec((None, Lk, HEAD), lambda b, h, i: (b, 0, h))
    return pl.pallas_call(
        functools.partial(_fox_kernel, tq=tq, tk=tk, q_off=q_off, scale=HEAD ** -0.5),
        grid=(B, H, nq),
        in_specs=[pl.BlockSpec((tq, HEAD), lambda b, h, i: (b * nq + i, h)),
                  pl.BlockSpec((tq, HEAD), lambda b, h, i: (b * nq + i, H + h)),
                  kv, kv,
                  pl.BlockSpec((None, None, tq, 1), lambda b, h, i: (b, h, i, 0)),
                  pl.BlockSpec((None, None, 1, Lk), lambda b, h, i: (b, h, 0, 0))],
        out_specs=pl.BlockSpec((tq, HEAD), lambda b, h, i: (b * nq + i, h)),
        out_shape=jax.ShapeDtypeStruct((B * T, D), BF16),
        compiler_params=_params(("parallel", "parallel", "arbitrary")),
        name="fox_mix",
    )(z, z, k_all, v_all, dq, dk)


FOX_BOUND_MARGIN = 1.02
FOX_BOUND_LIMIT = 32.0
FOX_FAST_TILE = 1024
AUG = 6


def _top16(x):
    bits = lax.bitcast_convert_type(x, jnp.uint32) & jnp.uint32(0xFFFF0000)
    return lax.bitcast_convert_type(bits, F32)


def _split3(x):
    hi = _top16(x)
    r1 = x - hi
    mid = _top16(r1)
    lo = r1 - mid
    return hi.astype(BF16), mid.astype(BF16), lo.astype(BF16)


def _sq_sums_kernel(x_ref, e_ref, o_ref):
    x = x_ref[...]
    o_ref[...] = jnp.dot((x * x).astype(BF16), e_ref[...], preferred_element_type=F32)


def _head_sq_sums(x, D):
    T = x.shape[0]
    H = D // HEAD
    tm = _pick(T, (512, 256, 128, 64))
    e = np.zeros((D, LANES), np.float32)
    e[np.arange(D), np.arange(D) // HEAD] = 1.0
    return pl.pallas_call(
        _sq_sums_kernel,
        grid=(T // tm,),
        in_specs=[pl.BlockSpec((tm, D), lambda i: (i, 0)),
                  pl.BlockSpec((D, LANES), lambda i: (0, 0))],
        out_specs=pl.BlockSpec((tm, LANES), lambda i: (i, 0)),
        out_shape=jax.ShapeDtypeStruct((T, LANES), F32),
        compiler_params=_params(("parallel",)),
        name="head_sq_sums",
    )(x, jnp.asarray(e, BF16))[:, :H]


def _fox_fast_kernel(q_ref, c_ref, gate_ref, ka_ref, v_ref, o_ref, qa_ref, *, tile, scale):
    i = pl.program_id(2)
    qa_ref[:, :HEAD] = (q_ref[...] * scale).astype(BF16)
    lane = lax.broadcasted_iota(jnp.int32, (tile, HEAD), 1)
    hi, mid, lo = _split3(c_ref[...])
    aug = jnp.where(lane == 0, hi.astype(F32), jnp.where(lane == 1, mid.astype(F32),
          jnp.where(lane == 2, lo.astype(F32), jnp.where(lane < AUG, 1.0, 0.0))))
    qa_ref[:, HEAD:] = aug.astype(BF16)

    def scores(j, masked):
        start = pl.multiple_of(j * tile, tile)
        s = lax.dot_general(qa_ref[...], ka_ref[pl.ds(start, tile), :], _NT,
                            preferred_element_type=F32)
        if masked:
            kpos = lax.broadcasted_iota(jnp.int32, (tile, tile), 1)
            qpos = lax.broadcasted_iota(jnp.int32, (tile, tile), 0)
            s = jnp.where(kpos <= qpos, s, MASK_VALUE)
        return s

    def consume(s, j, lp, acc):
        start = pl.multiple_of(j * tile, tile)
        p = jnp.exp(s)
        for g in range(tile // LANES):
            lp = lp + p[:, g * LANES:(g + 1) * LANES]
        acc = acc + jnp.dot(p.astype(BF16), v_ref[pl.ds(start, tile), :],
                            preferred_element_type=F32)
        return lp, acc

    def pair(j0, carry, masked_b):
        lp, acc = carry
        sa = scores(j0, False)
        sb = scores(j0 + 1, masked_b)
        lp, acc = consume(sa, j0, lp, acc)
        return consume(sb, j0 + 1, lp, acc)

    init = (jnp.zeros((tile, LANES), F32), jnp.zeros((tile, HEAD), F32))
    carry = lax.fori_loop(0, i // 2, lambda t, c: pair(2 * t, c, False), init)
    lp, acc = lax.cond(i % 2 == 1, lambda c: pair(i - 1, c, True),
                       lambda c: consume(scores(i, True), i, *c), carry)
    gate = gate_ref[...]
    o = acc * (1.0 / jnp.sum(lp, axis=-1, keepdims=True))
    o_ref[...] = (o * (gate * _sigmoid(gate))).astype(o_ref.dtype)


def _fox_mix_fast(z, c, k_aug, v_all, B, T):
    D = z.shape[1] // 2
    H = D // HEAD
    tile = FOX_FAST_TILE
    nq = T // tile
    return pl.pallas_call(
        functools.partial(_fox_fast_kernel, tile=tile, scale=HEAD ** -0.5),
        grid=(B, H, nq),
        in_specs=[pl.BlockSpec((tile, HEAD), lambda b, h, i: (b * nq + i, h)),
                  pl.BlockSpec((None, None, tile, 1), lambda b, h, i: (b, h, i, 0)),
                  pl.BlockSpec((tile, HEAD), lambda b, h, i: (b * nq + i, H + h)),
                  pl.BlockSpec((None, T, 2 * HEAD), lambda b, h, i: (b, 0, h)),
                  pl.BlockSpec((None, T, HEAD), lambda b, h, i: (b, 0, h))],
        out_specs=pl.BlockSpec((tile, HEAD), lambda b, h, i: (b * nq + i, h)),
        out_shape=jax.ShapeDtypeStruct((B * T, D), BF16),
        scratch_shapes=[pltpu.VMEM((tile, 2 * HEAD), BF16)],
        compiler_params=_params(("parallel", "parallel", "arbitrary")),
        name="fox_mix_fast",
    )(z, c, z, k_aug, v_all)


def _augment_keys(k_bf, dcum, B, Lk, H):
    hi, mid, lo = _split3(-jnp.transpose(dcum, (0, 2, 1)))
    one = jnp.ones_like(hi)
    aug = jnp.stack([one, one, one, hi, mid, lo], axis=-1)
    aug = jnp.pad(aug, ((0, 0), (0, 0), (0, 0), (0, HEAD - AUG)))
    return jnp.concatenate([k_bf.reshape(B, Lk, H, HEAD), aug], axis=-1).reshape(B, Lk, 2 * H * HEAD)


def _trunk(x, p, s0, past, W):
    B, T, D = x.shape
    H = D // HEAD
    depth = p.shape[0]
    n_a = W["w_in_a"].shape[0]
    xf = x.reshape(B * T, D)
    pf = p.reshape(depth, B * T, p.shape[-1])
    states = []
    for layer in range(depth):
        final = layer == depth - 1
        if layer < n_a:
            z = _norm_matmul(xf, W["g_norm_a"][layer], W["w_in_a"][layer])
            og, s_fin = _hgrn_mix(z, W["lbs"][layer], W["g_out_a"][layer], s0[layer], B, T)
            states.append(s_fin)
            w_out = W["w_out_a"][layer]
        else:
            j = layer - n_a
            z = _norm_matmul(xf, W["g_norm_b"][j], W["w_in_b"][j])
            if k_aug is None:
                og = _fox_mix(z, k_all, v_all, dq, dk, B, T, q_off)
            else:
                qn = jnp.sqrt(_head_sq_sums(z, D)).reshape(B, T, H)
                bound = (FOX_BOUND_MARGIN * HEAD ** -0.5) * jnp.transpose(qn, (0, 2, 1)) * kmax
                c = dq - bound[..., None]
                og = lax.cond(jnp.max(bound) <= FOX_BOUND_LIMIT,
                              lambda: _fox_mix_fast(z, c, k_aug, v_all, B, T),
                              lambda: _fox_mix(z, k_all, v_all, dq, dk, B, T, q_off))
            w_out = W["w_out_b"][j]
        xf = _out_ple(og, xf, pf[layer], w_out, W["g_ple"][layer], W["w_ple_gate"][layer],
                      W["w_ple_in"][layer], W["g_final"], final)
        if layer == n_a - 1:
            k_new, k_bf = _norm_matmul(xf, W["g_kv"], W["w_k"], with_bf16=True)
            v_new, v_bf = _norm_matmul(xf, W["g_kv"], W["w_v"], with_bf16=True)
            logf_new = _norm_matmul(xf, W["g_kv"], W["w_f"], bias=W["b_f"])[:, :H]
            logf_new = logf_new.reshape(B, T, H)
            k_bf = k_bf.reshape(B, T, D)
            v_bf = v_bf.reshape(B, T, D)
            if past is None:
                k_all, v_all, logf_all, q_off = k_bf, v_bf, logf_new, 0
            else:
                past_k, past_v, past_logf = past
                q_off = past_k.shape[1]
                k_all = jnp.concatenate([past_k.reshape(B, q_off, D).astype(BF16), k_bf], axis=1)
                v_all = jnp.concatenate([past_v.reshape(B, q_off, D).astype(BF16), v_bf], axis=1)
                logf_all = jnp.concatenate([past_logf.astype(F32), logf_new], axis=1)
            Lk = k_all.shape[1]
            pad = (-Lk) % LANES
            if pad:
                k_all = jnp.pad(k_all, ((0, 0), (0, pad), (0, 0)))
                v_all = jnp.pad(v_all, ((0, 0), (0, pad), (0, 0)))
                logf_all = jnp.pad(logf_all, ((0, 0), (0, pad), (0, 0)))
            rows = jnp.transpose(logf_all, (0, 2, 1)).reshape(B * H, Lk + pad)
            dcum = _cumsum_last(rows).reshape(B, H, Lk + pad)
            dk = dcum[:, :, None, :]
            dq = dcum[:, :, q_off:q_off + T, None]
            k_aug = None
            if past is None and T % FOX_FAST_TILE == 0:
                kn = jnp.sqrt(_head_sq_sums(k_new, D)).reshape(B, T, H)
                kmax = jnp.max(kn, axis=1)[:, :, None]
                k_aug = _augment_keys(k_bf, dcum, B, T, H)
    return (xf.reshape(B, T, D), jnp.stack(states), k_new.reshape(B, T, H, HEAD),
            v_new.reshape(B, T, H, HEAD), logf_new)


def kernel(x_prompt, x_sample, state_hgrn, cache_k, cache_v, cache_logf, p_prompt, p_sample,
           g_norm_a, w_in_a, lb_logits, g_out_a, w_out_a, g_kv, w_kv, b_f,
           g_norm_b, w_in_b, w_out_b, w_ple_in, g_ple, w_ple_gate, g_final):
    D = x_prompt.shape[-1]
    H = D // HEAD
    n_a = w_in_a.shape[0]
    lbs = jnp.cumsum(jax.nn.softmax(lb_logits.astype(F32), axis=0), axis=0)
    lbs = lbs - lbs[:1]
    w_f = jnp.pad(w_kv[:, 2 * D:], ((0, 0), (0, LANES - H)))
    W = dict(
        g_norm_a=g_norm_a, w_in_a=w_in_a.astype(BF16), lbs=lbs, g_out_a=g_out_a,
        w_out_a=w_out_a.astype(BF16), g_kv=g_kv, w_k=w_kv[:, :D].astype(BF16),
        w_v=w_kv[:, D:2 * D].astype(BF16), w_f=w_f.astype(BF16),
        b_f=jnp.pad(b_f, (0, LANES - H)), g_norm_b=g_norm_b, w_in_b=w_in_b.astype(BF16),
        w_out_b=w_out_b.astype(BF16), w_ple_in=w_ple_in.astype(BF16), g_ple=g_ple,
        w_ple_gate=w_ple_gate.astype(BF16), g_final=g_final)
    s0_prompt = jnp.zeros((n_a, x_prompt.shape[0], H, HEAD, HEAD), F32)
    y_p, st_p, k_p, v_p, f_p = _trunk(x_prompt, p_prompt, s0_prompt, None, W)
    y_s, st_s, k_s, v_s, f_s = _trunk(x_sample, p_sample, state_hgrn,
                                      (cache_k, cache_v, cache_logf), W)
    return (y_p, y_s, st_p, st_s, k_p, v_p, f_p, k_s, v_s, f_s)
```

```python
import functools

import jax
import jax.numpy as jnp
import numpy as np
from jax import lax
from jax.experimental import pallas as pl
from jax.experimental.pallas import tpu as pltpu

F32 = jnp.float32
BF16 = jnp.bfloat16

EPS = 1e-6
HEAD = 128
LANES = 128
SUBLANES = 8
CHUNK = 64
N_LEVELS = 6
HGRN_HEADS = 16
MASK_VALUE = -1e30
VMEM_LIMIT = 56 * 1024 * 1024

_NT = (((1,), (1,)), ((), ()))


def _params(sem, vmem=VMEM_LIMIT):
    return pltpu.CompilerParams(dimension_semantics=sem, vmem_limit_bytes=vmem)


def _sigmoid(x):
    return 1.0 / (1.0 + jnp.exp(-x))


def _pick(n, prefs):
    for p in prefs:
        if n % p == 0:
            return p
    return n


def _nmm_kernel(*refs, log_sigmoid, with_bf16, row_chunk):
    if log_sigmoid:
        x_ref, g_ref, w_ref, b_ref = refs[:4]
        rest = refs[4:]
    else:
        x_ref, g_ref, w_ref = refs[:3]
        b_ref = None
        rest = refs[3:]
    if with_bf16:
        o_ref, ob_ref, xn_ref = rest
    else:
        o_ref, xn_ref = rest
        ob_ref = None
    tm = x_ref.shape[0]

    @pl.when(pl.program_id(1) == 0)
    def _():
        for r in range(0, tm, row_chunk):
            x = x_ref[r:r + row_chunk, :]
            ms = jnp.mean(x * x, axis=-1, keepdims=True)
            xn_ref[r:r + row_chunk, :] = (x * lax.rsqrt(ms + EPS) * g_ref[...]).astype(BF16)

    z = jnp.dot(xn_ref[...], w_ref[...], preferred_element_type=F32)
    if log_sigmoid:
        z = z + b_ref[...]
        z = jnp.minimum(z, 0.0) - jnp.log(1.0 + jnp.exp(-jnp.abs(z)))
    o_ref[...] = z
    if with_bf16:
        ob_ref[...] = z.astype(BF16)


def _norm_matmul(x, g, w, bias=None, with_bf16=False):
    T, D = x.shape
    N = w.shape[1]
    tm = _pick(T, (1024, 512, 256, 128, 64))
    tn = _pick(N, (1024, 512, 256, 128))
    row_chunk = min(tm, 256)
    in_specs = [
        pl.BlockSpec((tm, D), lambda i, j: (i, 0)),
        pl.BlockSpec((1, D), lambda i, j: (0, 0)),
        pl.BlockSpec((D, tn), lambda i, j: (0, j)),
    ]
    args = [x, g.reshape(1, D).astype(F32), w]
    if bias is not None:
        in_specs.append(pl.BlockSpec((1, tn), lambda i, j: (0, j)))
        args.append(bias.reshape(1, N).astype(F32))
    out_shape = [jax.ShapeDtypeStruct((T, N), F32)]
    out_specs = [pl.BlockSpec((tm, tn), lambda i, j: (i, j))]
    if with_bf16:
        out_shape.append(jax.ShapeDtypeStruct((T, N), BF16))
        out_specs.append(pl.BlockSpec((tm, tn), lambda i, j: (i, j)))
    out = pl.pallas_call(
        functools.partial(_nmm_kernel, log_sigmoid=bias is not None, with_bf16=with_bf16,
                          row_chunk=row_chunk),
        grid=(T // tm, N // tn),
        in_specs=in_specs,
        out_specs=out_specs,
        out_shape=out_shape,
        scratch_shapes=[pltpu.VMEM((tm, D), BF16)],
        compiler_params=_params(("parallel", "arbitrary")),
        name="norm_matmul",
    )(*args)
    return out if with_bf16 else out[0]


def _hgrn_constants():
    C = CHUNK
    t = np.arange(C)[:, None]
    u = np.arange(C)[None, :]
    blocks = [(u <= t), (u > t)]
    masks = []
    for lvl in range(N_LEVELS):
        h = C >> (lvl + 1)
        mid = (t // (2 * h)) * (2 * h) + h
        second = t >= mid
        blocks.append(np.where(second, (u >= mid) & (u <= t), (u > t) & (u < mid)))
        mid_s = (u // (2 * h)) * (2 * h) + h
        masks.append((t // (2 * h) == u // (2 * h)) & (t >= mid) & (u < mid_s))
    masks.append(t == u)
    sm = np.concatenate(blocks, axis=0).astype(np.float32)
    mk = np.stack(masks, axis=0).astype(np.float32)
    return jnp.asarray(sm, BF16), jnp.asarray(mk, F32)


def _hgrn_kernel(zq_ref, zf_ref, zi_ref, zg_ref, lb_ref, go_ref, s0_ref, sm_ref, mk_ref,
                 o_ref, sout_ref, st_ref, *, n_chunks, n_heads):
    C = CHUNK
    tb = pl.program_id(2)

    @pl.when(tb == 0)
    def _():
        for hh in range(n_heads):
            st_ref[hh] = s0_ref[hh].T

    def chunk(c, carry):
        r0 = pl.multiple_of(c * C, C)
        heads = range(n_heads)
        cols = [slice(hh * HEAD, (hh + 1) * HEAD) for hh in heads]
        q, k, v, d2 = [], [], [], []
        for hh in heads:
            lb = lb_ref[:, cols[hh]]
            zq = zq_ref[pl.ds(r0, C), cols[hh]]
            zf = zf_ref[pl.ds(r0, C), cols[hh]]
            q.append(zq * _sigmoid(zq))
            f = lb + (1.0 - lb) * _sigmoid(zf)
            g = jnp.log(f)
            k.append(1.0 - f)
            v.append(zi_ref[pl.ds(r0, C), cols[hh]])
            g_hi = g.astype(BF16)
            g_lo = (g - g_hi.astype(F32)).astype(BF16)
            d2.append(jnp.dot(sm_ref[...], jnp.concatenate([g_hi, g_lo], axis=1),
                              preferred_element_type=F32))
        e = [jnp.exp(d[:, :HEAD] + d[:, HEAD:]) for d in d2]
        a = []
        for hh in heads:
            parts = [lax.dot_general(q[hh].astype(BF16), k[hh].astype(BF16), _NT,
                                     preferred_element_type=F32)]
            for lvl in range(N_LEVELS):
                el = e[hh][(2 + lvl) * C:(3 + lvl) * C]
                parts.append(lax.dot_general((q[hh] * el).astype(BF16), (k[hh] * el).astype(BF16),
                                             _NT, preferred_element_type=F32))
            acc = mk_ref[N_LEVELS] * parts[0]
            for lvl in range(N_LEVELS):
                acc = acc + mk_ref[lvl] * parts[1 + lvl]
            a.append(acc)
        o = []
        for hh in heads:
            st = st_ref[hh]
            qg = (q[hh] * e[hh][0:C]).astype(BF16)
            kg = (k[hh] * e[hh][C:2 * C]).astype(BF16)
            vb = v[hh].astype(BF16)
            oh = jnp.dot(a[hh].astype(BF16), vb, preferred_element_type=F32)
            o.append(oh + lax.dot_general(qg, st.astype(BF16), _NT, preferred_element_type=F32))
            st_ref[hh] = st * e[hh][C - 1:C, :] + jnp.dot(v[hh].T.astype(BF16), kg,
                                                          preferred_element_type=F32)
        for hh in heads:
            zg = zg_ref[pl.ds(r0, C), cols[hh]]
            ms = jnp.mean(o[hh] * o[hh], axis=-1, keepdims=True)
            og = o[hh] * lax.rsqrt(ms + EPS) * go_ref[:, cols[hh]] * (zg * _sigmoid(zg))
            o_ref[pl.ds(r0, C), cols[hh]] = og.astype(o_ref.dtype)
        return carry

    lax.fori_loop(0, n_chunks, chunk, 0)

    @pl.when(tb == pl.num_programs(2) - 1)
    def _():
        for hh in range(n_heads):
            sout_ref[hh] = st_ref[hh].T


def _hgrn_mix(z, lb, g_out, s0, B, T):
    D = z.shape[1] // 4
    H = D // HEAD
    hb = _pick(H, (HGRN_HEADS,))
    tb = _pick(T, (256, 128, 64))
    nt = T // tb
    nh = H // hb
    sm, mk = _hgrn_constants()

    def zspec(part):
        return pl.BlockSpec((tb, hb * HEAD), lambda b, h, t: (b * nt + t, part * nh + h))

    head_vec = pl.BlockSpec((1, hb * HEAD), lambda b, h, t: (0, h))
    state = pl.BlockSpec((None, hb, HEAD, HEAD), lambda b, h, t: (b, h, 0, 0))
    og, s_fin = pl.pallas_call(
        functools.partial(_hgrn_kernel, n_chunks=tb // CHUNK, n_heads=hb),
        grid=(B, nh, nt),
        in_specs=[zspec(0), zspec(1), zspec(2), zspec(3), head_vec, head_vec, state,
                  pl.BlockSpec(sm.shape, lambda b, h, t: (0, 0)),
                  pl.BlockSpec(mk.shape, lambda b, h, t: (0, 0, 0))],
        out_specs=[pl.BlockSpec((tb, hb * HEAD), lambda b, h, t: (b * nt + t, h)), state],
        out_shape=[jax.ShapeDtypeStruct((B * T, D), BF16),
                   jax.ShapeDtypeStruct((B, H, HEAD, HEAD), F32)],
        scratch_shapes=[pltpu.VMEM((hb, HEAD, HEAD), F32)],
        compiler_params=_params(("parallel", "parallel", "arbitrary")),
        name="hgrn_mix",
    )(z, z, z, z, lb.reshape(1, D), g_out.reshape(1, D).astype(F32), s0, sm, mk)
    return og, s_fin


def _out_ple_kernel(og_ref, x_ref, p_ref, wo_ref, gpg_ref, wpg_ref, wpin_ref, gfin_ref, o_ref,
                    *, final, col_chunk):
    D = x_ref.shape[1]
    y = x_ref[...] + jnp.dot(og_ref[...], wo_ref[...], preferred_element_type=F32)
    ms = jnp.mean(y * y, axis=-1, keepdims=True)
    yn = (y * lax.rsqrt(ms + EPS) * gpg_ref[...]).astype(BF16)
    pb = p_ref[...].astype(BF16)
    parts = []
    for c0 in range(0, D, col_chunk):
        gate = _sigmoid(jnp.dot(yn, wpg_ref[:, c0:c0 + col_chunk], preferred_element_type=F32))
        emb = jnp.dot(pb, wpin_ref[:, c0:c0 + col_chunk], preferred_element_type=F32)
        parts.append(y[:, c0:c0 + col_chunk] + gate * emb)
    if final:
        ms2 = sum(jnp.sum(t * t, axis=-1, keepdims=True) for t in parts) * (1.0 / D)
        inv = lax.rsqrt(ms2 + EPS)
        for n, c0 in enumerate(range(0, D, col_chunk)):
            o_ref[:, c0:c0 + col_chunk] = parts[n] * inv * gfin_ref[:, c0:c0 + col_chunk]
    else:
        for n, c0 in enumerate(range(0, D, col_chunk)):
            o_ref[:, c0:c0 + col_chunk] = parts[n]


def _out_ple(og, x, p, w_out, g_pg, w_pg, w_pin, g_final, final):
    T, D = x.shape
    P = p.shape[1]
    tm = _pick(T, (256, 128, 64))
    col_chunk = min(D, 512)
    rows = lambda i: (i, 0)
    fixed = lambda i: (0, 0)
    once = pl.Buffered(1)
    return pl.pallas_call(
        functools.partial(_out_ple_kernel, final=final, col_chunk=col_chunk),
        grid=(T // tm,),
        in_specs=[pl.BlockSpec((tm, D), rows), pl.BlockSpec((tm, D), rows),
                  pl.BlockSpec((tm, P), rows),
                  pl.BlockSpec((D, D), fixed, pipeline_mode=once),
                  pl.BlockSpec((1, D), fixed),
                  pl.BlockSpec((D, D), fixed, pipeline_mode=once),
                  pl.BlockSpec((P, D), fixed, pipeline_mode=once),
                  pl.BlockSpec((1, D), fixed)],
        out_specs=pl.BlockSpec((tm, D), rows),
        out_shape=jax.ShapeDtypeStruct((T, D), F32),
        compiler_params=_params(("parallel",)),
        name="out_ple",
    )(og, x, p, w_out, g_pg.reshape(1, D).astype(F32), w_pg, w_pin,
      g_final.reshape(1, D).astype(F32))


def _cumsum_kernel(x_ref, u_ref, o_ref, carry_ref, *, n_sub):
    @pl.when(pl.program_id(0) == 0)
    def _():
        carry_ref[...] = jnp.zeros_like(carry_ref)

    u = u_ref[...]
    c = carry_ref[:, 0:1]
    for s in range(n_sub):
        x = x_ref[:, s * LANES:(s + 1) * LANES]
        hi = x.astype(BF16)
        r1 = x - hi.astype(F32)
        mid = r1.astype(BF16)
        lo = (r1 - mid.astype(F32)).astype(BF16)
        cs = (jnp.dot(hi, u, preferred_element_type=F32)
              + jnp.dot(mid, u, preferred_element_type=F32)
              + jnp.dot(lo, u, preferred_element_type=F32)) + c
        o_ref[:, s * LANES:(s + 1) * LANES] = cs
        c = cs[:, LANES - 1:LANES]
    carry_ref[...] = jnp.broadcast_to(c, carry_ref.shape)


def _cumsum_last(x):
    R, L = x.shape
    cb = _pick(L, (2048, 1024, 512, 256, 128))
    u = jnp.asarray(np.triu(np.ones((LANES, LANES), np.float32)), BF16)
    return pl.pallas_call(
        functools.partial(_cumsum_kernel, n_sub=cb // LANES),
        grid=(L // cb,),
        in_specs=[pl.BlockSpec((R, cb), lambda j: (0, j)),
                  pl.BlockSpec((LANES, LANES), lambda j: (0, 0))],
        out_specs=pl.BlockSpec((R, cb), lambda j: (0, j)),
        out_shape=jax.ShapeDtypeStruct((R, L), F32),
        scratch_shapes=[pltpu.VMEM((R, LANES), F32)],
        compiler_params=_params(("arbitrary",)),
        name="cumsum",
    )(x, u)


def _fox_kernel(q_ref, gate_ref, k_ref, v_ref, dq_ref, dk_ref, o_ref, *, tq, tk, q_off, scale):
    i = pl.program_id(2)
    q = q_ref[...].astype(BF16)
    dq = dq_ref[...]
    first_pos = q_off + i * tq
    n_full = (first_pos + 1) // tk
    n_all = (first_pos + tq - 1) // tk + 1

    def step(j, carry, masked):
        m, l, acc = carry
        start = pl.multiple_of(j * tk, tk)
        k = k_ref[pl.ds(start, tk), :]
        v = v_ref[pl.ds(start, tk), :]
        s = lax.dot_general(q, k, _NT, preferred_element_type=F32) * scale
        s = s + dq - dk_ref[:, pl.ds(start, tk)]
        if masked:
            kpos = start + lax.broadcasted_iota(jnp.int32, (tq, tk), 1)
            qpos = first_pos + lax.broadcasted_iota(jnp.int32, (tq, tk), 0)
            s = jnp.where(kpos <= qpos, s, MASK_VALUE)
        m_new = jnp.maximum(m, jnp.max(s, axis=-1, keepdims=True))
        alpha = jnp.exp(m - m_new)
        p = jnp.exp(s - m_new)
        l = alpha * l + jnp.sum(p, axis=-1, keepdims=True)
        acc = alpha * acc + jnp.dot(p.astype(BF16), v, preferred_element_type=F32)
        return m_new, l, acc

    init = (jnp.full((tq, 1), MASK_VALUE, F32), jnp.zeros((tq, 1), F32),
            jnp.zeros((tq, HEAD), F32))
    carry = lax.fori_loop(0, n_full, functools.partial(step, masked=False), init)
    _, l, acc = lax.fori_loop(n_full, n_all, functools.partial(step, masked=True), carry)
    gate = gate_ref[...]
    o = acc * (1.0 / l)
    o_ref[...] = (o * (gate * _sigmoid(gate))).astype(o_ref.dtype)


def _fox_mix(z, k_all, v_all, dq, dk, B, T, q_off):
    D = z.shape[1] // 2
    H = D // HEAD
    Lk = k_all.shape[1]
    tq = _pick(T, (512, 256, 128, 64))
    tk = Lk if Lk <= 2048 else _pick(Lk, (512, 256, 128))
    nq = T // tq
    kv = pl.BlockSpec((None, Lk, HEAD), lambda b, h, i: (b, 0, h))
    return pl.pallas_call(
        functools.partial(_fox_kernel, tq=tq, tk=tk, q_off=q_off, scale=HEAD ** -0.5),
        grid=(B, H, nq),
        in_specs=[pl.BlockSpec((tq, HEAD), lambda b, h, i: (b * nq + i, h)),
                  pl.BlockSpec((tq, HEAD), lambda b, h, i: (b * nq + i, H + h)),
                  kv, kv,
                  pl.BlockSpec((None, None, tq, 1), lambda b, h, i: (b, h, i, 0)),
                  pl.BlockSpec((None, None, 1, Lk), lambda b, h, i: (b, h, 0, 0))],
        out_specs=pl.BlockSpec((tq, HEAD), lambda b, h, i: (b * nq + i, h)),
        out_shape=jax.ShapeDtypeStruct((B * T, D), BF16),
        compiler_params=_params(("parallel", "parallel", "arbitrary")),
        name="fox_mix",
    )(z, z, k_all, v_all, dq, dk)


FOX_BOUND_MARGIN = 1.02
FOX_BOUND_LIMIT = 32.0
FOX_FAST_TILE = 1024
AUG = 6


def _top16(x):
    bits = lax.bitcast_convert_type(x, jnp.uint32) & jnp.uint32(0xFFFF0000)
    return lax.bitcast_convert_type(bits, F32)


def _split3(x):
    hi = _top16(x)
    r1 = x - hi
    mid = _top16(r1)
    lo = r1 - mid
    return hi.astype(BF16), mid.astype(BF16), lo.astype(BF16)


def _sq_sums_kernel(x_ref, e_ref, o_ref):
    x = x_ref[...]
    o_ref[...] = jnp.dot((x * x).astype(BF16), e_ref[...], preferred_element_type=F32)


def _head_sq_sums(x, D):
    T = x.shape[0]
    H = D // HEAD
    tm = _pick(T, (512, 256, 128, 64))
    e = np.zeros((D, LANES), np.float32)
    e[np.arange(D), np.arange(D) // HEAD] = 1.0
    return pl.pallas_call(
        _sq_sums_kernel,
        grid=(T // tm,),
        in_specs=[pl.BlockSpec((tm, D), lambda i: (i, 0)),
                  pl.BlockSpec((D, LANES), lambda i: (0, 0))],
        out_specs=pl.BlockSpec((tm, LANES), lambda i: (i, 0)),
        out_shape=jax.ShapeDtypeStruct((T, LANES), F32),
        compiler_params=_params(("parallel",)),
        name="head_sq_sums",
    )(x, jnp.asarray(e, BF16))[:, :H]


def _fox_fast_kernel(q_ref, c_ref, gate_ref, ka_ref, vt_ref, o_ref, qa_ref, *, tile, scale):
    i = pl.program_id(2)
    qa_ref[:, :HEAD] = (q_ref[...] * scale).astype(BF16)
    lane = lax.broadcasted_iota(jnp.int32, (tile, HEAD), 1)
    hi, mid, lo = _split3(c_ref[...])
    aug = jnp.where(lane == 0, hi.astype(F32), jnp.where(lane == 1, mid.astype(F32),
          jnp.where(lane == 2, lo.astype(F32), jnp.where(lane < AUG, 1.0, 0.0))))
    qa_ref[:, HEAD:] = aug.astype(BF16)

    def scores(j, masked):
        start = pl.multiple_of(j * tile, tile)
        s = lax.dot_general(ka_ref[pl.ds(start, tile), :], qa_ref[...], _NT,
                            preferred_element_type=F32)
        if masked:
            kpos = lax.broadcasted_iota(jnp.int32, (tile, tile), 0)
            qpos = lax.broadcasted_iota(jnp.int32, (tile, tile), 1)
            s = jnp.where(kpos <= qpos, s, MASK_VALUE)
        return s

    def consume(s, j, lp, acc):
        start = pl.multiple_of(j * tile, tile)
        p = jnp.exp(s)
        for g in range(tile // SUBLANES):
            lp = lp + p[g * SUBLANES:(g + 1) * SUBLANES, :]
        acc = acc + jnp.dot(vt_ref[:, pl.ds(start, tile)], p.astype(BF16),
                            preferred_element_type=F32)
        return lp, acc

    def pair(j0, carry, masked_b):
        lp, acc = carry
        sa = scores(j0, False)
        sb = scores(j0 + 1, masked_b)
        lp, acc = consume(sa, j0, lp, acc)
        return consume(sb, j0 + 1, lp, acc)

    init = (jnp.zeros((SUBLANES, tile), F32), jnp.zeros((HEAD, tile), F32))
    carry = lax.fori_loop(0, i // 2, lambda t, c: pair(2 * t, c, False), init)
    lp, acc = lax.cond(i % 2 == 1, lambda c: pair(i - 1, c, True),
                       lambda c: consume(scores(i, True), i, *c), carry)
    gate = gate_ref[...]
    o = (acc * (1.0 / jnp.sum(lp, axis=0, keepdims=True))).T
    o_ref[...] = (o * (gate * _sigmoid(gate))).astype(o_ref.dtype)


def _fox_mix_fast(z, c, k_aug, v_t, B, T):
    D = z.shape[1] // 2
    H = D // HEAD
    tile = FOX_FAST_TILE
    nq = T // tile
    return pl.pallas_call(
        functools.partial(_fox_fast_kernel, tile=tile, scale=HEAD ** -0.5),
        grid=(B, H, nq),
        in_specs=[pl.BlockSpec((tile, HEAD), lambda b, h, i: (b * nq + i, h)),
                  pl.BlockSpec((None, None, tile, 1), lambda b, h, i: (b, h, i, 0)),
                  pl.BlockSpec((tile, HEAD), lambda b, h, i: (b * nq + i, H + h)),
                  pl.BlockSpec((None, T, 2 * HEAD), lambda b, h, i: (b, 0, h)),
                  pl.BlockSpec((None, None, HEAD, T), lambda b, h, i: (b, h, 0, 0))],
        out_specs=pl.BlockSpec((tile, HEAD), lambda b, h, i: (b * nq + i, h)),
        out_shape=jax.ShapeDtypeStruct((B * T, D), BF16),
        scratch_shapes=[pltpu.VMEM((tile, 2 * HEAD), BF16)],
        compiler_params=_params(("parallel", "parallel", "arbitrary")),
        name="fox_mix_fast",
    )(z, c, z, k_aug, v_t)


def _augment_keys(k_bf, dcum, B, Lk, H):
    hi, mid, lo = _split3(-jnp.transpose(dcum, (0, 2, 1)))
    one = jnp.ones_like(hi)
    aug = jnp.stack([one, one, one, hi, mid, lo], axis=-1)
    aug = jnp.pad(aug, ((0, 0), (0, 0), (0, 0), (0, HEAD - AUG)))
    return jnp.concatenate([k_bf.reshape(B, Lk, H, HEAD), aug], axis=-1).reshape(B, Lk, 2 * H * HEAD)


def _trunk(x, p, s0, past, W):
    B, T, D = x.shape
    H = D // HEAD
    depth = p.shape[0]
    n_a = W["w_in_a"].shape[0]
    xf = x.reshape(B * T, D)
    pf = p.reshape(depth, B * T, p.shape[-1])
    states = []
    for layer in range(depth):
        final = layer == depth - 1
        if layer < n_a:
            z = _norm_matmul(xf, W["g_norm_a"][layer], W["w_in_a"][layer])
            og, s_fin = _hgrn_mix(z, W["lbs"][layer], W["g_out_a"][layer], s0[layer], B, T)
            states.append(s_fin)
            w_out = W["w_out_a"][layer]
        else:
            j = layer - n_a
            z = _norm_matmul(xf, W["g_norm_b"][j], W["w_in_b"][j])
            if k_aug is None:
                og = _fox_mix(z, k_all, v_all, dq, dk, B, T, q_off)
            else:
                qn = jnp.sqrt(_head_sq_sums(z, D)).reshape(B, T, H)
                bound = (FOX_BOUND_MARGIN * HEAD ** -0.5) * jnp.transpose(qn, (0, 2, 1)) * kmax
                c = dq - bound[..., None]
                og = lax.cond(jnp.max(bound) <= FOX_BOUND_LIMIT,
                              lambda: _fox_mix_fast(z, c, k_aug, v_t, B, T),
                              lambda: _fox_mix(z, k_all, v_all, dq, dk, B, T, q_off))
            w_out = W["w_out_b"][j]
        xf = _out_ple(og, xf, pf[layer], w_out, W["g_ple"][layer], W["w_ple_gate"][layer],
                      W["w_ple_in"][layer], W["g_final"], final)
        if layer == n_a - 1:
            k_new, k_bf = _norm_matmul(xf, W["g_kv"], W["w_k"], with_bf16=True)
            v_new, v_bf = _norm_matmul(xf, W["g_kv"], W["w_v"], with_bf16=True)
            logf_new = _norm_matmul(xf, W["g_kv"], W["w_f"], bias=W["b_f"])[:, :H]
            logf_new = logf_new.reshape(B, T, H)
            k_bf = k_bf.reshape(B, T, D)
            v_bf = v_bf.reshape(B, T, D)
            if past is None:
                k_all, v_all, logf_all, q_off = k_bf, v_bf, logf_new, 0
            else:
                past_k, past_v, past_logf = past
                q_off = past_k.shape[1]
                k_all = jnp.concatenate([past_k.reshape(B, q_off, D).astype(BF16), k_bf], axis=1)
                v_all = jnp.concatenate([past_v.reshape(B, q_off, D).astype(BF16), v_bf], axis=1)
                logf_all = jnp.concatenate([past_logf.astype(F32), logf_new], axis=1)
            Lk = k_all.shape[1]
            pad = (-Lk) % LANES
            if pad:
                k_all = jnp.pad(k_all, ((0, 0), (0, pad), (0, 0)))
                v_all = jnp.pad(v_all, ((0, 0), (0, pad), (0, 0)))
                logf_all = jnp.pad(logf_all, ((0, 0), (0, pad), (0, 0)))
            rows = jnp.transpose(logf_all, (0, 2, 1)).reshape(B * H, Lk + pad)
            dcum = _cumsum_last(rows).reshape(B, H, Lk + pad)
            dk = dcum[:, :, None, :]
            dq = dcum[:, :, q_off:q_off + T, None]
            k_aug = None
            if past is None and T % FOX_FAST_TILE == 0:
                kn = jnp.sqrt(_head_sq_sums(k_new, D)).reshape(B, T, H)
                kmax = jnp.max(kn, axis=1)[:, :, None]
                k_aug = _augment_keys(k_bf, dcum, B, T, H)
                v_t = jnp.transpose(v_bf.reshape(B, T, H, HEAD), (0, 2, 3, 1))
    return (xf.reshape(B, T, D), jnp.stack(states), k_new.reshape(B, T, H, HEAD),
            v_new.reshape(B, T, H, HEAD), logf_new)


def kernel(x_prompt, x_sample, state_hgrn, cache_k, cache_v, cache_logf, p_prompt, p_sample,
           g_norm_a, w_in_a, lb_logits, g_out_a, w_out_a, g_kv, w_kv, b_f,
           g_norm_b, w_in_b, w_out_b, w_ple_in, g_ple, w_ple_gate, g_final):
    D = x_prompt.shape[-1]
    H = D // HEAD
    n_a = w_in_a.shape[0]
    lbs = jnp.cumsum(jax.nn.softmax(lb_logits.astype(F32), axis=0), axis=0)
    lbs = lbs - lbs[:1]
    w_f = jnp.pad(w_kv[:, 2 * D:], ((0, 0), (0, LANES - H)))
    W = dict(
        g_norm_a=g_norm_a, w_in_a=w_in_a.astype(BF16), lbs=lbs, g_out_a=g_out_a,
        w_out_a=w_out_a.astype(BF16), g_kv=g_kv, w_k=w_kv[:, :D].astype(BF16),
        w_v=w_kv[:, D:2 * D].astype(BF16), w_f=w_f.astype(BF16),
        b_f=jnp.pad(b_f, (0, LANES - H)), g_norm_b=g_norm_b, w_in_b=w_in_b.astype(BF16),
        w_out_b=w_out_b.astype(BF16), w_ple_in=w_ple_in.astype(BF16), g_ple=g_ple,
        w_ple_gate=w_ple_gate.astype(BF16), g_final=g_final)
    s0_prompt = jnp.zeros((n_a, x_prompt.shape[0], H, HEAD, HEAD), F32)
    y_p, st_p, k_p, v_p, f_p = _trunk(x_prompt, p_prompt, s0_prompt, None, W)
    y_s, st_s, k_s, v_s, f_s = _trunk(x_sample, p_sample, state_hgrn,
                                      (cache_k, cache_v, cache_logf), W)
    return (y_p, y_s, st_p, st_s, k_p, v_p, f_p, k_s, v_s, f_s)
```

```python
import functools

import jax
import jax.numpy as jnp
import numpy as np
from jax import lax
from jax.experimental import pallas as pl
from jax.experimental.pallas import tpu as pltpu

F32 = jnp.float32
BF16 = jnp.bfloat16

EPS = 1e-6
HEAD = 128
LANES = 128
SUBLANES = 8
CHUNK = 64
N_LEVELS = 6
HGRN_HEADS = 16
MASK_VALUE = -1e30
VMEM_LIMIT = 56 * 1024 * 1024

_NT = (((1,), (1,)), ((), ()))


def _params(sem, vmem=VMEM_LIMIT):
    return pltpu.CompilerParams(dimension_semantics=sem, vmem_limit_bytes=vmem)


def _sigmoid(x):
    return 1.0 / (1.0 + jnp.exp(-x))


def _pick(n, prefs):
    for p in prefs:
        if n % p == 0:
            return p
    return n


def _nmm_kernel(*refs, log_sigmoid, with_bf16, row_chunk):
    if log_sigmoid:
        x_ref, g_ref, w_ref, b_ref = refs[:4]
        rest = refs[4:]
    else:
        x_ref, g_ref, w_ref = refs[:3]
        b_ref = None
        rest = refs[3:]
    if with_bf16:
        o_ref, ob_ref, xn_ref = rest
    else:
        o_ref, xn_ref = rest
        ob_ref = None
    tm = x_ref.shape[0]

    @pl.when(pl.program_id(1) == 0)
    def _():
        for r in range(0, tm, row_chunk):
            x = x_ref[r:r + row_chunk, :]
            ms = jnp.mean(x * x, axis=-1, keepdims=True)
            xn_ref[r:r + row_chunk, :] = (x * lax.rsqrt(ms + EPS) * g_ref[...]).astype(BF16)

    z = jnp.dot(xn_ref[...], w_ref[...], preferred_element_type=F32)
    if log_sigmoid:
        z = z + b_ref[...]
        z = jnp.minimum(z, 0.0) - jnp.log(1.0 + jnp.exp(-jnp.abs(z)))
    o_ref[...] = z
    if with_bf16:
        ob_ref[...] = z.astype(BF16)


def _norm_matmul(x, g, w, bias=None, with_bf16=False):
    T, D = x.shape
    N = w.shape[1]
    tm = _pick(T, (1024, 512, 256, 128, 64))
    tn = _pick(N, (1024, 512, 256, 128))
    row_chunk = min(tm, 256)
    in_specs = [
        pl.BlockSpec((tm, D), lambda i, j: (i, 0)),
        pl.BlockSpec((1, D), lambda i, j: (0, 0)),
        pl.BlockSpec((D, tn), lambda i, j: (0, j)),
    ]
    args = [x, g.reshape(1, D).astype(F32), w]
    if bias is not None:
        in_specs.append(pl.BlockSpec((1, tn), lambda i, j: (0, j)))
        args.append(bias.reshape(1, N).astype(F32))
    out_shape = [jax.ShapeDtypeStruct((T, N), F32)]
    out_specs = [pl.BlockSpec((tm, tn), lambda i, j: (i, j))]
    if with_bf16:
        out_shape.append(jax.ShapeDtypeStruct((T, N), BF16))
        out_specs.append(pl.BlockSpec((tm, tn), lambda i, j: (i, j)))
    out = pl.pallas_call(
        functools.partial(_nmm_kernel, log_sigmoid=bias is not None, with_bf16=with_bf16,
                          row_chunk=row_chunk),
        grid=(T // tm, N // tn),
        in_specs=in_specs,
        out_specs=out_specs,
        out_shape=out_shape,
        scratch_shapes=[pltpu.VMEM((tm, D), BF16)],
        compiler_params=_params(("parallel", "arbitrary")),
        name="norm_matmul",
    )(*args)
    return out if with_bf16 else out[0]


def _hgrn_constants():
    C = CHUNK
    t = np.arange(C)[:, None]
    u = np.arange(C)[None, :]
    blocks = [(u <= t), (u > t)]
    masks = []
    for lvl in range(N_LEVELS):
        h = C >> (lvl + 1)
        mid = (t // (2 * h)) * (2 * h) + h
        second = t >= mid
        blocks.append(np.where(second, (u >= mid) & (u <= t), (u > t) & (u < mid)))
        mid_s = (u // (2 * h)) * (2 * h) + h
        masks.append((t // (2 * h) == u // (2 * h)) & (t >= mid) & (u < mid_s))
    masks.append(t == u)
    sm = np.concatenate(blocks, axis=0).astype(np.float32)
    mk = np.stack(masks, axis=0).astype(np.float32)
    return jnp.asarray(sm, BF16), jnp.asarray(mk, F32)


def _hgrn_kernel(zq_ref, zf_ref, zi_ref, zg_ref, lb_ref, go_ref, s0_ref, sm_ref, mk_ref,
                 o_ref, sout_ref, st_ref, *, n_chunks, n_heads):
    C = CHUNK
    tb = pl.program_id(2)

    @pl.when(tb == 0)
    def _():
        for hh in range(n_heads):
            st_ref[hh] = s0_ref[hh].T

    def chunk(c, carry):
        r0 = pl.multiple_of(c * C, C)
        heads = range(n_heads)
        cols = [slice(hh * HEAD, (hh + 1) * HEAD) for hh in heads]
        q, k, v, d2 = [], [], [], []
        for hh in heads:
            lb = lb_ref[:, cols[hh]]
            zq = zq_ref[pl.ds(r0, C), cols[hh]]
            zf = zf_ref[pl.ds(r0, C), cols[hh]]
            q.append(zq * _sigmoid(zq))
            f = lb + (1.0 - lb) * _sigmoid(zf)
            g = jnp.log(f)
            k.append(1.0 - f)
            v.append(zi_ref[pl.ds(r0, C), cols[hh]])
            g_hi = g.astype(BF16)
            g_lo = (g - g_hi.astype(F32)).astype(BF16)
            d2.append(jnp.dot(sm_ref[...], jnp.concatenate([g_hi, g_lo], axis=1),
                              preferred_element_type=F32))
        e = [jnp.exp(d[:, :HEAD] + d[:, HEAD:]) for d in d2]
        a = []
        for hh in heads:
            parts = [lax.dot_general(q[hh].astype(BF16), k[hh].astype(BF16), _NT,
                                     preferred_element_type=F32)]
            for lvl in range(N_LEVELS):
                el = e[hh][(2 + lvl) * C:(3 + lvl) * C]
                parts.append(lax.dot_general((q[hh] * el).astype(BF16), (k[hh] * el).astype(BF16),
                                             _NT, preferred_element_type=F32))
            acc = mk_ref[N_LEVELS] * parts[0]
            for lvl in range(N_LEVELS):
                acc = acc + mk_ref[lvl] * parts[1 + lvl]
            a.append(acc)
        o = []
        for hh in heads:
            st = st_ref[hh]
            qg = (q[hh] * e[hh][0:C]).astype(BF16)
            kg = (k[hh] * e[hh][C:2 * C]).astype(BF16)
            vb = v[hh].astype(BF16)
            oh = jnp.dot(a[hh].astype(BF16), vb, preferred_element_type=F32)
            o.append(oh + lax.dot_general(qg, st.astype(BF16), _NT, preferred_element_type=F32))
            st_ref[hh] = st * e[hh][C - 1:C, :] + jnp.dot(v[hh].T.astype(BF16), kg,
                                                          preferred_element_type=F32)
        for hh in heads:
            zg = zg_ref[pl.ds(r0, C), cols[hh]]
            ms = jnp.mean(o[hh] * o[hh], axis=-1, keepdims=True)
            og = o[hh] * lax.rsqrt(ms + EPS) * go_ref[:, cols[hh]] * (zg * _sigmoid(zg))
            o_ref[pl.ds(r0, C), cols[hh]] = og.astype(o_ref.dtype)
        return carry

    lax.fori_loop(0, n_chunks, chunk, 0)

    @pl.when(tb == pl.num_programs(2) - 1)
    def _():
        for hh in range(n_heads):
            sout_ref[hh] = st_ref[hh].T


def _hgrn_mix(z, lb, g_out, s0, B, T):
    D = z.shape[1] // 4
    H = D // HEAD
    hb = _pick(H, (HGRN_HEADS,))
    tb = _pick(T, (256, 128, 64))
    nt = T // tb
    nh = H // hb
    sm, mk = _hgrn_constants()

    def zspec(part):
        return pl.BlockSpec((tb, hb * HEAD), lambda b, h, t: (b * nt + t, part * nh + h))

    head_vec = pl.BlockSpec((1, hb * HEAD), lambda b, h, t: (0, h))
    state = pl.BlockSpec((None, hb, HEAD, HEAD), lambda b, h, t: (b, h, 0, 0))
    og, s_fin = pl.pallas_call(
        functools.partial(_hgrn_kernel, n_chunks=tb // CHUNK, n_heads=hb),
        grid=(B, nh, nt),
        in_specs=[zspec(0), zspec(1), zspec(2), zspec(3), head_vec, head_vec, state,
                  pl.BlockSpec(sm.shape, lambda b, h, t: (0, 0)),
                  pl.BlockSpec(mk.shape, lambda b, h, t: (0, 0, 0))],
        out_specs=[pl.BlockSpec((tb, hb * HEAD), lambda b, h, t: (b * nt + t, h)), state],
        out_shape=[jax.ShapeDtypeStruct((B * T, D), BF16),
                   jax.ShapeDtypeStruct((B, H, HEAD, HEAD), F32)],
        scratch_shapes=[pltpu.VMEM((hb, HEAD, HEAD), F32)],
        compiler_params=_params(("parallel", "parallel", "arbitrary")),
        name="hgrn_mix",
    )(z, z, z, z, lb.reshape(1, D), g_out.reshape(1, D).astype(F32), s0, sm, mk)
    return og, s_fin


def _out_ple_kernel(og_ref, x_ref, p_ref, wo_ref, gpg_ref, wpg_ref, wpin_ref, gfin_ref, o_ref,
                    *, final, col_chunk):
    D = x_ref.shape[1]
    y = x_ref[...] + jnp.dot(og_ref[...], wo_ref[...], preferred_element_type=F32)
    ms = jnp.mean(y * y, axis=-1, keepdims=True)
    yn = (y * lax.rsqrt(ms + EPS) * gpg_ref[...]).astype(BF16)
    pb = p_ref[...].astype(BF16)
    parts = []
    for c0 in range(0, D, col_chunk):
        gate = _sigmoid(jnp.dot(yn, wpg_ref[:, c0:c0 + col_chunk], preferred_element_type=F32))
        emb = jnp.dot(pb, wpin_ref[:, c0:c0 + col_chunk], preferred_element_type=F32)
        parts.append(y[:, c0:c0 + col_chunk] + gate * emb)
    if final:
        ms2 = sum(jnp.sum(t * t, axis=-1, keepdims=True) for t in parts) * (1.0 / D)
        inv = lax.rsqrt(ms2 + EPS)
        for n, c0 in enumerate(range(0, D, col_chunk)):
            o_ref[:, c0:c0 + col_chunk] = parts[n] * inv * gfin_ref[:, c0:c0 + col_chunk]
    else:
        for n, c0 in enumerate(range(0, D, col_chunk)):
            o_ref[:, c0:c0 + col_chunk] = parts[n]


def _out_ple(og, x, p, w_out, g_pg, w_pg, w_pin, g_final, final):
    T, D = x.shape
    P = p.shape[1]
    tm = _pick(T, (256, 128, 64))
    col_chunk = min(D, 512)
    rows = lambda i: (i, 0)
    fixed = lambda i: (0, 0)
    once = pl.Buffered(1)
    return pl.pallas_call(
        functools.partial(_out_ple_kernel, final=final, col_chunk=col_chunk),
        grid=(T // tm,),
        in_specs=[pl.BlockSpec((tm, D), rows), pl.BlockSpec((tm, D), rows),
                  pl.BlockSpec((tm, P), rows),
                  pl.BlockSpec((D, D), fixed, pipeline_mode=once),
                  pl.BlockSpec((1, D), fixed),
                  pl.BlockSpec((D, D), fixed, pipeline_mode=once),
                  pl.BlockSpec((P, D), fixed, pipeline_mode=once),
                  pl.BlockSpec((1, D), fixed)],
        out_specs=pl.BlockSpec((tm, D), rows),
        out_shape=jax.ShapeDtypeStruct((T, D), F32),
        compiler_params=_params(("parallel",)),
        name="out_ple",
    )(og, x, p, w_out, g_pg.reshape(1, D).astype(F32), w_pg, w_pin,
      g_final.reshape(1, D).astype(F32))


def _cumsum_kernel(x_ref, u_ref, o_ref, carry_ref, *, n_sub):
    @pl.when(pl.program_id(0) == 0)
    def _():
        carry_ref[...] = jnp.zeros_like(carry_ref)

    u = u_ref[...]
    c = carry_ref[:, 0:1]
    for s in range(n_sub):
        x = x_ref[:, s * LANES:(s + 1) * LANES]
        hi = x.astype(BF16)
        r1 = x - hi.astype(F32)
        mid = r1.astype(BF16)
        lo = (r1 - mid.astype(F32)).astype(BF16)
        cs = (jnp.dot(hi, u, preferred_element_type=F32)
              + jnp.dot(mid, u, preferred_element_type=F32)
              + jnp.dot(lo, u, preferred_element_type=F32)) + c
        o_ref[:, s * LANES:(s + 1) * LANES] = cs
        c = cs[:, LANES - 1:LANES]
    carry_ref[...] = jnp.broadcast_to(c, carry_ref.shape)


def _cumsum_last(x):
    R, L = x.shape
    cb = _pick(L, (2048, 1024, 512, 256, 128))
    u = jnp.asarray(np.triu(np.ones((LANES, LANES), np.float32)), BF16)
    return pl.pallas_call(
        functools.partial(_cumsum_kernel, n_sub=cb // LANES),
        grid=(L // cb,),
        in_specs=[pl.BlockSpec((R, cb), lambda j: (0, j)),
                  pl.BlockSpec((LANES, LANES), lambda j: (0, 0))],
        out_specs=pl.BlockSpec((R, cb), lambda j: (0, j)),
        out_shape=jax.ShapeDtypeStruct((R, L), F32),
        scratch_shapes=[pltpu.VMEM((R, LANES), F32)],
        compiler_params=_params(("arbitrary",)),
        name="cumsum",
    )(x, u)


def _fox_kernel(q_ref, gate_ref, k_ref, v_ref, dq_ref, dk_ref, o_ref, *, tq, tk, q_off, scale):
    i = pl.program_id(2)
    q = q_ref[...].astype(BF16)
    dq = dq_ref[...]
    first_pos = q_off + i * tq
    n_full = (first_pos + 1) // tk
    n_all = (first_pos + tq - 1) // tk + 1

    def step(j, carry, masked):
        m, l, acc = carry
        start = pl.multiple_of(j * tk, tk)
        k = k_ref[pl.ds(start, tk), :]
        v = v_ref[pl.ds(start, tk), :]
        s = lax.dot_general(q, k, _NT, preferred_element_type=F32) * scale
        s = s + dq - dk_ref[:, pl.ds(start, tk)]
        if masked:
            kpos = start + lax.broadcasted_iota(jnp.int32, (tq, tk), 1)
            qpos = first_pos + lax.broadcasted_iota(jnp.int32, (tq, tk), 0)
            s = jnp.where(kpos <= qpos, s, MASK_VALUE)
        m_new = jnp.maximum(m, jnp.max(s, axis=-1, keepdims=True))
        alpha = jnp.exp(m - m_new)
        p = jnp.exp(s - m_new)
        l = alpha * l + jnp.sum(p, axis=-1, keepdims=True)
        acc = alpha * acc + jnp.dot(p.astype(BF16), v, preferred_element_type=F32)
        return m_new, l, acc

    init = (jnp.full((tq, 1), MASK_VALUE, F32), jnp.zeros((tq, 1), F32),
            jnp.zeros((tq, HEAD), F32))
    carry = lax.fori_loop(0, n_full, functools.partial(step, masked=False), init)
    _, l, acc = lax.fori_loop(n_full, n_all, functools.partial(step, masked=True), carry)
    gate = gate_ref[...]
    o = acc * (1.0 / l)
    o_ref[...] = (o * (gate * _sigmoid(gate))).astype(o_ref.dtype)


def _fox_mix(z, k_all, v_all, dq, dk, B, T, q_off):
    D = z.shape[1] // 2
    H = D // HEAD
    Lk = k_all.shape[1]
    tq = _pick(T, (512, 256, 128, 64))
    tk = Lk if Lk <= 2048 else _pick(Lk, (512, 256, 128))
    nq = T // tq
    kv = pl.BlockSpec((None, Lk, HEAD), lambda b, h, i: (b, 0, h))
    return pl.pallas_call(
        functools.partial(_fox_kernel, tq=tq, tk=tk, q_off=q_off, scale=HEAD ** -0.5),
        grid=(B, H, nq),
        in_specs=[pl.BlockSpec((tq, HEAD), lambda b, h, i: (b * nq + i, h)),
                  pl.BlockSpec((tq, HEAD), lambda b, h, i: (b * nq + i, H + h)),
                  kv, kv,
                  pl.BlockSpec((None, None, tq, 1), lambda b, h, i: (b, h, i, 0)),
                  pl.BlockSpec((None, None, 1, Lk), lambda b, h, i: (b, h, 0, 0))],
        out_specs=pl.BlockSpec((tq, HEAD), lambda b, h, i: (b * nq + i, h)),
        out_shape=jax.ShapeDtypeStruct((B * T, D), BF16),
        compiler_params=_params(("parallel", "parallel", "arbitrary")),
        name="fox_mix",
    )(z, z, k_all, v_all, dq, dk)


FOX_BOUND_MARGIN = 1.02
FOX_BOUND_LIMIT = 32.0
FOX_FAST_TILE = 1024
AUG = 6


def _top16(x):
    bits = lax.bitcast_convert_type(x, jnp.uint32) & jnp.uint32(0xFFFF0000)
    return lax.bitcast_convert_type(bits, F32)


def _split3(x):
    hi = _top16(x)
    r1 = x - hi
    mid = _top16(r1)
    lo = r1 - mid
    return hi.astype(BF16), mid.astype(BF16), lo.astype(BF16)


def _bias_columns(x, value_first):
    lane = lax.broadcasted_iota(jnp.int32, x.shape, 1)
    v0, o0 = (0, AUG // 2) if value_first else (AUG // 2, 0)
    hi, mid, lo = _split3(x)
    ones = (lane >= o0) & (lane < o0 + AUG // 2)
    cols = jnp.where(lane == v0, hi.astype(F32), jnp.where(lane == v0 + 1, mid.astype(F32),
           jnp.where(lane == v0 + 2, lo.astype(F32), jnp.where(ones, 1.0, 0.0))))
    return cols.astype(BF16)


def _sq_sums_kernel(x_ref, e_ref, o_ref):
    x = x_ref[...]
    o_ref[...] = jnp.dot((x * x).astype(BF16), e_ref[...], preferred_element_type=F32)


def _head_sq_sums(x, D):
    T = x.shape[0]
    H = D // HEAD
    tm = _pick(T, (512, 256, 128, 64))
    e = np.zeros((D, LANES), np.float32)
    e[np.arange(D), np.arange(D) // HEAD] = 1.0
    return pl.pallas_call(
        _sq_sums_kernel,
        grid=(T // tm,),
        in_specs=[pl.BlockSpec((tm, D), lambda i: (i, 0)),
                  pl.BlockSpec((D, LANES), lambda i: (0, 0))],
        out_specs=pl.BlockSpec((tm, LANES), lambda i: (i, 0)),
        out_shape=jax.ShapeDtypeStruct((T, LANES), F32),
        compiler_params=_params(("parallel",)),
        name="head_sq_sums",
    )(x, jnp.asarray(e, BF16))[:, :H]


def _fox_fast_kernel(q_ref, c_ref, gate_ref, ka_ref, vt_ref, o_ref, qa_ref, *, tile, scale):
    i = pl.program_id(2)
    qa_ref[:, :HEAD] = (q_ref[...] * scale).astype(BF16)
    c_col = jnp.broadcast_to(c_ref[...], (HEAD, tile)).T
    qa_ref[:, HEAD:] = _bias_columns(c_col, value_first=True)

    def scores(j, masked):
        start = pl.multiple_of(j * tile, tile)
        s = lax.dot_general(ka_ref[pl.ds(start, tile), :], qa_ref[...], _NT,
                            preferred_element_type=F32)
        if masked:
            kpos = lax.broadcasted_iota(jnp.int32, (tile, tile), 0)
            qpos = lax.broadcasted_iota(jnp.int32, (tile, tile), 1)
            s = jnp.where(kpos <= qpos, s, MASK_VALUE)
        return s

    def consume(s, j, lp, acc):
        start = pl.multiple_of(j * tile, tile)
        p = jnp.exp(s)
        for g in range(tile // SUBLANES):
            lp = lp + p[g * SUBLANES:(g + 1) * SUBLANES, :]
        acc = acc + jnp.dot(vt_ref[:, pl.ds(start, tile)], p.astype(BF16),
                            preferred_element_type=F32)
        return lp, acc

    def pair(j0, carry, masked_b):
        lp, acc = carry
        sa = scores(j0, False)
        sb = scores(j0 + 1, masked_b)
        lp, acc = consume(sa, j0, lp, acc)
        return consume(sb, j0 + 1, lp, acc)

    init = (jnp.zeros((SUBLANES, tile), F32), jnp.zeros((HEAD, tile), F32))
    carry = lax.fori_loop(0, i // 2, lambda t, c: pair(2 * t, c, False), init)
    lp, acc = lax.cond(i % 2 == 1, lambda c: pair(i - 1, c, True),
                       lambda c: consume(scores(i, True), i, *c), carry)
    gate = gate_ref[...]
    o = (acc * (1.0 / jnp.sum(lp, axis=0, keepdims=True))).T
    o_ref[...] = (o * (gate * _sigmoid(gate))).astype(o_ref.dtype)


def _fox_mix_fast(z, c, k_aug, v_t, B, T):
    D = z.shape[1] // 2
    H = D // HEAD
    tile = FOX_FAST_TILE
    nq = T // tile
    return pl.pallas_call(
        functools.partial(_fox_fast_kernel, tile=tile, scale=HEAD ** -0.5),
        grid=(B, H, nq),
        in_specs=[pl.BlockSpec((tile, HEAD), lambda b, h, i: (b * nq + i, h)),
                  pl.BlockSpec((None, None, 1, tile), lambda b, h, i: (b, h, 0, i)),
                  pl.BlockSpec((tile, HEAD), lambda b, h, i: (b * nq + i, H + h)),
                  pl.BlockSpec((None, T, 2 * HEAD), lambda b, h, i: (b, 0, h)),
                  pl.BlockSpec((None, None, HEAD, T), lambda b, h, i: (b, h, 0, 0))],
        out_specs=pl.BlockSpec((tile, HEAD), lambda b, h, i: (b * nq + i, h)),
        out_shape=jax.ShapeDtypeStruct((B * T, D), BF16),
        scratch_shapes=[pltpu.VMEM((tile, 2 * HEAD), BF16)],
        compiler_params=_params(("parallel", "parallel", "arbitrary")),
        name="fox_mix_fast",
    )(z, c, z, k_aug, v_t)


def _augment_kernel(k_ref, dk_ref, o_ref, *, n_heads):
    tm = k_ref.shape[0]
    for h in range(n_heads):
        o_ref[:, 2 * h * HEAD:(2 * h + 1) * HEAD] = k_ref[:, h * HEAD:(h + 1) * HEAD]
        col = jnp.broadcast_to(-dk_ref[:, h:h + 1], (tm, HEAD))
        o_ref[:, (2 * h + 1) * HEAD:(2 * h + 2) * HEAD] = _bias_columns(col, value_first=False)


def _augment_keys(k_bf, dcum, B, Lk, H):
    D = H * HEAD
    T = B * Lk
    tm = _pick(T, (512, 256, 128, 64))
    dk_rows = jnp.transpose(dcum, (0, 2, 1)).reshape(T, H)
    out = pl.pallas_call(
        functools.partial(_augment_kernel, n_heads=H),
        grid=(T // tm,),
        in_specs=[pl.BlockSpec((tm, D), lambda i: (i, 0)),
                  pl.BlockSpec((tm, H), lambda i: (i, 0))],
        out_specs=pl.BlockSpec((tm, 2 * D), lambda i: (i, 0)),
        out_shape=jax.ShapeDtypeStruct((T, 2 * D), BF16),
        compiler_params=_params(("parallel",)),
        name="augment_keys",
    )(k_bf.reshape(T, D), dk_rows)
    return out.reshape(B, Lk, 2 * D)


def _trunk(x, p, s0, past, W):
    B, T, D = x.shape
    H = D // HEAD
    depth = p.shape[0]
    n_a = W["w_in_a"].shape[0]
    xf = x.reshape(B * T, D)
    pf = p.reshape(depth, B * T, p.shape[-1])
    states = []
    for layer in range(depth):
        final = layer == depth - 1
        if layer < n_a:
            z = _norm_matmul(xf, W["g_norm_a"][layer], W["w_in_a"][layer])
            og, s_fin = _hgrn_mix(z, W["lbs"][layer], W["g_out_a"][layer], s0[layer], B, T)
            states.append(s_fin)
            w_out = W["w_out_a"][layer]
        else:
            j = layer - n_a
            z = _norm_matmul(xf, W["g_norm_b"][j], W["w_in_b"][j])
            if k_aug is None:
                og = _fox_mix(z, k_all, v_all, dq, dk, B, T, q_off)
            else:
                qn = jnp.sqrt(_head_sq_sums(z, D)).reshape(B, T, H)
                bound = (FOX_BOUND_MARGIN * HEAD ** -0.5) * jnp.transpose(qn, (0, 2, 1)) * kmax
                c = (dcum - bound)[:, :, None, :]
                og = lax.cond(jnp.max(bound) <= FOX_BOUND_LIMIT,
                              lambda: _fox_mix_fast(z, c, k_aug, v_t, B, T),
                              lambda: _fox_mix(z, k_all, v_all, dq, dk, B, T, q_off))
            w_out = W["w_out_b"][j]
        xf = _out_ple(og, xf, pf[layer], w_out, W["g_ple"][layer], W["w_ple_gate"][layer],
                      W["w_ple_in"][layer], W["g_final"], final)
        if layer == n_a - 1:
            k_new, k_bf = _norm_matmul(xf, W["g_kv"], W["w_k"], with_bf16=True)
            v_new, v_bf = _norm_matmul(xf, W["g_kv"], W["w_v"], with_bf16=True)
            logf_new = _norm_matmul(xf, W["g_kv"], W["w_f"], bias=W["b_f"])[:, :H]
            logf_new = logf_new.reshape(B, T, H)
            k_bf = k_bf.reshape(B, T, D)
            v_bf = v_bf.reshape(B, T, D)
            if past is None:
                k_all, v_all, logf_all, q_off = k_bf, v_bf, logf_new, 0
            else:
                past_k, past_v, past_logf = past
                q_off = past_k.shape[1]
                k_all = jnp.concatenate([past_k.reshape(B, q_off, D).astype(BF16), k_bf], axis=1)
                v_all = jnp.concatenate([past_v.reshape(B, q_off, D).astype(BF16), v_bf], axis=1)
                logf_all = jnp.concatenate([past_logf.astype(F32), logf_new], axis=1)
            Lk = k_all.shape[1]
            pad = (-Lk) % LANES
            if pad:
                k_all = jnp.pad(k_all, ((0, 0), (0, pad), (0, 0)))
                v_all = jnp.pad(v_all, ((0, 0), (0, pad), (0, 0)))
                logf_all = jnp.pad(logf_all, ((0, 0), (0, pad), (0, 0)))
            rows = jnp.transpose(logf_all, (0, 2, 1)).reshape(B * H, Lk + pad)
            dcum = _cumsum_last(rows).reshape(B, H, Lk + pad)
            dk = dcum[:, :, None, :]
            dq = dcum[:, :, q_off:q_off + T, None]
            k_aug = None
            if past is None and T % FOX_FAST_TILE == 0:
                kn = jnp.sqrt(_head_sq_sums(k_new, D)).reshape(B, T, H)
                kmax = jnp.max(kn, axis=1)[:, :, None]
                k_aug = _augment_keys(k_bf, dcum, B, T, H)
                v_t = jnp.transpose(v_bf.reshape(B, T, H, HEAD), (0, 2, 3, 1))
    return (xf.reshape(B, T, D), jnp.stack(states), k_new.reshape(B, T, H, HEAD),
            v_new.reshape(B, T, H, HEAD), logf_new)


def kernel(x_prompt, x_sample, state_hgrn, cache_k, cache_v, cache_logf, p_prompt, p_sample,
           g_norm_a, w_in_a, lb_logits, g_out_a, w_out_a, g_kv, w_kv, b_f,
           g_norm_b, w_in_b, w_out_b, w_ple_in, g_ple, w_ple_gate, g_final):
    D = x_prompt.shape[-1]
    H = D // HEAD
    n_a = w_in_a.shape[0]
    lbs = jnp.cumsum(jax.nn.softmax(lb_logits.astype(F32), axis=0), axis=0)
    lbs = lbs - lbs[:1]
    w_f = jnp.pad(w_kv[:, 2 * D:], ((0, 0), (0, LANES - H)))
    W = dict(
        g_norm_a=g_norm_a, w_in_a=w_in_a.astype(BF16), lbs=lbs, g_out_a=g_out_a,
        w_out_a=w_out_a.astype(BF16), g_kv=g_kv, w_k=w_kv[:, :D].astype(BF16),
        w_v=w_kv[:, D:2 * D].astype(BF16), w_f=w_f.astype(BF16),
        b_f=jnp.pad(b_f, (0, LANES - H)), g_norm_b=g_norm_b, w_in_b=w_in_b.astype(BF16),
        w_out_b=w_out_b.astype(BF16), w_ple_in=w_ple_in.astype(BF16), g_ple=g_ple,
        w_ple_gate=w_ple_gate.astype(BF16), g_final=g_final)
    s0_prompt = jnp.zeros((n_a, x_prompt.shape[0], H, HEAD, HEAD), F32)
    y_p, st_p, k_p, v_p, f_p = _trunk(x_prompt, p_prompt, s0_prompt, None, W)
    y_s, st_s, k_s, v_s, f_s = _trunk(x_sample, p_sample, state_hgrn,
                                      (cache_k, cache_v, cache_logf), W)
    return (y_p, y_s, st_p, st_s, k_p, v_p, f_p, k_s, v_s, f_s)
```

```python
import functools

import jax
import jax.numpy as jnp
import numpy as np
from jax import lax
from jax.experimental import pallas as pl
from jax.experimental.pallas import tpu as pltpu

F32 = jnp.float32
BF16 = jnp.bfloat16

EPS = 1e-6
HEAD = 128
LANES = 128
SUBLANES = 8
CHUNK = 64
N_LEVELS = 6
N_TOP = 3
HGRN_HEADS = 16
FOX_SHORT_HEADS = 4
FOX_SHORT_QUERIES = 128
FOX_SHORT_KEYS = 2048
MASK_VALUE = -1e30
VMEM_LIMIT = 56 * 1024 * 1024

_NT = (((1,), (1,)), ((), ()))


def _params(sem, vmem=VMEM_LIMIT):
    return pltpu.CompilerParams(dimension_semantics=sem, vmem_limit_bytes=vmem)


def _sigmoid(x):
    return 1.0 / (1.0 + jnp.exp(-x))


def _pick(n, prefs):
    for p in prefs:
        if n % p == 0:
            return p
    return n


def _nmm_kernel(*refs, log_sigmoid, with_bf16, row_chunk):
    if log_sigmoid:
        x_ref, g_ref, w_ref, b_ref = refs[:4]
        rest = refs[4:]
    else:
        x_ref, g_ref, w_ref = refs[:3]
        b_ref = None
        rest = refs[3:]
    if with_bf16:
        o_ref, ob_ref, xn_ref = rest
    else:
        o_ref, xn_ref = rest
        ob_ref = None
    tm = x_ref.shape[0]

    @pl.when(pl.program_id(1) == 0)
    def _():
        for r in range(0, tm, row_chunk):
            x = x_ref[r:r + row_chunk, :]
            ms = jnp.mean(x * x, axis=-1, keepdims=True)
            xn_ref[r:r + row_chunk, :] = (x * lax.rsqrt(ms + EPS) * g_ref[...]).astype(BF16)

    z = jnp.dot(xn_ref[...], w_ref[...], preferred_element_type=F32)
    if log_sigmoid:
        z = z + b_ref[...]
        z = jnp.minimum(z, 0.0) - jnp.log(1.0 + jnp.exp(-jnp.abs(z)))
    o_ref[...] = z
    if with_bf16:
        ob_ref[...] = z.astype(BF16)


def _norm_matmul(x, g, w, bias=None, with_bf16=False):
    T, D = x.shape
    N = w.shape[1]
    tm = _pick(T, (1024, 512, 256, 128, 64))
    tn = _pick(N, (1024, 512, 256, 128))
    row_chunk = min(tm, 256)
    in_specs = [
        pl.BlockSpec((tm, D), lambda i, j: (i, 0)),
        pl.BlockSpec((1, D), lambda i, j: (0, 0)),
        pl.BlockSpec((D, tn), lambda i, j: (0, j)),
    ]
    args = [x, g.reshape(1, D).astype(F32), w]
    if bias is not None:
        in_specs.append(pl.BlockSpec((1, tn), lambda i, j: (0, j)))
        args.append(bias.reshape(1, N).astype(F32))
    out_shape = [jax.ShapeDtypeStruct((T, N), F32)]
    out_specs = [pl.BlockSpec((tm, tn), lambda i, j: (i, j))]
    if with_bf16:
        out_shape.append(jax.ShapeDtypeStruct((T, N), BF16))
        out_specs.append(pl.BlockSpec((tm, tn), lambda i, j: (i, j)))
    out = pl.pallas_call(
        functools.partial(_nmm_kernel, log_sigmoid=bias is not None, with_bf16=with_bf16,
                          row_chunk=row_chunk),
        grid=(T // tm, N // tn),
        in_specs=in_specs,
        out_specs=out_specs,
        out_shape=out_shape,
        scratch_shapes=[pltpu.VMEM((tm, D), BF16)],
        compiler_params=_params(("parallel", "arbitrary")),
        name="norm_matmul",
    )(*args)
    return out if with_bf16 else out[0]


def _hgrn_constants():
    C = CHUNK
    t = np.arange(C)[:, None]
    u = np.arange(C)[None, :]
    blocks = [(u <= t), (u > t)]
    masks = []
    for lvl in range(N_LEVELS):
        h = C >> (lvl + 1)
        mid = (t // (2 * h)) * (2 * h) + h
        if N_TOP <= lvl < N_LEVELS - 1:
            blocks.append(np.where(t >= mid, (u >= mid) & (u <= t), (u > t) & (u < mid)))
        mid_s = (u // (2 * h)) * (2 * h) + h
        masks.append((t // (2 * h) == u // (2 * h)) & (t >= mid) & (u < mid_s))
    masks.append(t == u)
    sm = np.concatenate(blocks, axis=0).astype(np.float32)
    mk = np.stack(masks, axis=0).astype(np.float32)
    return jnp.asarray(sm, BF16), jnp.asarray(mk, F32)


def _midpoint_rows(G, h):
    rows = [jnp.broadcast_to(G[b + h - 1:b + h, :], (2 * h, HEAD)) for b in range(0, CHUNK, 2 * h)]
    return rows[0] if len(rows) == 1 else jnp.concatenate(rows, axis=0)


def _hgrn_kernel(zq_ref, zf_ref, zi_ref, zg_ref, lb_ref, go_ref, s0_ref, sm_ref, mk_ref,
                 o_ref, sout_ref, st_ref, *, n_chunks, n_heads):
    C = CHUNK
    tb = pl.program_id(2)

    @pl.when(tb == 0)
    def _():
        for hh in range(n_heads):
            st_ref[hh] = s0_ref[hh].T

    odd_row = lax.broadcasted_iota(jnp.int32, (C, HEAD), 0) % 2 == 1

    def chunk(c, carry):
        r0 = pl.multiple_of(c * C, C)
        heads = range(n_heads)
        cols = [slice(hh * HEAD, (hh + 1) * HEAD) for hh in heads]
        q, k, v, f, d2 = [], [], [], [], []
        for hh in heads:
            lb = lb_ref[:, cols[hh]]
            zq = zq_ref[pl.ds(r0, C), cols[hh]]
            zf = zf_ref[pl.ds(r0, C), cols[hh]]
            q.append(zq * _sigmoid(zq))
            f.append(lb + (1.0 - lb) * _sigmoid(zf))
            g = jnp.log(f[hh])
            k.append(1.0 - f[hh])
            v.append(zi_ref[pl.ds(r0, C), cols[hh]])
            g_hi = g.astype(BF16)
            g_lo = (g - g_hi.astype(F32)).astype(BF16)
            d2.append(jnp.dot(sm_ref[...], jnp.concatenate([g_hi, g_lo], axis=1),
                              preferred_element_type=F32))
        e, lv = [], []
        for hh in heads:
            d = d2[hh][:, :HEAD] + d2[hh][:, HEAD:]
            e.append(jnp.exp(d))
            G = d[0:C]
            top = [jnp.exp(-jnp.abs(G - _midpoint_rows(G, C >> (lvl + 1)))) for lvl in range(N_TOP)]
            lv.append(top + [e[hh][2 * C:3 * C], e[hh][3 * C:4 * C],
                             jnp.where(odd_row, f[hh], 1.0)])
        a = []
        for hh in heads:
            parts = [lax.dot_general(q[hh].astype(BF16), k[hh].astype(BF16), _NT,
                                     preferred_element_type=F32)]
            for lvl in range(N_LEVELS):
                el = lv[hh][lvl]
                parts.append(lax.dot_general((q[hh] * el).astype(BF16), (k[hh] * el).astype(BF16),
                                             _NT, preferred_element_type=F32))
            acc = mk_ref[N_LEVELS] * parts[0]
            for lvl in range(N_LEVELS):
                acc = acc + mk_ref[lvl] * parts[1 + lvl]
            a.append(acc)
        o = []
        for hh in heads:
            st = st_ref[hh]
            qg = (q[hh] * e[hh][0:C]).astype(BF16)
            kg = (k[hh] * e[hh][C:2 * C]).astype(BF16)
            vb = v[hh].astype(BF16)
            oh = jnp.dot(a[hh].astype(BF16), vb, preferred_element_type=F32)
            o.append(oh + lax.dot_general(qg, st.astype(BF16), _NT, preferred_element_type=F32))
            st_ref[hh] = st * e[hh][C - 1:C, :] + jnp.dot(v[hh].T.astype(BF16), kg,
                                                          preferred_element_type=F32)
        for hh in heads:
            zg = zg_ref[pl.ds(r0, C), cols[hh]]
            ms = jnp.mean(o[hh] * o[hh], axis=-1, keepdims=True)
            og = o[hh] * lax.rsqrt(ms + EPS) * go_ref[:, cols[hh]] * (zg * _sigmoid(zg))
            o_ref[pl.ds(r0, C), cols[hh]] = og.astype(o_ref.dtype)
        return carry

    lax.fori_loop(0, n_chunks, chunk, 0)

    @pl.when(tb == pl.num_programs(2) - 1)
    def _():
        for hh in range(n_heads):
            sout_ref[hh] = st_ref[hh].T


def _hgrn_mix(z, lb, g_out, s0, B, T):
    D = z.shape[1] // 4
    H = D // HEAD
    hb = _pick(H, (HGRN_HEADS,))
    tb = _pick(T, (256, 128, 64))
    nt = T // tb
    nh = H // hb
    sm, mk = _hgrn_constants()

    def zspec(part):
        return pl.BlockSpec((tb, hb * HEAD), lambda b, h, t: (b * nt + t, part * nh + h))

    head_vec = pl.BlockSpec((1, hb * HEAD), lambda b, h, t: (0, h))
    state = pl.BlockSpec((None, hb, HEAD, HEAD), lambda b, h, t: (b, h, 0, 0))
    og, s_fin = pl.pallas_call(
        functools.partial(_hgrn_kernel, n_chunks=tb // CHUNK, n_heads=hb),
        grid=(B, nh, nt),
        in_specs=[zspec(0), zspec(1), zspec(2), zspec(3), head_vec, head_vec, state,
                  pl.BlockSpec(sm.shape, lambda b, h, t: (0, 0)),
                  pl.BlockSpec(mk.shape, lambda b, h, t: (0, 0, 0))],
        out_specs=[pl.BlockSpec((tb, hb * HEAD), lambda b, h, t: (b * nt + t, h)), state],
        out_shape=[jax.ShapeDtypeStruct((B * T, D), BF16),
                   jax.ShapeDtypeStruct((B, H, HEAD, HEAD), F32)],
        scratch_shapes=[pltpu.VMEM((hb, HEAD, HEAD), F32)],
        compiler_params=_params(("parallel", "parallel", "arbitrary")),
        name="hgrn_mix",
    )(z, z, z, z, lb.reshape(1, D), g_out.reshape(1, D).astype(F32), s0, sm, mk)
    return og, s_fin


def _out_ple_kernel(og_ref, x_ref, p_ref, wo_ref, gpg_ref, wpg_ref, wpin_ref, gfin_ref, o_ref,
                    *, final, col_chunk):
    D = x_ref.shape[1]
    y = x_ref[...] + jnp.dot(og_ref[...], wo_ref[...], preferred_element_type=F32)
    ms = jnp.mean(y * y, axis=-1, keepdims=True)
    yn = (y * lax.rsqrt(ms + EPS) * gpg_ref[...]).astype(BF16)
    pb = p_ref[...].astype(BF16)
    parts = []
    for c0 in range(0, D, col_chunk):
        gate = _sigmoid(jnp.dot(yn, wpg_ref[:, c0:c0 + col_chunk], preferred_element_type=F32))
        emb = jnp.dot(pb, wpin_ref[:, c0:c0 + col_chunk], preferred_element_type=F32)
        parts.append(y[:, c0:c0 + col_chunk] + gate * emb)
    if final:
        ms2 = sum(jnp.sum(t * t, axis=-1, keepdims=True) for t in parts) * (1.0 / D)
        inv = lax.rsqrt(ms2 + EPS)
        for n, c0 in enumerate(range(0, D, col_chunk)):
            o_ref[:, c0:c0 + col_chunk] = parts[n] * inv * gfin_ref[:, c0:c0 + col_chunk]
    else:
        for n, c0 in enumerate(range(0, D, col_chunk)):
            o_ref[:, c0:c0 + col_chunk] = parts[n]


def _out_ple(og, x, p, w_out, g_pg, w_pg, w_pin, g_final, final):
    T, D = x.shape
    P = p.shape[1]
    tm = _pick(T, (256, 128, 64))
    col_chunk = min(D, 512)
    rows = lambda i: (i, 0)
    fixed = lambda i: (0, 0)
    once = pl.Buffered(1)
    return pl.pallas_call(
        functools.partial(_out_ple_kernel, final=final, col_chunk=col_chunk),
        grid=(T // tm,),
        in_specs=[pl.BlockSpec((tm, D), rows), pl.BlockSpec((tm, D), rows),
                  pl.BlockSpec((tm, P), rows),
                  pl.BlockSpec((D, D), fixed, pipeline_mode=once),
                  pl.BlockSpec((1, D), fixed),
                  pl.BlockSpec((D, D), fixed, pipeline_mode=once),
                  pl.BlockSpec((P, D), fixed, pipeline_mode=once),
                  pl.BlockSpec((1, D), fixed)],
        out_specs=pl.BlockSpec((tm, D), rows),
        out_shape=jax.ShapeDtypeStruct((T, D), F32),
        compiler_params=_params(("parallel",)),
        name="out_ple",
    )(og, x, p, w_out, g_pg.reshape(1, D).astype(F32), w_pg, w_pin,
      g_final.reshape(1, D).astype(F32))


def _cumsum_kernel(x_ref, u_ref, o_ref, carry_ref, *, n_sub):
    @pl.when(pl.program_id(0) == 0)
    def _():
        carry_ref[...] = jnp.zeros_like(carry_ref)

    u = u_ref[...]
    c = carry_ref[:, 0:1]
    for s in range(n_sub):
        x = x_ref[:, s * LANES:(s + 1) * LANES]
        hi = x.astype(BF16)
        r1 = x - hi.astype(F32)
        mid = r1.astype(BF16)
        lo = (r1 - mid.astype(F32)).astype(BF16)
        cs = (jnp.dot(hi, u, preferred_element_type=F32)
              + jnp.dot(mid, u, preferred_element_type=F32)
              + jnp.dot(lo, u, preferred_element_type=F32)) + c
        o_ref[:, s * LANES:(s + 1) * LANES] = cs
        c = cs[:, LANES - 1:LANES]
    carry_ref[...] = jnp.broadcast_to(c, carry_ref.shape)


def _cumsum_last(x):
    R, L = x.shape
    cb = _pick(L, (2048, 1024, 512, 256, 128))
    u = jnp.asarray(np.triu(np.ones((LANES, LANES), np.float32)), BF16)
    return pl.pallas_call(
        functools.partial(_cumsum_kernel, n_sub=cb // LANES),
        grid=(L // cb,),
        in_specs=[pl.BlockSpec((R, cb), lambda j: (0, j)),
                  pl.BlockSpec((LANES, LANES), lambda j: (0, 0))],
        out_specs=pl.BlockSpec((R, cb), lambda j: (0, j)),
        out_shape=jax.ShapeDtypeStruct((R, L), F32),
        scratch_shapes=[pltpu.VMEM((R, LANES), F32)],
        compiler_params=_params(("arbitrary",)),
        name="cumsum",
    )(x, u)


def _fox_kernel(q_ref, gate_ref, k_ref, v_ref, dq_ref, dk_ref, o_ref, *, tq, tk, q_off, scale):
    i = pl.program_id(2)
    q = q_ref[...].astype(BF16)
    dq = dq_ref[...]
    first_pos = q_off + i * tq
    n_full = (first_pos + 1) // tk
    n_all = (first_pos + tq - 1) // tk + 1

    def step(j, carry, masked):
        m, l, acc = carry
        start = pl.multiple_of(j * tk, tk)
        k = k_ref[pl.ds(start, tk), :]
        v = v_ref[pl.ds(start, tk), :]
        s = lax.dot_general(q, k, _NT, preferred_element_type=F32) * scale
        s = s + dq - dk_ref[:, pl.ds(start, tk)]
        if masked:
            kpos = start + lax.broadcasted_iota(jnp.int32, (tq, tk), 1)
            qpos = first_pos + lax.broadcasted_iota(jnp.int32, (tq, tk), 0)
            s = jnp.where(kpos <= qpos, s, MASK_VALUE)
        m_new = jnp.maximum(m, jnp.max(s, axis=-1, keepdims=True))
        alpha = jnp.exp(m - m_new)
        p = jnp.exp(s - m_new)
        l = alpha * l + jnp.sum(p, axis=-1, keepdims=True)
        acc = alpha * acc + jnp.dot(p.astype(BF16), v, preferred_element_type=F32)
        return m_new, l, acc

    init = (jnp.full((tq, 1), MASK_VALUE, F32), jnp.zeros((tq, 1), F32),
            jnp.zeros((tq, HEAD), F32))
    carry = lax.fori_loop(0, n_full, functools.partial(step, masked=False), init)
    _, l, acc = lax.fori_loop(n_full, n_all, functools.partial(step, masked=True), carry)
    gate = gate_ref[...]
    o = acc * (1.0 / l)
    o_ref[...] = (o * (gate * _sigmoid(gate))).astype(o_ref.dtype)


def _fox_short_kernel(q_ref, gate_ref, k_ref, v_ref, dq_ref, dk_ref, o_ref, *, n_heads, q_off,
                      scale):
    T, Lk = q_ref.shape[0], k_ref.shape[0]
    heads = range(n_heads)
    cols = [slice(hh * HEAD, (hh + 1) * HEAD) for hh in heads]
    visible = (lax.broadcasted_iota(jnp.int32, (T, Lk), 1)
               <= q_off + lax.broadcasted_iota(jnp.int32, (T, Lk), 0))
    s = []
    for hh in heads:
        sh = lax.dot_general(q_ref[:, cols[hh]].astype(BF16), k_ref[:, cols[hh]], _NT,
                             preferred_element_type=F32) * scale
        s.append(jnp.where(visible, sh + dq_ref[hh] - dk_ref[hh], MASK_VALUE))
    p = [jnp.exp(sh - jnp.max(sh, axis=-1, keepdims=True)) for sh in s]
    o = [jnp.dot(p[hh].astype(BF16), v_ref[:, cols[hh]], preferred_element_type=F32)
         for hh in heads]
    for hh in heads:
        gate = gate_ref[:, cols[hh]]
        oh = o[hh] * (1.0 / jnp.sum(p[hh], axis=-1, keepdims=True))
        o_ref[:, cols[hh]] = (oh * (gate * _sigmoid(gate))).astype(o_ref.dtype)


def _fox_mix_short(z, k_all, v_all, dq, dk, B, T, q_off):
    D = z.shape[1] // 2
    H = D // HEAD
    Lk = k_all.shape[1]
    hb = _pick(H, (FOX_SHORT_HEADS,))
    nh = H // hb
    kv = pl.BlockSpec((None, Lk, hb * HEAD), lambda b, h: (b, 0, h))
    return pl.pallas_call(
        functools.partial(_fox_short_kernel, n_heads=hb, q_off=q_off, scale=HEAD ** -0.5),
        grid=(B, nh),
        in_specs=[pl.BlockSpec((T, hb * HEAD), lambda b, h: (b, h)),
                  pl.BlockSpec((T, hb * HEAD), lambda b, h: (b, nh + h)),
                  kv, kv,
                  pl.BlockSpec((None, hb, T, 1), lambda b, h: (b, h, 0, 0)),
                  pl.BlockSpec((None, hb, 1, Lk), lambda b, h: (b, h, 0, 0))],
        out_specs=pl.BlockSpec((T, hb * HEAD), lambda b, h: (b, h)),
        out_shape=jax.ShapeDtypeStruct((B * T, D), BF16),
        compiler_params=_params(("parallel", "parallel")),
        name="fox_mix_short",
    )(z, z, k_all, v_all, dq, dk)


def _fox_mix(z, k_all, v_all, dq, dk, B, T, q_off):
    D = z.shape[1] // 2
    H = D // HEAD
    Lk = k_all.shape[1]
    if T <= FOX_SHORT_QUERIES and Lk <= FOX_SHORT_KEYS:
        return _fox_mix_short(z, k_all, v_all, dq, dk, B, T, q_off)
    tq = _pick(T, (512, 256, 128, 64))
    tk = Lk if Lk <= 2048 else _pick(Lk, (512, 256, 128))
    nq = T // tq
    kv = pl.BlockSpec((None, Lk, HEAD), lambda b, h, i: (b, 0, h))
    return pl.pallas_call(
        functools.partial(_fox_kernel, tq=tq, tk=tk, q_off=q_off, scale=HEAD ** -0.5),
        grid=(B, H, nq),
        in_specs=[pl.BlockSpec((tq, HEAD), lambda b, h, i: (b * nq + i, h)),
                  pl.BlockSpec((tq, HEAD), lambda b, h, i: (b * nq + i, H + h)),
                  kv, kv,
                  pl.BlockSpec((None, None, tq, 1), lambda b, h, i: (b, h, i, 0)),
                  pl.BlockSpec((None, None, 1, Lk), lambda b, h, i: (b, h, 0, 0))],
        out_specs=pl.BlockSpec((tq, HEAD), lambda b, h, i: (b * nq + i, h)),
        out_shape=jax.ShapeDtypeStruct((B * T, D), BF16),
        compiler_params=_params(("parallel", "parallel", "arbitrary")),
        name="fox_mix",
    )(z, z, k_all, v_all, dq, dk)


FOX_BOUND_MARGIN = 1.02
FOX_BOUND_LIMIT = 32.0
FOX_FAST_TILE = 1024
AUG = 6


def _top16(x):
    bits = lax.bitcast_convert_type(x, jnp.uint32) & jnp.uint32(0xFFFF0000)
    return lax.bitcast_convert_type(bits, F32)


def _split3(x):
    hi = _top16(x)
    r1 = x - hi
    mid = _top16(r1)
    lo = r1 - mid
    return hi.astype(BF16), mid.astype(BF16), lo.astype(BF16)


def _bias_columns(x, value_first):
    lane = lax.broadcasted_iota(jnp.int32, x.shape, 1)
    v0, o0 = (0, AUG // 2) if value_first else (AUG // 2, 0)
    hi, mid, lo = _split3(x)
    ones = (lane >= o0) & (lane < o0 + AUG // 2)
    cols = jnp.where(lane == v0, hi.astype(F32), jnp.where(lane == v0 + 1, mid.astype(F32),
           jnp.where(lane == v0 + 2, lo.astype(F32), jnp.where(ones, 1.0, 0.0))))
    return cols.astype(BF16)


def _sq_sums_kernel(x_ref, e_ref, o_ref):
    x = x_ref[...]
    o_ref[...] = jnp.dot((x * x).astype(BF16), e_ref[...], preferred_element_type=F32)


def _head_sq_sums(x, D):
    T = x.shape[0]
    H = D // HEAD
    tm = _pick(T, (512, 256, 128, 64))
    e = np.zeros((D, LANES), np.float32)
    e[np.arange(D), np.arange(D) // HEAD] = 1.0
    return pl.pallas_call(
        _sq_sums_kernel,
        grid=(T // tm,),
        in_specs=[pl.BlockSpec((tm, D), lambda i: (i, 0)),
                  pl.BlockSpec((D, LANES), lambda i: (0, 0))],
        out_specs=pl.BlockSpec((tm, LANES), lambda i: (i, 0)),
        out_shape=jax.ShapeDtypeStruct((T, LANES), F32),
        compiler_params=_params(("parallel",)),
        name="head_sq_sums",
    )(x, jnp.asarray(e, BF16))[:, :H]


def _fox_fast_kernel(q_ref, c_ref, gate_ref, ka_ref, vt_ref, o_ref, qa_ref, *, tile, scale):
    i = pl.program_id(2)
    qa_ref[:, :HEAD] = (q_ref[...] * scale).astype(BF16)
    c_col = jnp.broadcast_to(c_ref[...], (HEAD, tile)).T
    qa_ref[:, HEAD:] = _bias_columns(c_col, value_first=True)

    def scores(j, masked):
        start = pl.multiple_of(j * tile, tile)
        s = lax.dot_general(ka_ref[pl.ds(start, tile), :], qa_ref[...], _NT,
                            preferred_element_type=F32)
        if masked:
            kpos = lax.broadcasted_iota(jnp.int32, (tile, tile), 0)
            qpos = lax.broadcasted_iota(jnp.int32, (tile, tile), 1)
            s = jnp.where(kpos <= qpos, s, MASK_VALUE)
        return s

    def consume(s, j, lp, acc):
        start = pl.multiple_of(j * tile, tile)
        p = jnp.exp(s)
        for g in range(tile // SUBLANES):
            lp = lp + p[g * SUBLANES:(g + 1) * SUBLANES, :]
        acc = acc + jnp.dot(vt_ref[:, pl.ds(start, tile)], p.astype(BF16),
                            preferred_element_type=F32)
        return lp, acc

    def pair(j0, carry, masked_b):
        lp, acc = carry
        sa = scores(j0, False)
        sb = scores(j0 + 1, masked_b)
        lp, acc = consume(sa, j0, lp, acc)
        return consume(sb, j0 + 1, lp, acc)

    init = (jnp.zeros((SUBLANES, tile), F32), jnp.zeros((HEAD, tile), F32))
    carry = lax.fori_loop(0, i // 2, lambda t, c: pair(2 * t, c, False), init)
    lp, acc = lax.cond(i % 2 == 1, lambda c: pair(i - 1, c, True),
                       lambda c: consume(scores(i, True), i, *c), carry)
    gate = gate_ref[...]
    o = (acc * (1.0 / jnp.sum(lp, axis=0, keepdims=True))).T
    o_ref[...] = (o * (gate * _sigmoid(gate))).astype(o_ref.dtype)


def _fox_mix_fast(z, c, k_aug, v_t, B, T):
    D = z.shape[1] // 2
    H = D // HEAD
    tile = FOX_FAST_TILE
    nq = T // tile
    return pl.pallas_call(
        functools.partial(_fox_fast_kernel, tile=tile, scale=HEAD ** -0.5),
        grid=(B, H, nq),
        in_specs=[pl.BlockSpec((tile, HEAD), lambda b, h, i: (b * nq + i, h)),
                  pl.BlockSpec((None, None, 1, tile), lambda b, h, i: (b, h, 0, i)),
                  pl.BlockSpec((tile, HEAD), lambda b, h, i: (b * nq + i, H + h)),
                  pl.BlockSpec((None, T, 2 * HEAD), lambda b, h, i: (b, 0, h)),
                  pl.BlockSpec((None, None, HEAD, T), lambda b, h, i: (b, h, 0, 0))],
        out_specs=pl.BlockSpec((tile, HEAD), lambda b, h, i: (b * nq + i, h)),
        out_shape=jax.ShapeDtypeStruct((B * T, D), BF16),
        scratch_shapes=[pltpu.VMEM((tile, 2 * HEAD), BF16)],
        compiler_params=_params(("parallel", "parallel", "arbitrary")),
        name="fox_mix_fast",
    )(z, c, z, k_aug, v_t)


def _augment_kernel(k_ref, dk_ref, o_ref, *, n_heads):
    tm = k_ref.shape[0]
    for h in range(n_heads):
        o_ref[:, 2 * h * HEAD:(2 * h + 1) * HEAD] = k_ref[:, h * HEAD:(h + 1) * HEAD].astype(BF16)
        col = jnp.broadcast_to(-dk_ref[:, h:h + 1], (tm, HEAD))
        o_ref[:, (2 * h + 1) * HEAD:(2 * h + 2) * HEAD] = _bias_columns(col, value_first=False)


def _augment_keys(k, dcum, B, Lk, H):
    D = H * HEAD
    T = B * Lk
    tm = _pick(T, (512, 256, 128, 64))
    dk_rows = jnp.transpose(dcum, (0, 2, 1)).reshape(T, H)
    out = pl.pallas_call(
        functools.partial(_augment_kernel, n_heads=H),
        grid=(T // tm,),
        in_specs=[pl.BlockSpec((tm, D), lambda i: (i, 0)),
                  pl.BlockSpec((tm, H), lambda i: (i, 0))],
        out_specs=pl.BlockSpec((tm, 2 * D), lambda i: (i, 0)),
        out_shape=jax.ShapeDtypeStruct((T, 2 * D), BF16),
        compiler_params=_params(("parallel",)),
        name="augment_keys",
    )(k, dk_rows)
    return out.reshape(B, Lk, 2 * D)


def _trunk(x, p, s0, past, W):
    B, T, D = x.shape
    H = D // HEAD
    depth = p.shape[0]
    n_a = W["w_in_a"].shape[0]
    xf = x.reshape(B * T, D)
    pf = p.reshape(depth, B * T, p.shape[-1])
    states = []
    for layer in range(depth):
        final = layer == depth - 1
        if layer < n_a:
            z = _norm_matmul(xf, W["g_norm_a"][layer], W["w_in_a"][layer])
            og, s_fin = _hgrn_mix(z, W["lbs"][layer], W["g_out_a"][layer], s0[layer], B, T)
            states.append(s_fin)
            w_out = W["w_out_a"][layer]
        else:
            j = layer - n_a
            z = _norm_matmul(xf, W["g_norm_b"][j], W["w_in_b"][j])
            if k_aug is None:
                og = _fox_mix(z, k_all, v_all, dq, dk, B, T, q_off)
            else:
                qn = jnp.sqrt(_head_sq_sums(z, D)).reshape(B, T, H)
                bound = (FOX_BOUND_MARGIN * HEAD ** -0.5) * jnp.transpose(qn, (0, 2, 1)) * kmax
                c = (dcum - bound)[:, :, None, :]
                og = lax.cond(jnp.max(bound) <= FOX_BOUND_LIMIT,
                              lambda: _fox_mix_fast(z, c, k_aug, v_t, B, T),
                              lambda: _fox_mix(z, k_new.reshape(B, T, D).astype(BF16), v_all, dq,
                                               dk, B, T, q_off))
            w_out = W["w_out_b"][j]
        xf = _out_ple(og, xf, pf[layer], w_out, W["g_ple"][layer], W["w_ple_gate"][layer],
                      W["w_ple_in"][layer], W["g_final"], final)
        if layer == n_a - 1:
            fast = past is None and T % FOX_FAST_TILE == 0
            if fast:
                k_new, k_bf = _norm_matmul(xf, W["g_kv"], W["w_k"]), None
            else:
                k_new, k_bf = _norm_matmul(xf, W["g_kv"], W["w_k"], with_bf16=True)
                k_bf = k_bf.reshape(B, T, D)
            v_new, v_bf = _norm_matmul(xf, W["g_kv"], W["w_v"], with_bf16=True)
            logf_new = _norm_matmul(xf, W["g_kv"], W["w_f"], bias=W["b_f"])[:, :H]
            logf_new = logf_new.reshape(B, T, H)
            v_bf = v_bf.reshape(B, T, D)
            if past is None:
                k_all, v_all, logf_all, q_off = k_bf, v_bf, logf_new, 0
            else:
                past_k, past_v, past_logf = past
                q_off = past_k.shape[1]
                k_all = jnp.concatenate([past_k.reshape(B, q_off, D).astype(BF16), k_bf], axis=1)
                v_all = jnp.concatenate([past_v.reshape(B, q_off, D).astype(BF16), v_bf], axis=1)
                logf_all = jnp.concatenate([past_logf.astype(F32), logf_new], axis=1)
            Lk = v_all.shape[1]
            pad = (-Lk) % LANES
            if pad:
                k_all = jnp.pad(k_all, ((0, 0), (0, pad), (0, 0)))
                v_all = jnp.pad(v_all, ((0, 0), (0, pad), (0, 0)))
                logf_all = jnp.pad(logf_all, ((0, 0), (0, pad), (0, 0)))
            rows = jnp.transpose(logf_all, (0, 2, 1)).reshape(B * H, Lk + pad)
            dcum = _cumsum_last(rows).reshape(B, H, Lk + pad)
            dk = dcum[:, :, None, :]
            dq = dcum[:, :, q_off:q_off + T, None]
            k_aug = None
            if fast:
                kn = jnp.sqrt(_head_sq_sums(k_new, D)).reshape(B, T, H)
                kmax = jnp.max(kn, axis=1)[:, :, None]
                k_aug = _augment_keys(k_new, dcum, B, T, H)
                v_t = jnp.transpose(v_bf.reshape(B, T, H, HEAD), (0, 2, 3, 1))
    return (xf.reshape(B, T, D), jnp.stack(states), k_new.reshape(B, T, H, HEAD),
            v_new.reshape(B, T, H, HEAD), logf_new)


def kernel(x_prompt, x_sample, state_hgrn, cache_k, cache_v, cache_logf, p_prompt, p_sample,
           g_norm_a, w_in_a, lb_logits, g_out_a, w_out_a, g_kv, w_kv, b_f,
           g_norm_b, w_in_b, w_out_b, w_ple_in, g_ple, w_ple_gate, g_final):
    D = x_prompt.shape[-1]
    H = D // HEAD
    n_a = w_in_a.shape[0]
    lbs = jnp.cumsum(jax.nn.softmax(lb_logits.astype(F32), axis=0), axis=0)
    lbs = lbs - lbs[:1]
    w_f = jnp.pad(w_kv[:, 2 * D:], ((0, 0), (0, LANES - H)))
    W = dict(
        g_norm_a=g_norm_a, w_in_a=w_in_a.astype(BF16), lbs=lbs, g_out_a=g_out_a,
        w_out_a=w_out_a.astype(BF16), g_kv=g_kv, w_k=w_kv[:, :D].astype(BF16),
        w_v=w_kv[:, D:2 * D].astype(BF16), w_f=w_f.astype(BF16),
        b_f=jnp.pad(b_f, (0, LANES - H)), g_norm_b=g_norm_b, w_in_b=w_in_b.astype(BF16),
        w_out_b=w_out_b.astype(BF16), w_ple_in=w_ple_in.astype(BF16), g_ple=g_ple,
        w_ple_gate=w_ple_gate.astype(BF16), g_final=g_final)
    s0_prompt = jnp.zeros((n_a, x_prompt.shape[0], H, HEAD, HEAD), F32)
    y_p, st_p, k_p, v_p, f_p = _trunk(x_prompt, p_prompt, s0_prompt, None, W)
    y_s, st_s, k_s, v_s, f_s = _trunk(x_sample, p_sample, state_hgrn,
                                      (cache_k, cache_v, cache_logf), W)
    return (y_p, y_s, st_p, st_s, k_p, v_p, f_p, k_s, v_s, f_s)
```

```python
import functools

import jax
import jax.numpy as jnp
import numpy as np
from jax import lax
from jax.experimental import pallas as pl
from jax.experimental.pallas import tpu as pltpu

F32 = jnp.float32
BF16 = jnp.bfloat16

EPS = 1e-6
HEAD = 128
LANES = 128
SUBLANES = 8
CHUNK = 64
N_LEVELS = 6
N_TOP = 3
HGRN_HEADS = 16
FOX_SHORT_HEADS = 4
FOX_SHORT_QUERIES = 128
FOX_SHORT_KEYS = 2048
MASK_VALUE = -1e30
VMEM_LIMIT = 56 * 1024 * 1024

_NT = (((1,), (1,)), ((), ()))


def _params(sem, vmem=VMEM_LIMIT):
    return pltpu.CompilerParams(dimension_semantics=sem, vmem_limit_bytes=vmem)


def _sigmoid(x):
    return 1.0 / (1.0 + jnp.exp(-x))


def _pick(n, prefs):
    for p in prefs:
        if n % p == 0:
            return p
    return n


def _nmm_kernel(*refs, log_sigmoid, with_bf16, row_chunk):
    if log_sigmoid:
        x_ref, g_ref, w_ref, b_ref = refs[:4]
        rest = refs[4:]
    else:
        x_ref, g_ref, w_ref = refs[:3]
        b_ref = None
        rest = refs[3:]
    if with_bf16:
        o_ref, ob_ref, xn_ref = rest
    else:
        o_ref, xn_ref = rest
        ob_ref = None
    tm = x_ref.shape[0]

    @pl.when(pl.program_id(1) == 0)
    def _():
        for r in range(0, tm, row_chunk):
            x = x_ref[r:r + row_chunk, :]
            ms = jnp.mean(x * x, axis=-1, keepdims=True)
            xn_ref[r:r + row_chunk, :] = (x * lax.rsqrt(ms + EPS) * g_ref[...]).astype(BF16)

    z = jnp.dot(xn_ref[...], w_ref[...], preferred_element_type=F32)
    if log_sigmoid:
        z = z + b_ref[...]
        z = jnp.minimum(z, 0.0) - jnp.log(1.0 + jnp.exp(-jnp.abs(z)))
    o_ref[...] = z
    if with_bf16:
        ob_ref[...] = z.astype(BF16)


def _norm_matmul(x, g, w, bias=None, with_bf16=False):
    T, D = x.shape
    N = w.shape[1]
    tm = _pick(T, (1024, 512, 256, 128, 64))
    tn = _pick(N, (1024, 512, 256, 128))
    row_chunk = min(tm, 256)
    in_specs = [
        pl.BlockSpec((tm, D), lambda i, j: (i, 0)),
        pl.BlockSpec((1, D), lambda i, j: (0, 0)),
        pl.BlockSpec((D, tn), lambda i, j: (0, j)),
    ]
    args = [x, g.reshape(1, D).astype(F32), w]
    if bias is not None:
        in_specs.append(pl.BlockSpec((1, tn), lambda i, j: (0, j)))
        args.append(bias.reshape(1, N).astype(F32))
    out_shape = [jax.ShapeDtypeStruct((T, N), F32)]
    out_specs = [pl.BlockSpec((tm, tn), lambda i, j: (i, j))]
    if with_bf16:
        out_shape.append(jax.ShapeDtypeStruct((T, N), BF16))
        out_specs.append(pl.BlockSpec((tm, tn), lambda i, j: (i, j)))
    out = pl.pallas_call(
        functools.partial(_nmm_kernel, log_sigmoid=bias is not None, with_bf16=with_bf16,
                          row_chunk=row_chunk),
        grid=(T // tm, N // tn),
        in_specs=in_specs,
        out_specs=out_specs,
        out_shape=out_shape,
        scratch_shapes=[pltpu.VMEM((tm, D), BF16)],
        compiler_params=_params(("parallel", "arbitrary")),
        name="norm_matmul",
    )(*args)
    return out if with_bf16 else out[0]


def _hgrn_constants():
    C = CHUNK
    t = np.arange(C)[:, None]
    u = np.arange(C)[None, :]
    blocks = [(u <= t), (u > t)]
    masks = []
    for lvl in range(N_LEVELS):
        h = C >> (lvl + 1)
        mid = (t // (2 * h)) * (2 * h) + h
        if N_TOP <= lvl < N_LEVELS - 1:
            blocks.append(np.where(t >= mid, (u >= mid) & (u <= t), (u > t) & (u < mid)))
        mid_s = (u // (2 * h)) * (2 * h) + h
        masks.append((t // (2 * h) == u // (2 * h)) & (t >= mid) & (u < mid_s))
    masks.append(t == u)
    sm = np.concatenate(blocks, axis=0).astype(np.float32)
    mk = np.stack(masks, axis=0).astype(np.float32)
    return jnp.asarray(sm, BF16), jnp.asarray(mk, F32)


def _midpoint_rows(G, h):
    rows = [jnp.broadcast_to(G[b + h - 1:b + h, :], (2 * h, HEAD)) for b in range(0, CHUNK, 2 * h)]
    return rows[0] if len(rows) == 1 else jnp.concatenate(rows, axis=0)


def _hgrn_kernel(zq_ref, zf_ref, zi_ref, zg_ref, lb_ref, go_ref, s0_ref, sm_ref, mk_ref,
                 o_ref, sout_ref, st_ref, *, n_chunks, n_heads):
    C = CHUNK
    tb = pl.program_id(2)

    @pl.when(tb == 0)
    def _():
        for hh in range(n_heads):
            st_ref[hh] = s0_ref[hh].T

    odd_row = lax.broadcasted_iota(jnp.int32, (C, HEAD), 0) % 2 == 1

    def chunk(c, carry):
        r0 = pl.multiple_of(c * C, C)
        heads = range(n_heads)
        cols = [slice(hh * HEAD, (hh + 1) * HEAD) for hh in heads]
        q, k, v, f, d2 = [], [], [], [], []
        for hh in heads:
            lb = lb_ref[:, cols[hh]]
            zq = zq_ref[pl.ds(r0, C), cols[hh]]
            zf = zf_ref[pl.ds(r0, C), cols[hh]]
            q.append(zq * _sigmoid(zq))
            f.append(lb + (1.0 - lb) * _sigmoid(zf))
            g = jnp.log(f[hh])
            k.append(1.0 - f[hh])
            v.append(zi_ref[pl.ds(r0, C), cols[hh]])
            g_hi = g.astype(BF16)
            g_lo = (g - g_hi.astype(F32)).astype(BF16)
            d2.append(jnp.dot(sm_ref[...], jnp.concatenate([g_hi, g_lo], axis=1),
                              preferred_element_type=F32))
        e, lv = [], []
        for hh in heads:
            d = d2[hh][:, :HEAD] + d2[hh][:, HEAD:]
            e.append(jnp.exp(d))
            G = d[0:C]
            top = [jnp.exp(-jnp.abs(G - _midpoint_rows(G, C >> (lvl + 1)))) for lvl in range(N_TOP)]
            lv.append(top + [e[hh][2 * C:3 * C], e[hh][3 * C:4 * C],
                             jnp.where(odd_row, f[hh], 1.0)])
        a = []
        for hh in heads:
            parts = [lax.dot_general(q[hh].astype(BF16), k[hh].astype(BF16), _NT,
                                     preferred_element_type=F32)]
            for lvl in range(N_LEVELS):
                el = lv[hh][lvl]
                parts.append(lax.dot_general((q[hh] * el).astype(BF16), (k[hh] * el).astype(BF16),
                                             _NT, preferred_element_type=F32))
            acc = mk_ref[N_LEVELS] * parts[0]
            for lvl in range(N_LEVELS):
                acc = acc + mk_ref[lvl] * parts[1 + lvl]
            a.append(acc)
        o = []
        for hh in heads:
            st = st_ref[hh]
            qg = (q[hh] * e[hh][0:C]).astype(BF16)
            kg = (k[hh] * e[hh][C:2 * C]).astype(BF16)
            vb = v[hh].astype(BF16)
            oh = jnp.dot(a[hh].astype(BF16), vb, preferred_element_type=F32)
            o.append(oh + lax.dot_general(qg, st.astype(BF16), _NT, preferred_element_type=F32))
            st_ref[hh] = st * e[hh][C - 1:C, :] + jnp.dot(v[hh].T.astype(BF16), kg,
                                                          preferred_element_type=F32)
        for hh in heads:
            zg = zg_ref[pl.ds(r0, C), cols[hh]]
            ms = jnp.mean(o[hh] * o[hh], axis=-1, keepdims=True)
            og = o[hh] * lax.rsqrt(ms + EPS) * go_ref[:, cols[hh]] * (zg * _sigmoid(zg))
            o_ref[pl.ds(r0, C), cols[hh]] = og.astype(o_ref.dtype)
        return carry

    lax.fori_loop(0, n_chunks, chunk, 0)

    @pl.when(tb == pl.num_programs(2) - 1)
    def _():
        for hh in range(n_heads):
            sout_ref[hh] = st_ref[hh].T


def _hgrn_mix(z, lb, g_out, s0, B, T):
    D = z.shape[1] // 4
    H = D // HEAD
    hb = _pick(H, (HGRN_HEADS,))
    tb = _pick(T, (256, 128, 64))
    nt = T // tb
    nh = H // hb
    sm, mk = _hgrn_constants()

    def zspec(part):
        return pl.BlockSpec((tb, hb * HEAD), lambda b, h, t: (b * nt + t, part * nh + h))

    head_vec = pl.BlockSpec((1, hb * HEAD), lambda b, h, t: (0, h))
    state = pl.BlockSpec((None, hb, HEAD, HEAD), lambda b, h, t: (b, h, 0, 0))
    og, s_fin = pl.pallas_call(
        functools.partial(_hgrn_kernel, n_chunks=tb // CHUNK, n_heads=hb),
        grid=(B, nh, nt),
        in_specs=[zspec(0), zspec(1), zspec(2), zspec(3), head_vec, head_vec, state,
                  pl.BlockSpec(sm.shape, lambda b, h, t: (0, 0)),
                  pl.BlockSpec(mk.shape, lambda b, h, t: (0, 0, 0))],
        out_specs=[pl.BlockSpec((tb, hb * HEAD), lambda b, h, t: (b * nt + t, h)), state],
        out_shape=[jax.ShapeDtypeStruct((B * T, D), BF16),
                   jax.ShapeDtypeStruct((B, H, HEAD, HEAD), F32)],
        scratch_shapes=[pltpu.VMEM((hb, HEAD, HEAD), F32)],
        compiler_params=_params(("parallel", "parallel", "arbitrary")),
        name="hgrn_mix",
    )(z, z, z, z, lb.reshape(1, D), g_out.reshape(1, D).astype(F32), s0, sm, mk)
    return og, s_fin


def _out_ple_kernel(og_ref, x_ref, p_ref, wo_ref, gpg_ref, wpg_ref, wpin_ref, gfin_ref, o_ref,
                    *, final, col_chunk):
    D = x_ref.shape[1]
    y = x_ref[...] + jnp.dot(og_ref[...], wo_ref[...], preferred_element_type=F32)
    ms = jnp.mean(y * y, axis=-1, keepdims=True)
    yn = (y * lax.rsqrt(ms + EPS) * gpg_ref[...]).astype(BF16)
    pb = p_ref[...].astype(BF16)
    parts = []
    for c0 in range(0, D, col_chunk):
        gate = _sigmoid(jnp.dot(yn, wpg_ref[:, c0:c0 + col_chunk], preferred_element_type=F32))
        emb = jnp.dot(pb, wpin_ref[:, c0:c0 + col_chunk], preferred_element_type=F32)
        parts.append(y[:, c0:c0 + col_chunk] + gate * emb)
    if final:
        ms2 = sum(jnp.sum(t * t, axis=-1, keepdims=True) for t in parts) * (1.0 / D)
        inv = lax.rsqrt(ms2 + EPS)
        for n, c0 in enumerate(range(0, D, col_chunk)):
            o_ref[:, c0:c0 + col_chunk] = parts[n] * inv * gfin_ref[:, c0:c0 + col_chunk]
    else:
        for n, c0 in enumerate(range(0, D, col_chunk)):
            o_ref[:, c0:c0 + col_chunk] = parts[n]


def _out_ple(og, x, p, layer, w_out, g_pg, w_pg, w_pin, g_final, final):
    T, D = x.shape
    P = p.shape[2]
    tm = _pick(T, (256, 128, 64))
    col_chunk = min(D, 512)
    rows = lambda i: (i, 0)
    fixed = lambda i: (0, 0)
    once = pl.Buffered(1)
    return pl.pallas_call(
        functools.partial(_out_ple_kernel, final=final, col_chunk=col_chunk),
        grid=(T // tm,),
        in_specs=[pl.BlockSpec((tm, D), rows), pl.BlockSpec((tm, D), rows),
                  pl.BlockSpec((None, tm, P), lambda i: (layer, i, 0)),
                  pl.BlockSpec((D, D), fixed, pipeline_mode=once),
                  pl.BlockSpec((1, D), fixed),
                  pl.BlockSpec((D, D), fixed, pipeline_mode=once),
                  pl.BlockSpec((P, D), fixed, pipeline_mode=once),
                  pl.BlockSpec((1, D), fixed)],
        out_specs=pl.BlockSpec((tm, D), rows),
        out_shape=jax.ShapeDtypeStruct((T, D), F32),
        compiler_params=_params(("parallel",)),
        name="out_ple",
    )(og, x, p, w_out, g_pg.reshape(1, D).astype(F32), w_pg, w_pin,
      g_final.reshape(1, D).astype(F32))


def _cumsum_kernel(x_ref, u_ref, o_ref, carry_ref, *, n_sub):
    @pl.when(pl.program_id(0) == 0)
    def _():
        carry_ref[...] = jnp.zeros_like(carry_ref)

    u = u_ref[...]
    c = carry_ref[:, 0:1]
    for s in range(n_sub):
        x = x_ref[:, s * LANES:(s + 1) * LANES]
        hi = x.astype(BF16)
        r1 = x - hi.astype(F32)
        mid = r1.astype(BF16)
        lo = (r1 - mid.astype(F32)).astype(BF16)
        cs = (jnp.dot(hi, u, preferred_element_type=F32)
              + jnp.dot(mid, u, preferred_element_type=F32)
              + jnp.dot(lo, u, preferred_element_type=F32)) + c
        o_ref[:, s * LANES:(s + 1) * LANES] = cs
        c = cs[:, LANES - 1:LANES]
    carry_ref[...] = jnp.broadcast_to(c, carry_ref.shape)


def _cumsum_last(x):
    R, L = x.shape
    cb = _pick(L, (2048, 1024, 512, 256, 128))
    u = jnp.asarray(np.triu(np.ones((LANES, LANES), np.float32)), BF16)
    return pl.pallas_call(
        functools.partial(_cumsum_kernel, n_sub=cb // LANES),
        grid=(L // cb,),
        in_specs=[pl.BlockSpec((R, cb), lambda j: (0, j)),
                  pl.BlockSpec((LANES, LANES), lambda j: (0, 0))],
        out_specs=pl.BlockSpec((R, cb), lambda j: (0, j)),
        out_shape=jax.ShapeDtypeStruct((R, L), F32),
        scratch_shapes=[pltpu.VMEM((R, LANES), F32)],
        compiler_params=_params(("arbitrary",)),
        name="cumsum",
    )(x, u)


def _fox_kernel(q_ref, gate_ref, k_ref, v_ref, dq_ref, dk_ref, o_ref, *, tq, tk, q_off, scale):
    i = pl.program_id(2)
    q = q_ref[...].astype(BF16)
    dq = dq_ref[...]
    first_pos = q_off + i * tq
    n_full = (first_pos + 1) // tk
    n_all = (first_pos + tq - 1) // tk + 1

    def step(j, carry, masked):
        m, l, acc = carry
        start = pl.multiple_of(j * tk, tk)
        k = k_ref[pl.ds(start, tk), :]
        v = v_ref[pl.ds(start, tk), :]
        s = lax.dot_general(q, k, _NT, preferred_element_type=F32) * scale
        s = s + dq - dk_ref[:, pl.ds(start, tk)]
        if masked:
            kpos = start + lax.broadcasted_iota(jnp.int32, (tq, tk), 1)
            qpos = first_pos + lax.broadcasted_iota(jnp.int32, (tq, tk), 0)
            s = jnp.where(kpos <= qpos, s, MASK_VALUE)
        m_new = jnp.maximum(m, jnp.max(s, axis=-1, keepdims=True))
        alpha = jnp.exp(m - m_new)
        p = jnp.exp(s - m_new)
        l = alpha * l + jnp.sum(p, axis=-1, keepdims=True)
        acc = alpha * acc + jnp.dot(p.astype(BF16), v, preferred_element_type=F32)
        return m_new, l, acc

    init = (jnp.full((tq, 1), MASK_VALUE, F32), jnp.zeros((tq, 1), F32),
            jnp.zeros((tq, HEAD), F32))
    carry = lax.fori_loop(0, n_full, functools.partial(step, masked=False), init)
    _, l, acc = lax.fori_loop(n_full, n_all, functools.partial(step, masked=True), carry)
    gate = gate_ref[...]
    o = acc * (1.0 / l)
    o_ref[...] = (o * (gate * _sigmoid(gate))).astype(o_ref.dtype)


def _fox_short_kernel(q_ref, gate_ref, k_ref, v_ref, dq_ref, dk_ref, o_ref, *, n_heads, q_off,
                      scale):
    T, Lk = q_ref.shape[0], k_ref.shape[0]
    heads = range(n_heads)
    cols = [slice(hh * HEAD, (hh + 1) * HEAD) for hh in heads]
    visible = (lax.broadcasted_iota(jnp.int32, (T, Lk), 1)
               <= q_off + lax.broadcasted_iota(jnp.int32, (T, Lk), 0))
    s = []
    for hh in heads:
        sh = lax.dot_general(q_ref[:, cols[hh]].astype(BF16), k_ref[:, cols[hh]], _NT,
                             preferred_element_type=F32) * scale
        s.append(jnp.where(visible, sh + dq_ref[hh] - dk_ref[hh], MASK_VALUE))
    p = [jnp.exp(sh - jnp.max(sh, axis=-1, keepdims=True)) for sh in s]
    o = [jnp.dot(p[hh].astype(BF16), v_ref[:, cols[hh]], preferred_element_type=F32)
         for hh in heads]
    for hh in heads:
        gate = gate_ref[:, cols[hh]]
        oh = o[hh] * (1.0 / jnp.sum(p[hh], axis=-1, keepdims=True))
        o_ref[:, cols[hh]] = (oh * (gate * _sigmoid(gate))).astype(o_ref.dtype)


def _fox_mix_short(z, k_all, v_all, dq, dk, B, T, q_off):
    D = z.shape[1] // 2
    H = D // HEAD
    Lk = k_all.shape[1]
    hb = _pick(H, (FOX_SHORT_HEADS,))
    nh = H // hb
    kv = pl.BlockSpec((None, Lk, hb * HEAD), lambda b, h: (b, 0, h))
    return pl.pallas_call(
        functools.partial(_fox_short_kernel, n_heads=hb, q_off=q_off, scale=HEAD ** -0.5),
        grid=(B, nh),
        in_specs=[pl.BlockSpec((T, hb * HEAD), lambda b, h: (b, h)),
                  pl.BlockSpec((T, hb * HEAD), lambda b, h: (b, nh + h)),
                  kv, kv,
                  pl.BlockSpec((None, hb, T, 1), lambda b, h: (b, h, 0, 0)),
                  pl.BlockSpec((None, hb, 1, Lk), lambda b, h: (b, h, 0, 0))],
        out_specs=pl.BlockSpec((T, hb * HEAD), lambda b, h: (b, h)),
        out_shape=jax.ShapeDtypeStruct((B * T, D), BF16),
        compiler_params=_params(("parallel", "parallel")),
        name="fox_mix_short",
    )(z, z, k_all, v_all, dq, dk)


def _fox_mix(z, k_all, v_all, dq, dk, B, T, q_off):
    D = z.shape[1] // 2
    H = D // HEAD
    Lk = k_all.shape[1]
    if T <= FOX_SHORT_QUERIES and Lk <= FOX_SHORT_KEYS:
        return _fox_mix_short(z, k_all, v_all, dq, dk, B, T, q_off)
    tq = _pick(T, (512, 256, 128, 64))
    tk = Lk if Lk <= 2048 else _pick(Lk, (512, 256, 128))
    nq = T // tq
    kv = pl.BlockSpec((None, Lk, HEAD), lambda b, h, i: (b, 0, h))
    return pl.pallas_call(
        functools.partial(_fox_kernel, tq=tq, tk=tk, q_off=q_off, scale=HEAD ** -0.5),
        grid=(B, H, nq),
        in_specs=[pl.BlockSpec((tq, HEAD), lambda b, h, i: (b * nq + i, h)),
                  pl.BlockSpec((tq, HEAD), lambda b, h, i: (b * nq + i, H + h)),
                  kv, kv,
                  pl.BlockSpec((None, None, tq, 1), lambda b, h, i: (b, h, i, 0)),
                  pl.BlockSpec((None, None, 1, Lk), lambda b, h, i: (b, h, 0, 0))],
        out_specs=pl.BlockSpec((tq, HEAD), lambda b, h, i: (b * nq + i, h)),
        out_shape=jax.ShapeDtypeStruct((B * T, D), BF16),
        compiler_params=_params(("parallel", "parallel", "arbitrary")),
        name="fox_mix",
    )(z, z, k_all, v_all, dq, dk)


FOX_BOUND_MARGIN = 1.02
FOX_BOUND_LIMIT = 32.0
FOX_FAST_TILE = 1024
AUG = 6


def _top16(x):
    bits = lax.bitcast_convert_type(x, jnp.uint32) & jnp.uint32(0xFFFF0000)
    return lax.bitcast_convert_type(bits, F32)


def _split3(x):
    hi = _top16(x)
    r1 = x - hi
    mid = _top16(r1)
    lo = r1 - mid
    return hi.astype(BF16), mid.astype(BF16), lo.astype(BF16)


def _bias_terms(x, shape, axis, value_first):
    idx = lax.broadcasted_iota(jnp.int32, shape, axis)
    v0, o0 = (0, AUG // 2) if value_first else (AUG // 2, 0)
    hi, mid, lo = _split3(x)
    ones = (idx >= o0) & (idx < o0 + AUG // 2)
    return jnp.where(idx == v0, hi.astype(F32), jnp.where(idx == v0 + 1, mid.astype(F32),
           jnp.where(idx == v0 + 2, lo.astype(F32), jnp.where(ones, 1.0, 0.0))))


def _sq_sums_kernel(x_ref, e_ref, o_ref):
    x = x_ref[...]
    o_ref[...] = lax.dot_general(e_ref[...], (x * x).astype(BF16), _NT,
                                 preferred_element_type=F32)


def _head_sq_sums(x, D):
    T = x.shape[0]
    H = D // HEAD
    tm = _pick(T, (512, 256, 128, 64))
    e = np.zeros((LANES, D), np.float32)
    e[np.arange(D) // HEAD, np.arange(D)] = 1.0
    return pl.pallas_call(
        _sq_sums_kernel,
        grid=(T // tm,),
        in_specs=[pl.BlockSpec((tm, D), lambda i: (i, 0)),
                  pl.BlockSpec((LANES, D), lambda i: (0, 0))],
        out_specs=pl.BlockSpec((LANES, tm), lambda i: (0, i)),
        out_shape=jax.ShapeDtypeStruct((LANES, T), F32),
        compiler_params=_params(("parallel",)),
        name="head_sq_sums",
    )(x, jnp.asarray(e, BF16))[:H]


def _fox_fast_kernel(q_ref, c_ref, gate_ref, ka_ref, vt_ref, o_ref, qa_ref, *, tile, scale):
    i = pl.program_id(2)
    qa_ref[:, :HEAD] = (q_ref[...] * scale).astype(BF16)
    qa_ref[:, HEAD:] = _bias_terms(c_ref[...], (HEAD, tile), 0, True).T.astype(BF16)

    def scores(j, masked):
        start = pl.multiple_of(j * tile, tile)
        s = lax.dot_general(ka_ref[pl.ds(start, tile), :], qa_ref[...], _NT,
                            preferred_element_type=F32)
        if masked:
            kpos = lax.broadcasted_iota(jnp.int32, (tile, tile), 0)
            qpos = lax.broadcasted_iota(jnp.int32, (tile, tile), 1)
            s = jnp.where(kpos <= qpos, s, MASK_VALUE)
        return s

    def consume(s, j, lp, acc):
        start = pl.multiple_of(j * tile, tile)
        p = jnp.exp(s)
        for g in range(tile // SUBLANES):
            lp = lp + p[g * SUBLANES:(g + 1) * SUBLANES, :]
        acc = acc + jnp.dot(vt_ref[:, pl.ds(start, tile)], p.astype(BF16),
                            preferred_element_type=F32)
        return lp, acc

    def pair(j0, carry, masked_b):
        lp, acc = carry
        sa = scores(j0, False)
        sb = scores(j0 + 1, masked_b)
        lp, acc = consume(sa, j0, lp, acc)
        return consume(sb, j0 + 1, lp, acc)

    init = (jnp.zeros((SUBLANES, tile), F32), jnp.zeros((HEAD, tile), F32))
    carry = lax.fori_loop(0, i // 2, lambda t, c: pair(2 * t, c, False), init)
    lp, acc = lax.cond(i % 2 == 1, lambda c: pair(i - 1, c, True),
                       lambda c: consume(scores(i, True), i, *c), carry)
    gate = gate_ref[...]
    o = (acc * (1.0 / jnp.sum(lp, axis=0, keepdims=True))).T
    o_ref[...] = (o * (gate * _sigmoid(gate))).astype(o_ref.dtype)


def _fox_mix_fast(z, c, k_aug, v_t, B, T):
    D = z.shape[1] // 2
    H = D // HEAD
    tile = FOX_FAST_TILE
    nq = T // tile
    return pl.pallas_call(
        functools.partial(_fox_fast_kernel, tile=tile, scale=HEAD ** -0.5),
        grid=(B, H, nq),
        in_specs=[pl.BlockSpec((tile, HEAD), lambda b, h, i: (b * nq + i, h)),
                  pl.BlockSpec((None, None, 1, tile), lambda b, h, i: (b, h, 0, i)),
                  pl.BlockSpec((tile, HEAD), lambda b, h, i: (b * nq + i, H + h)),
                  pl.BlockSpec((None, T, 2 * HEAD), lambda b, h, i: (b, 0, h)),
                  pl.BlockSpec((None, None, HEAD, T), lambda b, h, i: (b, h, 0, 0))],
        out_specs=pl.BlockSpec((tile, HEAD), lambda b, h, i: (b * nq + i, h)),
        out_shape=jax.ShapeDtypeStruct((B * T, D), BF16),
        scratch_shapes=[pltpu.VMEM((tile, 2 * HEAD), BF16)],
        compiler_params=_params(("parallel", "parallel", "arbitrary")),
        name="fox_mix_fast",
    )(z, c, z, k_aug, v_t)


def _augment_kernel(k_ref, dk_ref, o_ref, *, n_heads):
    tm = k_ref.shape[0]
    for h in range(n_heads):
        o_ref[:, 2 * h * HEAD:(2 * h + 1) * HEAD] = k_ref[:, h * HEAD:(h + 1) * HEAD].astype(BF16)
        o_ref[:, (2 * h + 1) * HEAD:(2 * h + 2) * HEAD] = _bias_terms(
            -dk_ref[:, h:h + 1], (tm, HEAD), 1, False).astype(BF16)


def _augment_keys(k, dcum, B, Lk, H):
    D = H * HEAD
    T = B * Lk
    tm = _pick(T, (512, 256, 128, 64))
    dk_rows = jnp.transpose(dcum, (0, 2, 1)).reshape(T, H)
    out = pl.pallas_call(
        functools.partial(_augment_kernel, n_heads=H),
        grid=(T // tm,),
        in_specs=[pl.BlockSpec((tm, D), lambda i: (i, 0)),
                  pl.BlockSpec((tm, H), lambda i: (i, 0))],
        out_specs=pl.BlockSpec((tm, 2 * D), lambda i: (i, 0)),
        out_shape=jax.ShapeDtypeStruct((T, 2 * D), BF16),
        compiler_params=_params(("parallel",)),
        name="augment_keys",
    )(k, dk_rows)
    return out.reshape(B, Lk, 2 * D)


def _trunk(x, p, s0, past, W):
    B, T, D = x.shape
    H = D // HEAD
    depth = p.shape[0]
    n_a = W["w_in_a"].shape[0]
    xf = x.reshape(B * T, D)
    pf = p.reshape(depth, B * T, p.shape[-1])
    states = []
    for layer in range(depth):
        final = layer == depth - 1
        if layer < n_a:
            z = _norm_matmul(xf, W["g_norm_a"][layer], W["w_in_a"][layer])
            og, s_fin = _hgrn_mix(z, W["lbs"][layer], W["g_out_a"][layer], s0[layer], B, T)
            states.append(s_fin)
            w_out = W["w_out_a"][layer]
        else:
            j = layer - n_a
            z = _norm_matmul(xf, W["g_norm_b"][j], W["w_in_b"][j])
            if k_aug is None:
                og = _fox_mix(z, k_all, v_all, dq, dk, B, T, q_off)
            else:
                qn = jnp.transpose(jnp.sqrt(_head_sq_sums(z, D)).reshape(H, B, T), (1, 0, 2))
                bound = (FOX_BOUND_MARGIN * HEAD ** -0.5) * qn * kmax
                c = (dcum - bound)[:, :, None, :]
                og = lax.cond(jnp.max(bound) <= FOX_BOUND_LIMIT,
                              lambda: _fox_mix_fast(z, c, k_aug, v_t, B, T),
                              lambda: _fox_mix(z, k_new.reshape(B, T, D).astype(BF16), v_all, dq,
                                               dk, B, T, q_off))
            w_out = W["w_out_b"][j]
        xf = _out_ple(og, xf, pf, layer, w_out, W["g_ple"][layer], W["w_ple_gate"][layer],
                      W["w_ple_in"][layer], W["g_final"], final)
        if layer == n_a - 1:
            fast = past is None and T % FOX_FAST_TILE == 0
            if fast:
                k_new, k_bf = _norm_matmul(xf, W["g_kv"], W["w_k"]), None
            else:
                k_new, k_bf = _norm_matmul(xf, W["g_kv"], W["w_k"], with_bf16=True)
                k_bf = k_bf.reshape(B, T, D)
            v_new, v_bf = _norm_matmul(xf, W["g_kv"], W["w_v"], with_bf16=True)
            logf_new = _norm_matmul(xf, W["g_kv"], W["w_f"], bias=W["b_f"])[:, :H]
            logf_new = logf_new.reshape(B, T, H)
            v_bf = v_bf.reshape(B, T, D)
            if past is None:
                k_all, v_all, logf_all, q_off = k_bf, v_bf, logf_new, 0
            else:
                past_k, past_v, past_logf = past
                q_off = past_k.shape[1]
                k_all = jnp.concatenate([past_k.reshape(B, q_off, D).astype(BF16), k_bf], axis=1)
                v_all = jnp.concatenate([past_v.reshape(B, q_off, D).astype(BF16), v_bf], axis=1)
                logf_all = jnp.concatenate([past_logf.astype(F32), logf_new], axis=1)
            Lk = v_all.shape[1]
            pad = (-Lk) % LANES
            if pad:
                k_all = jnp.pad(k_all, ((0, 0), (0, pad), (0, 0)))
                v_all = jnp.pad(v_all, ((0, 0), (0, pad), (0, 0)))
                logf_all = jnp.pad(logf_all, ((0, 0), (0, pad), (0, 0)))
            rows = jnp.transpose(logf_all, (0, 2, 1)).reshape(B * H, Lk + pad)
            dcum = _cumsum_last(rows).reshape(B, H, Lk + pad)
            dk = dcum[:, :, None, :]
            dq = dcum[:, :, q_off:q_off + T, None]
            k_aug = None
            if fast:
                kn = jnp.sqrt(_head_sq_sums(k_new, D)).reshape(H, B, T)
                kmax = jnp.transpose(jnp.max(kn, axis=2))[:, :, None]
                k_aug = _augment_keys(k_new, dcum, B, T, H)
                v_t = jnp.transpose(v_bf.reshape(B, T, H, HEAD), (0, 2, 3, 1))
    return (xf.reshape(B, T, D), jnp.stack(states), k_new.reshape(B, T, H, HEAD),
            v_new.reshape(B, T, H, HEAD), logf_new)


def kernel(x_prompt, x_sample, state_hgrn, cache_k, cache_v, cache_logf, p_prompt, p_sample,
           g_norm_a, w_in_a, lb_logits, g_out_a, w_out_a, g_kv, w_kv, b_f,
           g_norm_b, w_in_b, w_out_b, w_ple_in, g_ple, w_ple_gate, g_final):
    D = x_prompt.shape[-1]
    H = D // HEAD
    n_a = w_in_a.shape[0]
    lbs = jnp.cumsum(jax.nn.softmax(lb_logits.astype(F32), axis=0), axis=0)
    lbs = lbs - lbs[:1]
    w_f = jnp.pad(w_kv[:, 2 * D:], ((0, 0), (0, LANES - H)))
    W = dict(
        g_norm_a=g_norm_a, w_in_a=w_in_a.astype(BF16), lbs=lbs, g_out_a=g_out_a,
        w_out_a=w_out_a.astype(BF16), g_kv=g_kv, w_k=w_kv[:, :D].astype(BF16),
        w_v=w_kv[:, D:2 * D].astype(BF16), w_f=w_f.astype(BF16),
        b_f=jnp.pad(b_f, (0, LANES - H)), g_norm_b=g_norm_b, w_in_b=w_in_b.astype(BF16),
        w_out_b=w_out_b.astype(BF16), w_ple_in=w_ple_in.astype(BF16), g_ple=g_ple,
        w_ple_gate=w_ple_gate.astype(BF16), g_final=g_final)
    s0_prompt = jnp.zeros((n_a, x_prompt.shape[0], H, HEAD, HEAD), F32)
    y_p, st_p, k_p, v_p, f_p = _trunk(x_prompt, p_prompt, s0_prompt, None, W)
    y_s, st_s, k_s, v_s, f_s = _trunk(x_sample, p_sample, state_hgrn,
                                      (cache_k, cache_v, cache_logf), W)
    return (y_p, y_s, st_p, st_s, k_p, v_p, f_p, k_s, v_s, f_s)
```

```python
import functools

import jax
import jax.numpy as jnp
import numpy as np
from jax import lax
from jax.experimental import pallas as pl
from jax.experimental.pallas import tpu as pltpu

F32 = jnp.float32
BF16 = jnp.bfloat16

EPS = 1e-6
HEAD = 128
LANES = 128
SUBLANES = 8
CHUNK = 64
N_LEVELS = 6
N_TOP = 3
HGRN_HEADS = 16
FOX_SHORT_HEADS = 4
FOX_SHORT_QUERIES = 128
FOX_SHORT_KEYS = 2048
MASK_VALUE = -1e30
VMEM_LIMIT = 56 * 1024 * 1024

_NT = (((1,), (1,)), ((), ()))


def _params(sem, vmem=VMEM_LIMIT):
    return pltpu.CompilerParams(dimension_semantics=sem, vmem_limit_bytes=vmem)


def _sigmoid(x):
    return 1.0 / (1.0 + jnp.exp(-x))


def _pick(n, prefs):
    for p in prefs:
        if n % p == 0:
            return p
    return n


def _nmm_kernel(*refs, log_sigmoid, with_bf16, row_chunk):
    if log_sigmoid:
        x_ref, g_ref, w_ref, b_ref = refs[:4]
        rest = refs[4:]
    else:
        x_ref, g_ref, w_ref = refs[:3]
        b_ref = None
        rest = refs[3:]
    if with_bf16:
        o_ref, ob_ref, xn_ref = rest
    else:
        o_ref, xn_ref = rest
        ob_ref = None
    tm = x_ref.shape[0]

    @pl.when(pl.program_id(1) == 0)
    def _():
        for r in range(0, tm, row_chunk):
            x = x_ref[r:r + row_chunk, :]
            ms = jnp.mean(x * x, axis=-1, keepdims=True)
            xn_ref[r:r + row_chunk, :] = (x * lax.rsqrt(ms + EPS) * g_ref[...]).astype(BF16)

    z = jnp.dot(xn_ref[...], w_ref[...], preferred_element_type=F32)
    if log_sigmoid:
        z = z + b_ref[...]
        z = jnp.minimum(z, 0.0) - jnp.log(1.0 + jnp.exp(-jnp.abs(z)))
    o_ref[...] = z
    if with_bf16:
        ob_ref[...] = z.astype(BF16)


def _norm_matmul(x, g, w, bias=None, with_bf16=False):
    T, D = x.shape
    N = w.shape[1]
    tm = _pick(T, (1024, 512, 256, 128, 64))
    tn = _pick(N, (1024, 512, 256, 128))
    row_chunk = min(tm, 256)
    in_specs = [
        pl.BlockSpec((tm, D), lambda i, j: (i, 0)),
        pl.BlockSpec((1, D), lambda i, j: (0, 0)),
        pl.BlockSpec((D, tn), lambda i, j: (0, j)),
    ]
    args = [x, g.reshape(1, D).astype(F32), w]
    if bias is not None:
        in_specs.append(pl.BlockSpec((1, tn), lambda i, j: (0, j)))
        args.append(bias.reshape(1, N).astype(F32))
    out_shape = [jax.ShapeDtypeStruct((T, N), F32)]
    out_specs = [pl.BlockSpec((tm, tn), lambda i, j: (i, j))]
    if with_bf16:
        out_shape.append(jax.ShapeDtypeStruct((T, N), BF16))
        out_specs.append(pl.BlockSpec((tm, tn), lambda i, j: (i, j)))
    out = pl.pallas_call(
        functools.partial(_nmm_kernel, log_sigmoid=bias is not None, with_bf16=with_bf16,
                          row_chunk=row_chunk),
        grid=(T // tm, N // tn),
        in_specs=in_specs,
        out_specs=out_specs,
        out_shape=out_shape,
        scratch_shapes=[pltpu.VMEM((tm, D), BF16)],
        compiler_params=_params(("parallel", "arbitrary")),
        name="norm_matmul",
    )(*args)
    return out if with_bf16 else out[0]


def _nmm_kv_kernel(x_ref, g_ref, w_ref, k_ref, v_ref, vb_ref, xn_ref, *, half, row_chunk):
    tm = x_ref.shape[0]
    j = pl.program_id(1)

    @pl.when(j == 0)
    def _():
        for r in range(0, tm, row_chunk):
            x = x_ref[r:r + row_chunk, :]
            ms = jnp.mean(x * x, axis=-1, keepdims=True)
            xn_ref[r:r + row_chunk, :] = (x * lax.rsqrt(ms + EPS) * g_ref[...]).astype(BF16)

    z = jnp.dot(xn_ref[...], w_ref[...], preferred_element_type=F32)

    @pl.when(j < half)
    def _():
        k_ref[...] = z

    @pl.when(j >= half)
    def _():
        v_ref[...] = z
        vb_ref[...] = z.astype(BF16)


def _norm_matmul_kv(x, g, w_kv):
    T, D = x.shape
    N = w_kv.shape[1] // 2
    tm = _pick(T, (1024, 512, 256, 128, 64))
    tn = _pick(N, (1024, 512, 256, 128))
    half = N // tn
    k_cols = lambda i, j: (i, jnp.minimum(j, half - 1))
    v_cols = lambda i, j: (i, jnp.maximum(j - half, 0))
    return pl.pallas_call(
        functools.partial(_nmm_kv_kernel, half=half, row_chunk=min(tm, 256)),
        grid=(T // tm, 2 * half),
        in_specs=[pl.BlockSpec((tm, D), lambda i, j: (i, 0)),
                  pl.BlockSpec((1, D), lambda i, j: (0, 0)),
                  pl.BlockSpec((D, tn), lambda i, j: (0, j))],
        out_specs=[pl.BlockSpec((tm, tn), k_cols), pl.BlockSpec((tm, tn), v_cols),
                   pl.BlockSpec((tm, tn), v_cols)],
        out_shape=[jax.ShapeDtypeStruct((T, N), F32), jax.ShapeDtypeStruct((T, N), F32),
                   jax.ShapeDtypeStruct((T, N), BF16)],
        scratch_shapes=[pltpu.VMEM((tm, D), BF16)],
        compiler_params=_params(("parallel", "arbitrary")),
        name="norm_matmul_kv",
    )(x, g.reshape(1, D).astype(F32), w_kv)


def _hgrn_constants():
    C = CHUNK
    t = np.arange(C)[:, None]
    u = np.arange(C)[None, :]
    blocks = [(u <= t), (u > t)]
    masks = []
    for lvl in range(N_LEVELS):
        h = C >> (lvl + 1)
        mid = (t // (2 * h)) * (2 * h) + h
        if N_TOP <= lvl < N_LEVELS - 1:
            blocks.append(np.where(t >= mid, (u >= mid) & (u <= t), (u > t) & (u < mid)))
        mid_s = (u // (2 * h)) * (2 * h) + h
        masks.append((t // (2 * h) == u // (2 * h)) & (t >= mid) & (u < mid_s))
    masks.append(t == u)
    sm = np.concatenate(blocks, axis=0).astype(np.float32)
    mk = np.stack(masks, axis=0).astype(np.float32)
    return jnp.asarray(sm, BF16), jnp.asarray(mk, F32)


def _midpoint_rows(G, h):
    rows = [jnp.broadcast_to(G[b + h - 1:b + h, :], (2 * h, HEAD)) for b in range(0, CHUNK, 2 * h)]
    return rows[0] if len(rows) == 1 else jnp.concatenate(rows, axis=0)


def _hgrn_kernel(zq_ref, zf_ref, zi_ref, zg_ref, lb_ref, go_ref, s0_ref, sm_ref, mk_ref,
                 o_ref, sout_ref, st_ref, *, n_chunks, n_heads):
    C = CHUNK
    tb = pl.program_id(2)

    @pl.when(tb == 0)
    def _():
        for hh in range(n_heads):
            st_ref[hh] = s0_ref[hh].T

    odd_row = lax.broadcasted_iota(jnp.int32, (C, HEAD), 0) % 2 == 1

    def chunk(c, carry):
        r0 = pl.multiple_of(c * C, C)
        heads = range(n_heads)
        cols = [slice(hh * HEAD, (hh + 1) * HEAD) for hh in heads]
        q, k, v, f, d2 = [], [], [], [], []
        for hh in heads:
            lb = lb_ref[:, cols[hh]]
            zq = zq_ref[pl.ds(r0, C), cols[hh]]
            zf = zf_ref[pl.ds(r0, C), cols[hh]]
            q.append(zq * _sigmoid(zq))
            f.append(lb + (1.0 - lb) * _sigmoid(zf))
            g = jnp.log(f[hh])
            k.append(1.0 - f[hh])
            v.append(zi_ref[pl.ds(r0, C), cols[hh]])
            g_hi = g.astype(BF16)
            g_lo = (g - g_hi.astype(F32)).astype(BF16)
            d2.append(jnp.dot(sm_ref[...], jnp.concatenate([g_hi, g_lo], axis=1),
                              preferred_element_type=F32))
        e, lv = [], []
        for hh in heads:
            d = d2[hh][:, :HEAD] + d2[hh][:, HEAD:]
            e.append(jnp.exp(d))
            G = d[0:C]
            top = [jnp.exp(-jnp.abs(G - _midpoint_rows(G, C >> (lvl + 1)))) for lvl in range(N_TOP)]
            lv.append(top + [e[hh][2 * C:3 * C], e[hh][3 * C:4 * C],
                             jnp.where(odd_row, f[hh], 1.0)])
        a = []
        for hh in heads:
            parts = [lax.dot_general(q[hh].astype(BF16), k[hh].astype(BF16), _NT,
                                     preferred_element_type=F32)]
            for lvl in range(N_LEVELS):
                el = lv[hh][lvl]
                parts.append(lax.dot_general((q[hh] * el).astype(BF16), (k[hh] * el).astype(BF16),
                                             _NT, preferred_element_type=F32))
            acc = mk_ref[N_LEVELS] * parts[0]
            for lvl in range(N_LEVELS):
                acc = acc + mk_ref[lvl] * parts[1 + lvl]
            a.append(acc)
        o = []
        for hh in heads:
            st = st_ref[hh]
            qg = (q[hh] * e[hh][0:C]).astype(BF16)
            kg = (k[hh] * e[hh][C:2 * C]).astype(BF16)
            vb = v[hh].astype(BF16)
            oh = jnp.dot(a[hh].astype(BF16), vb, preferred_element_type=F32)
            o.append(oh + lax.dot_general(qg, st.astype(BF16), _NT, preferred_element_type=F32))
            st_ref[hh] = st * e[hh][C - 1:C, :] + jnp.dot(v[hh].T.astype(BF16), kg,
                                                          preferred_element_type=F32)
        for hh in heads:
            zg = zg_ref[pl.ds(r0, C), cols[hh]]
            ms = jnp.mean(o[hh] * o[hh], axis=-1, keepdims=True)
            og = o[hh] * lax.rsqrt(ms + EPS) * go_ref[:, cols[hh]] * (zg * _sigmoid(zg))
            o_ref[pl.ds(r0, C), cols[hh]] = og.astype(o_ref.dtype)
        return carry

    lax.fori_loop(0, n_chunks, chunk, 0)

    @pl.when(tb == pl.num_programs(2) - 1)
    def _():
        for hh in range(n_heads):
            sout_ref[hh] = st_ref[hh].T


def _hgrn_mix(z, lb, g_out, s0, B, T):
    D = z.shape[1] // 4
    H = D // HEAD
    hb = _pick(H, (HGRN_HEADS,))
    tb = _pick(T, (256, 128, 64))
    nt = T // tb
    nh = H // hb
    sm, mk = _hgrn_constants()

    def zspec(part):
        return pl.BlockSpec((tb, hb * HEAD), lambda b, h, t: (b * nt + t, part * nh + h))

    head_vec = pl.BlockSpec((1, hb * HEAD), lambda b, h, t: (0, h))
    state = pl.BlockSpec((None, hb, HEAD, HEAD), lambda b, h, t: (b, h, 0, 0))
    og, s_fin = pl.pallas_call(
        functools.partial(_hgrn_kernel, n_chunks=tb // CHUNK, n_heads=hb),
        grid=(B, nh, nt),
        in_specs=[zspec(0), zspec(1), zspec(2), zspec(3), head_vec, head_vec, state,
                  pl.BlockSpec(sm.shape, lambda b, h, t: (0, 0)),
                  pl.BlockSpec(mk.shape, lambda b, h, t: (0, 0, 0))],
        out_specs=[pl.BlockSpec((tb, hb * HEAD), lambda b, h, t: (b * nt + t, h)), state],
        out_shape=[jax.ShapeDtypeStruct((B * T, D), BF16),
                   jax.ShapeDtypeStruct((B, H, HEAD, HEAD), F32)],
        scratch_shapes=[pltpu.VMEM((hb, HEAD, HEAD), F32)],
        compiler_params=_params(("parallel", "parallel", "arbitrary")),
        name="hgrn_mix",
    )(z, z, z, z, lb.reshape(1, D), g_out.reshape(1, D).astype(F32), s0, sm, mk)
    return og, s_fin


def _out_ple_kernel(og_ref, x_ref, p_ref, wo_ref, gpg_ref, wpg_ref, wpin_ref, gfin_ref, o_ref,
                    *, final, col_chunk):
    D = x_ref.shape[1]
    y = x_ref[...] + jnp.dot(og_ref[...], wo_ref[...], preferred_element_type=F32)
    ms = jnp.mean(y * y, axis=-1, keepdims=True)
    yn = (y * lax.rsqrt(ms + EPS) * gpg_ref[...]).astype(BF16)
    pb = p_ref[...].astype(BF16)
    parts = []
    for c0 in range(0, D, col_chunk):
        gate = _sigmoid(jnp.dot(yn, wpg_ref[:, c0:c0 + col_chunk], preferred_element_type=F32))
        emb = jnp.dot(pb, wpin_ref[:, c0:c0 + col_chunk], preferred_element_type=F32)
        parts.append(y[:, c0:c0 + col_chunk] + gate * emb)
    if final:
        ms2 = sum(jnp.sum(t * t, axis=-1, keepdims=True) for t in parts) * (1.0 / D)
        inv = lax.rsqrt(ms2 + EPS)
        for n, c0 in enumerate(range(0, D, col_chunk)):
            o_ref[:, c0:c0 + col_chunk] = parts[n] * inv * gfin_ref[:, c0:c0 + col_chunk]
    else:
        for n, c0 in enumerate(range(0, D, col_chunk)):
            o_ref[:, c0:c0 + col_chunk] = parts[n]


def _out_ple(og, x, p, layer, w_out, g_pg, w_pg, w_pin, g_final, final):
    T, D = x.shape
    P = p.shape[2]
    tm = _pick(T, (256, 128, 64))
    col_chunk = min(D, 512)
    rows = lambda i: (i, 0)
    fixed = lambda i: (0, 0)
    once = pl.Buffered(1)
    return pl.pallas_call(
        functools.partial(_out_ple_kernel, final=final, col_chunk=col_chunk),
        grid=(T // tm,),
        in_specs=[pl.BlockSpec((tm, D), rows), pl.BlockSpec((tm, D), rows),
                  pl.BlockSpec((None, tm, P), lambda i: (layer, i, 0)),
                  pl.BlockSpec((D, D), fixed, pipeline_mode=once),
                  pl.BlockSpec((1, D), fixed),
                  pl.BlockSpec((D, D), fixed, pipeline_mode=once),
                  pl.BlockSpec((P, D), fixed, pipeline_mode=once),
                  pl.BlockSpec((1, D), fixed)],
        out_specs=pl.BlockSpec((tm, D), rows),
        out_shape=jax.ShapeDtypeStruct((T, D), F32),
        compiler_params=_params(("parallel",)),
        name="out_ple",
    )(og, x, p, w_out, g_pg.reshape(1, D).astype(F32), w_pg, w_pin,
      g_final.reshape(1, D).astype(F32))


def _cumsum_kernel(x_ref, u_ref, o_ref, carry_ref, *, n_sub):
    @pl.when(pl.program_id(0) == 0)
    def _():
        carry_ref[...] = jnp.zeros_like(carry_ref)

    u = u_ref[...]
    c = carry_ref[:, 0:1]
    for s in range(n_sub):
        x = x_ref[:, s * LANES:(s + 1) * LANES]
        hi = x.astype(BF16)
        r1 = x - hi.astype(F32)
        mid = r1.astype(BF16)
        lo = (r1 - mid.astype(F32)).astype(BF16)
        cs = (jnp.dot(hi, u, preferred_element_type=F32)
              + jnp.dot(mid, u, preferred_element_type=F32)
              + jnp.dot(lo, u, preferred_element_type=F32)) + c
        o_ref[:, s * LANES:(s + 1) * LANES] = cs
        c = cs[:, LANES - 1:LANES]
    carry_ref[...] = jnp.broadcast_to(c, carry_ref.shape)


def _cumsum_last(x):
    R, L = x.shape
    cb = _pick(L, (2048, 1024, 512, 256, 128))
    u = jnp.asarray(np.triu(np.ones((LANES, LANES), np.float32)), BF16)
    return pl.pallas_call(
        functools.partial(_cumsum_kernel, n_sub=cb // LANES),
        grid=(L // cb,),
        in_specs=[pl.BlockSpec((R, cb), lambda j: (0, j)),
                  pl.BlockSpec((LANES, LANES), lambda j: (0, 0))],
        out_specs=pl.BlockSpec((R, cb), lambda j: (0, j)),
        out_shape=jax.ShapeDtypeStruct((R, L), F32),
        scratch_shapes=[pltpu.VMEM((R, LANES), F32)],
        compiler_params=_params(("arbitrary",)),
        name="cumsum",
    )(x, u)


def _fox_kernel(q_ref, gate_ref, k_ref, v_ref, dq_ref, dk_ref, o_ref, *, tq, tk, q_off, scale):
    i = pl.program_id(2)
    q = q_ref[...].astype(BF16)
    dq = dq_ref[...]
    first_pos = q_off + i * tq
    n_full = (first_pos + 1) // tk
    n_all = (first_pos + tq - 1) // tk + 1

    def step(j, carry, masked):
        m, l, acc = carry
        start = pl.multiple_of(j * tk, tk)
        k = k_ref[pl.ds(start, tk), :]
        v = v_ref[pl.ds(start, tk), :]
        s = lax.dot_general(q, k, _NT, preferred_element_type=F32) * scale
        s = s + dq - dk_ref[:, pl.ds(start, tk)]
        if masked:
            kpos = start + lax.broadcasted_iota(jnp.int32, (tq, tk), 1)
            qpos = first_pos + lax.broadcasted_iota(jnp.int32, (tq, tk), 0)
            s = jnp.where(kpos <= qpos, s, MASK_VALUE)
        m_new = jnp.maximum(m, jnp.max(s, axis=-1, keepdims=True))
        alpha = jnp.exp(m - m_new)
        p = jnp.exp(s - m_new)
        l = alpha * l + jnp.sum(p, axis=-1, keepdims=True)
        acc = alpha * acc + jnp.dot(p.astype(BF16), v, preferred_element_type=F32)
        return m_new, l, acc

    init = (jnp.full((tq, 1), MASK_VALUE, F32), jnp.zeros((tq, 1), F32),
            jnp.zeros((tq, HEAD), F32))
    carry = lax.fori_loop(0, n_full, functools.partial(step, masked=False), init)
    _, l, acc = lax.fori_loop(n_full, n_all, functools.partial(step, masked=True), carry)
    gate = gate_ref[...]
    o = acc * (1.0 / l)
    o_ref[...] = (o * (gate * _sigmoid(gate))).astype(o_ref.dtype)


def _fox_short_kernel(q_ref, gate_ref, k_ref, v_ref, dq_ref, dk_ref, o_ref, *, n_heads, q_off,
                      scale):
    T, Lk = q_ref.shape[0], k_ref.shape[0]
    heads = range(n_heads)
    cols = [slice(hh * HEAD, (hh + 1) * HEAD) for hh in heads]
    visible = (lax.broadcasted_iota(jnp.int32, (T, Lk), 1)
               <= q_off + lax.broadcasted_iota(jnp.int32, (T, Lk), 0))
    s = []
    for hh in heads:
        sh = lax.dot_general(q_ref[:, cols[hh]].astype(BF16), k_ref[:, cols[hh]], _NT,
                             preferred_element_type=F32) * scale
        s.append(jnp.where(visible, sh + dq_ref[hh] - dk_ref[hh], MASK_VALUE))
    p = [jnp.exp(sh - jnp.max(sh, axis=-1, keepdims=True)) for sh in s]
    o = [jnp.dot(p[hh].astype(BF16), v_ref[:, cols[hh]], preferred_element_type=F32)
         for hh in heads]
    for hh in heads:
        gate = gate_ref[:, cols[hh]]
        oh = o[hh] * (1.0 / jnp.sum(p[hh], axis=-1, keepdims=True))
        o_ref[:, cols[hh]] = (oh * (gate * _sigmoid(gate))).astype(o_ref.dtype)


def _fox_mix_short(z, k_all, v_all, dq, dk, B, T, q_off):
    D = z.shape[1] // 2
    H = D // HEAD
    Lk = k_all.shape[1]
    hb = _pick(H, (FOX_SHORT_HEADS,))
    nh = H // hb
    kv = pl.BlockSpec((None, Lk, hb * HEAD), lambda b, h: (b, 0, h))
    return pl.pallas_call(
        functools.partial(_fox_short_kernel, n_heads=hb, q_off=q_off, scale=HEAD ** -0.5),
        grid=(B, nh),
        in_specs=[pl.BlockSpec((T, hb * HEAD), lambda b, h: (b, h)),
                  pl.BlockSpec((T, hb * HEAD), lambda b, h: (b, nh + h)),
                  kv, kv,
                  pl.BlockSpec((None, hb, T, 1), lambda b, h: (b, h, 0, 0)),
                  pl.BlockSpec((None, hb, 1, Lk), lambda b, h: (b, h, 0, 0))],
        out_specs=pl.BlockSpec((T, hb * HEAD), lambda b, h: (b, h)),
        out_shape=jax.ShapeDtypeStruct((B * T, D), BF16),
        compiler_params=_params(("parallel", "parallel")),
        name="fox_mix_short",
    )(z, z, k_all, v_all, dq, dk)


def _fox_mix(z, k_all, v_all, dq, dk, B, T, q_off):
    D = z.shape[1] // 2
    H = D // HEAD
    Lk = k_all.shape[1]
    if T <= FOX_SHORT_QUERIES and Lk <= FOX_SHORT_KEYS:
        return _fox_mix_short(z, k_all, v_all, dq, dk, B, T, q_off)
    tq = _pick(T, (512, 256, 128, 64))
    tk = Lk if Lk <= 2048 else _pick(Lk, (512, 256, 128))
    nq = T // tq
    kv = pl.BlockSpec((None, Lk, HEAD), lambda b, h, i: (b, 0, h))
    return pl.pallas_call(
        functools.partial(_fox_kernel, tq=tq, tk=tk, q_off=q_off, scale=HEAD ** -0.5),
        grid=(B, H, nq),
        in_specs=[pl.BlockSpec((tq, HEAD), lambda b, h, i: (b * nq + i, h)),
                  pl.BlockSpec((tq, HEAD), lambda b, h, i: (b * nq + i, H + h)),
                  kv, kv,
                  pl.BlockSpec((None, None, tq, 1), lambda b, h, i: (b, h, i, 0)),
                  pl.BlockSpec((None, None, 1, Lk), lambda b, h, i: (b, h, 0, 0))],
        out_specs=pl.BlockSpec((tq, HEAD), lambda b, h, i: (b * nq + i, h)),
        out_shape=jax.ShapeDtypeStruct((B * T, D), BF16),
        compiler_params=_params(("parallel", "parallel", "arbitrary")),
        name="fox_mix",
    )(z, z, k_all, v_all, dq, dk)


FOX_BOUND_MARGIN = 1.02
FOX_BOUND_LIMIT = 32.0
FOX_FAST_TILE = 1024
AUG = 6


def _top16(x):
    bits = lax.bitcast_convert_type(x, jnp.uint32) & jnp.uint32(0xFFFF0000)
    return lax.bitcast_convert_type(bits, F32)


def _split3(x):
    hi = _top16(x)
    r1 = x - hi
    mid = _top16(r1)
    lo = r1 - mid
    return hi.astype(BF16), mid.astype(BF16), lo.astype(BF16)


def _bias_terms(x, shape, axis, value_first):
    idx = lax.broadcasted_iota(jnp.int32, shape, axis)
    v0, o0 = (0, AUG // 2) if value_first else (AUG // 2, 0)
    hi, mid, lo = _split3(x)
    ones = (idx >= o0) & (idx < o0 + AUG // 2)
    return jnp.where(idx == v0, hi.astype(F32), jnp.where(idx == v0 + 1, mid.astype(F32),
           jnp.where(idx == v0 + 2, lo.astype(F32), jnp.where(ones, 1.0, 0.0))))


def _sq_sums_kernel(x_ref, e_ref, o_ref):
    x = x_ref[...]
    o_ref[...] = lax.dot_general(e_ref[...], (x * x).astype(BF16), _NT,
                                 preferred_element_type=F32)


def _head_sq_sums(x, D):
    T = x.shape[0]
    H = D // HEAD
    tm = _pick(T, (512, 256, 128, 64))
    e = np.zeros((LANES, D), np.float32)
    e[np.arange(D) // HEAD, np.arange(D)] = 1.0
    return pl.pallas_call(
        _sq_sums_kernel,
        grid=(T // tm,),
        in_specs=[pl.BlockSpec((tm, D), lambda i: (i, 0)),
                  pl.BlockSpec((LANES, D), lambda i: (0, 0))],
        out_specs=pl.BlockSpec((LANES, tm), lambda i: (0, i)),
        out_shape=jax.ShapeDtypeStruct((LANES, T), F32),
        compiler_params=_params(("parallel",)),
        name="head_sq_sums",
    )(x, jnp.asarray(e, BF16))[:H]


def _fox_fast_kernel(q_ref, c_ref, gate_ref, ka_ref, vt_ref, o_ref, qa_ref, *, tile, scale):
    i = pl.program_id(2)
    qa_ref[:, :HEAD] = (q_ref[...] * scale).astype(BF16)
    qa_ref[:, HEAD:] = _bias_terms(c_ref[...], (HEAD, tile), 0, True).T.astype(BF16)

    def scores(j, masked):
        start = pl.multiple_of(j * tile, tile)
        s = lax.dot_general(ka_ref[pl.ds(start, tile), :], qa_ref[...], _NT,
                            preferred_element_type=F32)
        if masked:
            kpos = lax.broadcasted_iota(jnp.int32, (tile, tile), 0)
            qpos = lax.broadcasted_iota(jnp.int32, (tile, tile), 1)
            s = jnp.where(kpos <= qpos, s, MASK_VALUE)
        return s

    def consume(s, j, lp, acc):
        start = pl.multiple_of(j * tile, tile)
        p = jnp.exp(s)
        for g in range(tile // SUBLANES):
            lp = lp + p[g * SUBLANES:(g + 1) * SUBLANES, :]
        acc = acc + jnp.dot(vt_ref[:, pl.ds(start, tile)], p.astype(BF16),
                            preferred_element_type=F32)
        return lp, acc

    def pair(j0, carry, masked_b):
        lp, acc = carry
        sa = scores(j0, False)
        sb = scores(j0 + 1, masked_b)
        lp, acc = consume(sa, j0, lp, acc)
        return consume(sb, j0 + 1, lp, acc)

    init = (jnp.zeros((SUBLANES, tile), F32), jnp.zeros((HEAD, tile), F32))
    carry = lax.fori_loop(0, i // 2, lambda t, c: pair(2 * t, c, False), init)
    lp, acc = lax.cond(i % 2 == 1, lambda c: pair(i - 1, c, True),
                       lambda c: consume(scores(i, True), i, *c), carry)
    gate = gate_ref[...]
    o = (acc * (1.0 / jnp.sum(lp, axis=0, keepdims=True))).T
    o_ref[...] = (o * (gate * _sigmoid(gate))).astype(o_ref.dtype)


def _fox_mix_fast(z, c, k_aug, v_t, B, T):
    D = z.shape[1] // 2
    H = D // HEAD
    tile = FOX_FAST_TILE
    nq = T // tile
    return pl.pallas_call(
        functools.partial(_fox_fast_kernel, tile=tile, scale=HEAD ** -0.5),
        grid=(B, H, nq),
        in_specs=[pl.BlockSpec((tile, HEAD), lambda b, h, i: (b * nq + i, h)),
                  pl.BlockSpec((None, None, 1, tile), lambda b, h, i: (b, h, 0, i)),
                  pl.BlockSpec((tile, HEAD), lambda b, h, i: (b * nq + i, H + h)),
                  pl.BlockSpec((None, T, 2 * HEAD), lambda b, h, i: (b, 0, h)),
                  pl.BlockSpec((None, None, HEAD, T), lambda b, h, i: (b, h, 0, 0))],
        out_specs=pl.BlockSpec((tile, HEAD), lambda b, h, i: (b * nq + i, h)),
        out_shape=jax.ShapeDtypeStruct((B * T, D), BF16),
        scratch_shapes=[pltpu.VMEM((tile, 2 * HEAD), BF16)],
        compiler_params=_params(("parallel", "parallel", "arbitrary")),
        name="fox_mix_fast",
    )(z, c, z, k_aug, v_t)


def _augment_kernel(k_ref, dk_ref, o_ref, *, n_heads):
    tm = k_ref.shape[0]
    for h in range(n_heads):
        o_ref[:, 2 * h * HEAD:(2 * h + 1) * HEAD] = k_ref[:, h * HEAD:(h + 1) * HEAD].astype(BF16)
        o_ref[:, (2 * h + 1) * HEAD:(2 * h + 2) * HEAD] = _bias_terms(
            -dk_ref[:, h:h + 1], (tm, HEAD), 1, False).astype(BF16)


def _augment_keys(k, dcum, B, Lk, H):
    D = H * HEAD
    T = B * Lk
    tm = _pick(T, (512, 256, 128, 64))
    dk_rows = jnp.transpose(dcum, (0, 2, 1)).reshape(T, H)
    out = pl.pallas_call(
        functools.partial(_augment_kernel, n_heads=H),
        grid=(T // tm,),
        in_specs=[pl.BlockSpec((tm, D), lambda i: (i, 0)),
                  pl.BlockSpec((tm, H), lambda i: (i, 0))],
        out_specs=pl.BlockSpec((tm, 2 * D), lambda i: (i, 0)),
        out_shape=jax.ShapeDtypeStruct((T, 2 * D), BF16),
        compiler_params=_params(("parallel",)),
        name="augment_keys",
    )(k, dk_rows)
    return out.reshape(B, Lk, 2 * D)


def _trunk(x, p, s0, past, W):
    B, T, D = x.shape
    H = D // HEAD
    depth = p.shape[0]
    n_a = W["w_in_a"].shape[0]
    xf = x.reshape(B * T, D)
    pf = p.reshape(depth, B * T, p.shape[-1])
    states = []
    for layer in range(depth):
        final = layer == depth - 1
        if layer < n_a:
            z = _norm_matmul(xf, W["g_norm_a"][layer], W["w_in_a"][layer])
            og, s_fin = _hgrn_mix(z, W["lbs"][layer], W["g_out_a"][layer], s0[layer], B, T)
            states.append(s_fin)
            w_out = W["w_out_a"][layer]
        else:
            j = layer - n_a
            z = _norm_matmul(xf, W["g_norm_b"][j], W["w_in_b"][j])
            if k_aug is None:
                og = _fox_mix(z, k_all, v_all, dq, dk, B, T, q_off)
            else:
                qn = jnp.transpose(jnp.sqrt(_head_sq_sums(z, D)).reshape(H, B, T), (1, 0, 2))
                bound = (FOX_BOUND_MARGIN * HEAD ** -0.5) * qn * kmax
                c = (dcum - bound)[:, :, None, :]
                og = lax.cond(jnp.max(bound) <= FOX_BOUND_LIMIT,
                              lambda: _fox_mix_fast(z, c, k_aug, v_t, B, T),
                              lambda: _fox_mix(z, k_new.reshape(B, T, D).astype(BF16), v_all, dq,
                                               dk, B, T, q_off))
            w_out = W["w_out_b"][j]
        xf = _out_ple(og, xf, pf, layer, w_out, W["g_ple"][layer], W["w_ple_gate"][layer],
                      W["w_ple_in"][layer], W["g_final"], final)
        if layer == n_a - 1:
            fast = past is None and T % FOX_FAST_TILE == 0
            if fast:
                k_new, v_new, v_bf = _norm_matmul_kv(xf, W["g_kv"], W["w_kv"])
                k_bf = None
            else:
                k_new, k_bf = _norm_matmul(xf, W["g_kv"], W["w_kv"][:, :D], with_bf16=True)
                v_new, v_bf = _norm_matmul(xf, W["g_kv"], W["w_kv"][:, D:], with_bf16=True)
                k_bf = k_bf.reshape(B, T, D)
            logf_new = _norm_matmul(xf, W["g_kv"], W["w_f"], bias=W["b_f"])[:, :H]
            logf_new = logf_new.reshape(B, T, H)
            v_bf = v_bf.reshape(B, T, D)
            if past is None:
                k_all, v_all, logf_all, q_off = k_bf, v_bf, logf_new, 0
            else:
                past_k, past_v, past_logf = past
                q_off = past_k.shape[1]
                k_all = jnp.concatenate([past_k.reshape(B, q_off, D).astype(BF16), k_bf], axis=1)
                v_all = jnp.concatenate([past_v.reshape(B, q_off, D).astype(BF16), v_bf], axis=1)
                logf_all = jnp.concatenate([past_logf.astype(F32), logf_new], axis=1)
            Lk = v_all.shape[1]
            pad = (-Lk) % LANES
            if pad:
                k_all = jnp.pad(k_all, ((0, 0), (0, pad), (0, 0)))
                v_all = jnp.pad(v_all, ((0, 0), (0, pad), (0, 0)))
                logf_all = jnp.pad(logf_all, ((0, 0), (0, pad), (0, 0)))
            rows = jnp.transpose(logf_all, (0, 2, 1)).reshape(B * H, Lk + pad)
            dcum = _cumsum_last(rows).reshape(B, H, Lk + pad)
            dk = dcum[:, :, None, :]
            dq = dcum[:, :, q_off:q_off + T, None]
            k_aug = None
            if fast:
                kn = jnp.sqrt(_head_sq_sums(k_new, D)).reshape(H, B, T)
                kmax = jnp.transpose(jnp.max(kn, axis=2))[:, :, None]
                k_aug = _augment_keys(k_new, dcum, B, T, H)
                v_t = jnp.transpose(v_bf.reshape(B, T, H, HEAD), (0, 2, 3, 1))
    return (xf.reshape(B, T, D), jnp.stack(states), k_new.reshape(B, T, H, HEAD),
            v_new.reshape(B, T, H, HEAD), logf_new)


def kernel(x_prompt, x_sample, state_hgrn, cache_k, cache_v, cache_logf, p_prompt, p_sample,
           g_norm_a, w_in_a, lb_logits, g_out_a, w_out_a, g_kv, w_kv, b_f,
           g_norm_b, w_in_b, w_out_b, w_ple_in, g_ple, w_ple_gate, g_final):
    D = x_prompt.shape[-1]
    H = D // HEAD
    n_a = w_in_a.shape[0]
    lbs = jnp.cumsum(jax.nn.softmax(lb_logits.astype(F32), axis=0), axis=0)
    lbs = lbs - lbs[:1]
    w_f = jnp.pad(w_kv[:, 2 * D:], ((0, 0), (0, LANES - H)))
    W = dict(
        g_norm_a=g_norm_a, w_in_a=w_in_a.astype(BF16), lbs=lbs, g_out_a=g_out_a,
        w_out_a=w_out_a.astype(BF16), g_kv=g_kv, w_kv=w_kv[:, :2 * D].astype(BF16),
        w_f=w_f.astype(BF16),
        b_f=jnp.pad(b_f, (0, LANES - H)), g_norm_b=g_norm_b, w_in_b=w_in_b.astype(BF16),
        w_out_b=w_out_b.astype(BF16), w_ple_in=w_ple_in.astype(BF16), g_ple=g_ple,
        w_ple_gate=w_ple_gate.astype(BF16), g_final=g_final)
    s0_prompt = jnp.zeros((n_a, x_prompt.shape[0], H, HEAD, HEAD), F32)
    y_p, st_p, k_p, v_p, f_p = _trunk(x_prompt, p_prompt, s0_prompt, None, W)
    y_s, st_s, k_s, v_s, f_s = _trunk(x_sample, p_sample, state_hgrn,
                                      (cache_k, cache_v, cache_logf), W)
    return (y_p, y_s, st_p, st_s, k_p, v_p, f_p, k_s, v_s, f_s)
```

```python
import functools

import jax
import jax.numpy as jnp
import numpy as np
from jax import lax
from jax.experimental import pallas as pl
from jax.experimental.pallas import tpu as pltpu

F32 = jnp.float32
BF16 = jnp.bfloat16

EPS = 1e-6
HEAD = 128
LANES = 128
SUBLANES = 8
CHUNK = 64
N_LEVELS = 6
N_TOP = 3
HGRN_HEADS = 16
FOX_SHORT_HEADS = 4
FOX_SHORT_QUERIES = 128
FOX_SHORT_KEYS = 2048
MASK_VALUE = -1e30
VMEM_LIMIT = 56 * 1024 * 1024

_NT = (((1,), (1,)), ((), ()))


def _params(sem, vmem=VMEM_LIMIT):
    return pltpu.CompilerParams(dimension_semantics=sem, vmem_limit_bytes=vmem)


def _sigmoid(x):
    return 1.0 / (1.0 + jnp.exp(-x))


def _pick(n, prefs):
    for p in prefs:
        if n % p == 0:
            return p
    return n


def _nmm_kernel(*refs, log_sigmoid, with_bf16, row_chunk):
    if log_sigmoid:
        x_ref, g_ref, w_ref, b_ref = refs[:4]
        rest = refs[4:]
    else:
        x_ref, g_ref, w_ref = refs[:3]
        b_ref = None
        rest = refs[3:]
    if with_bf16:
        o_ref, ob_ref, xn_ref = rest
    else:
        o_ref, xn_ref = rest
        ob_ref = None
    tm = x_ref.shape[0]

    @pl.when(pl.program_id(1) == 0)
    def _():
        for r in range(0, tm, row_chunk):
            x = x_ref[r:r + row_chunk, :]
            ms = jnp.mean(x * x, axis=-1, keepdims=True)
            xn_ref[r:r + row_chunk, :] = (x * lax.rsqrt(ms + EPS) * g_ref[...]).astype(BF16)

    z = jnp.dot(xn_ref[...], w_ref[...], preferred_element_type=F32)
    if log_sigmoid:
        z = z + b_ref[...]
        z = jnp.minimum(z, 0.0) - jnp.log(1.0 + jnp.exp(-jnp.abs(z)))
    o_ref[...] = z
    if with_bf16:
        ob_ref[...] = z.astype(BF16)


def _norm_matmul(x, g, w, bias=None, with_bf16=False):
    T, D = x.shape
    N = w.shape[1]
    tm = _pick(T, (1024, 512, 256, 128, 64))
    tn = _pick(N, (1024, 512, 256, 128))
    row_chunk = min(tm, 256)
    in_specs = [
        pl.BlockSpec((tm, D), lambda i, j: (i, 0)),
        pl.BlockSpec((1, D), lambda i, j: (0, 0)),
        pl.BlockSpec((D, tn), lambda i, j: (0, j)),
    ]
    args = [x, g.reshape(1, D).astype(F32), w]
    if bias is not None:
        in_specs.append(pl.BlockSpec((1, tn), lambda i, j: (0, j)))
        args.append(bias.reshape(1, N).astype(F32))
    out_shape = [jax.ShapeDtypeStruct((T, N), F32)]
    out_specs = [pl.BlockSpec((tm, tn), lambda i, j: (i, j))]
    if with_bf16:
        out_shape.append(jax.ShapeDtypeStruct((T, N), BF16))
        out_specs.append(pl.BlockSpec((tm, tn), lambda i, j: (i, j)))
    out = pl.pallas_call(
        functools.partial(_nmm_kernel, log_sigmoid=bias is not None, with_bf16=with_bf16,
                          row_chunk=row_chunk),
        grid=(T // tm, N // tn),
        in_specs=in_specs,
        out_specs=out_specs,
        out_shape=out_shape,
        scratch_shapes=[pltpu.VMEM((tm, D), BF16)],
        compiler_params=_params(("parallel", "arbitrary")),
        name="norm_matmul",
    )(*args)
    return out if with_bf16 else out[0]


def _nmm_kv_kernel(x_ref, g_ref, w_ref, wf_ref, bf_ref, k_ref, v_ref, vb_ref, f_ref, xn_ref, *,
                   half, row_chunk):
    tm = x_ref.shape[0]
    j = pl.program_id(1)

    @pl.when(j == 0)
    def _():
        for r in range(0, tm, row_chunk):
            x = x_ref[r:r + row_chunk, :]
            ms = jnp.mean(x * x, axis=-1, keepdims=True)
            xn_ref[r:r + row_chunk, :] = (x * lax.rsqrt(ms + EPS) * g_ref[...]).astype(BF16)

    z = jnp.dot(xn_ref[...], w_ref[...], preferred_element_type=F32)

    @pl.when(j < half)
    def _():
        k_ref[...] = z

    @pl.when(j >= half)
    def _():
        v_ref[...] = z
        vb_ref[...] = z.astype(BF16)

    @pl.when(j == 2 * half - 1)
    def _():
        t = jnp.dot(xn_ref[...], wf_ref[...], preferred_element_type=F32) + bf_ref[...]
        f_ref[...] = jnp.minimum(t, 0.0) - jnp.log(1.0 + jnp.exp(-jnp.abs(t)))


def _norm_matmul_kv(x, g, w_kv, w_f, b_f):
    T, D = x.shape
    N = w_kv.shape[1] // 2
    tm = _pick(T, (1024, 512, 256, 128, 64))
    tn = _pick(N, (1024, 512, 256, 128))
    half = N // tn
    k_cols = lambda i, j: (i, jnp.minimum(j, half - 1))
    v_cols = lambda i, j: (i, jnp.maximum(j - half, 0))
    return pl.pallas_call(
        functools.partial(_nmm_kv_kernel, half=half, row_chunk=min(tm, 256)),
        grid=(T // tm, 2 * half),
        in_specs=[pl.BlockSpec((tm, D), lambda i, j: (i, 0)),
                  pl.BlockSpec((1, D), lambda i, j: (0, 0)),
                  pl.BlockSpec((D, tn), lambda i, j: (0, j)),
                  pl.BlockSpec((D, LANES), lambda i, j: (0, 0)),
                  pl.BlockSpec((1, LANES), lambda i, j: (0, 0))],
        out_specs=[pl.BlockSpec((tm, tn), k_cols), pl.BlockSpec((tm, tn), v_cols),
                   pl.BlockSpec((tm, tn), v_cols), pl.BlockSpec((tm, LANES), lambda i, j: (i, 0))],
        out_shape=[jax.ShapeDtypeStruct((T, N), F32), jax.ShapeDtypeStruct((T, N), F32),
                   jax.ShapeDtypeStruct((T, N), BF16), jax.ShapeDtypeStruct((T, LANES), F32)],
        scratch_shapes=[pltpu.VMEM((tm, D), BF16)],
        compiler_params=_params(("parallel", "arbitrary")),
        name="norm_matmul_kv",
    )(x, g.reshape(1, D).astype(F32), w_kv, w_f, b_f.reshape(1, LANES).astype(F32))


def _hgrn_constants():
    C = CHUNK
    t = np.arange(C)[:, None]
    u = np.arange(C)[None, :]
    blocks = [(u <= t), (u > t)]
    masks = []
    for lvl in range(N_LEVELS):
        h = C >> (lvl + 1)
        mid = (t // (2 * h)) * (2 * h) + h
        if N_TOP <= lvl < N_LEVELS - 1:
            blocks.append(np.where(t >= mid, (u >= mid) & (u <= t), (u > t) & (u < mid)))
        mid_s = (u // (2 * h)) * (2 * h) + h
        masks.append((t // (2 * h) == u // (2 * h)) & (t >= mid) & (u < mid_s))
    masks.append(t == u)
    sm = np.concatenate(blocks, axis=0).astype(np.float32)
    mk = np.stack(masks, axis=0).astype(np.float32)
    return jnp.asarray(sm, BF16), jnp.asarray(mk, F32)


def _midpoint_rows(G, h):
    rows = [jnp.broadcast_to(G[b + h - 1:b + h, :], (2 * h, HEAD)) for b in range(0, CHUNK, 2 * h)]
    return rows[0] if len(rows) == 1 else jnp.concatenate(rows, axis=0)


def _hgrn_kernel(zq_ref, zf_ref, zi_ref, zg_ref, lb_ref, go_ref, s0_ref, sm_ref, mk_ref,
                 o_ref, sout_ref, st_ref, *, n_chunks, n_heads):
    C = CHUNK
    tb = pl.program_id(2)

    @pl.when(tb == 0)
    def _():
        for hh in range(n_heads):
            st_ref[hh] = s0_ref[hh].T

    odd_row = lax.broadcasted_iota(jnp.int32, (C, HEAD), 0) % 2 == 1

    def chunk(c, carry):
        r0 = pl.multiple_of(c * C, C)
        heads = range(n_heads)
        cols = [slice(hh * HEAD, (hh + 1) * HEAD) for hh in heads]
        q, k, v, f, d2 = [], [], [], [], []
        for hh in heads:
            lb = lb_ref[:, cols[hh]]
            zq = zq_ref[pl.ds(r0, C), cols[hh]]
            zf = zf_ref[pl.ds(r0, C), cols[hh]]
            q.append(zq * _sigmoid(zq))
            f.append(lb + (1.0 - lb) * _sigmoid(zf))
            g = jnp.log(f[hh])
            k.append(1.0 - f[hh])
            v.append(zi_ref[pl.ds(r0, C), cols[hh]])
            g_hi = g.astype(BF16)
            g_lo = (g - g_hi.astype(F32)).astype(BF16)
            d2.append(jnp.dot(sm_ref[...], jnp.concatenate([g_hi, g_lo], axis=1),
                              preferred_element_type=F32))
        e, lv = [], []
        for hh in heads:
            d = d2[hh][:, :HEAD] + d2[hh][:, HEAD:]
            e.append(jnp.exp(d))
            G = d[0:C]
            top = [jnp.exp(-jnp.abs(G - _midpoint_rows(G, C >> (lvl + 1)))) for lvl in range(N_TOP)]
            lv.append(top + [e[hh][2 * C:3 * C], e[hh][3 * C:4 * C],
                             jnp.where(odd_row, f[hh], 1.0)])
        a = []
        for hh in heads:
            parts = [lax.dot_general(q[hh].astype(BF16), k[hh].astype(BF16), _NT,
                                     preferred_element_type=F32)]
            for lvl in range(N_LEVELS):
                el = lv[hh][lvl]
                parts.append(lax.dot_general((q[hh] * el).astype(BF16), (k[hh] * el).astype(BF16),
                                             _NT, preferred_element_type=F32))
            acc = mk_ref[N_LEVELS] * parts[0]
            for lvl in range(N_LEVELS):
                acc = acc + mk_ref[lvl] * parts[1 + lvl]
            a.append(acc)
        o = []
        for hh in heads:
            st = st_ref[hh]
            qg = (q[hh] * e[hh][0:C]).astype(BF16)
            kg = (k[hh] * e[hh][C:2 * C]).astype(BF16)
            vb = v[hh].astype(BF16)
            oh = jnp.dot(a[hh].astype(BF16), vb, preferred_element_type=F32)
            o.append(oh + lax.dot_general(qg, st.astype(BF16), _NT, preferred_element_type=F32))
            st_ref[hh] = st * e[hh][C - 1:C, :] + jnp.dot(v[hh].T.astype(BF16), kg,
                                                          preferred_element_type=F32)
        for hh in heads:
            zg = zg_ref[pl.ds(r0, C), cols[hh]]
            ms = jnp.mean(o[hh] * o[hh], axis=-1, keepdims=True)
            og = o[hh] * lax.rsqrt(ms + EPS) * go_ref[:, cols[hh]] * (zg * _sigmoid(zg))
            o_ref[pl.ds(r0, C), cols[hh]] = og.astype(o_ref.dtype)
        return carry

    lax.fori_loop(0, n_chunks, chunk, 0)

    @pl.when(tb == pl.num_programs(2) - 1)
    def _():
        for hh in range(n_heads):
            sout_ref[hh] = st_ref[hh].T


def _hgrn_mix(z, lb, g_out, s0, B, T):
    D = z.shape[1] // 4
    H = D // HEAD
    hb = _pick(H, (HGRN_HEADS,))
    tb = _pick(T, (256, 128, 64))
    nt = T // tb
    nh = H // hb
    sm, mk = _hgrn_constants()

    def zspec(part):
        return pl.BlockSpec((tb, hb * HEAD), lambda b, h, t: (b * nt + t, part * nh + h))

    head_vec = pl.BlockSpec((1, hb * HEAD), lambda b, h, t: (0, h))
    state = pl.BlockSpec((None, hb, HEAD, HEAD), lambda b, h, t: (b, h, 0, 0))
    og, s_fin = pl.pallas_call(
        functools.partial(_hgrn_kernel, n_chunks=tb // CHUNK, n_heads=hb),
        grid=(B, nh, nt),
        in_specs=[zspec(0), zspec(1), zspec(2), zspec(3), head_vec, head_vec, state,
                  pl.BlockSpec(sm.shape, lambda b, h, t: (0, 0)),
                  pl.BlockSpec(mk.shape, lambda b, h, t: (0, 0, 0))],
        out_specs=[pl.BlockSpec((tb, hb * HEAD), lambda b, h, t: (b * nt + t, h)), state],
        out_shape=[jax.ShapeDtypeStruct((B * T, D), BF16),
                   jax.ShapeDtypeStruct((B, H, HEAD, HEAD), F32)],
        scratch_shapes=[pltpu.VMEM((hb, HEAD, HEAD), F32)],
        compiler_params=_params(("parallel", "parallel", "arbitrary")),
        name="hgrn_mix",
    )(z, z, z, z, lb.reshape(1, D), g_out.reshape(1, D).astype(F32), s0, sm, mk)
    return og, s_fin


def _out_ple_kernel(og_ref, x_ref, p_ref, wo_ref, gpg_ref, wpg_ref, wpin_ref, gfin_ref, o_ref,
                    *, final, col_chunk):
    D = x_ref.shape[1]
    y = x_ref[...] + jnp.dot(og_ref[...], wo_ref[...], preferred_element_type=F32)
    ms = jnp.mean(y * y, axis=-1, keepdims=True)
    yn = (y * lax.rsqrt(ms + EPS) * gpg_ref[...]).astype(BF16)
    pb = p_ref[...].astype(BF16)
    parts = []
    for c0 in range(0, D, col_chunk):
        gate = _sigmoid(jnp.dot(yn, wpg_ref[:, c0:c0 + col_chunk], preferred_element_type=F32))
        emb = jnp.dot(pb, wpin_ref[:, c0:c0 + col_chunk], preferred_element_type=F32)
        parts.append(y[:, c0:c0 + col_chunk] + gate * emb)
    if final:
        ms2 = sum(jnp.sum(t * t, axis=-1, keepdims=True) for t in parts) * (1.0 / D)
        inv = lax.rsqrt(ms2 + EPS)
        for n, c0 in enumerate(range(0, D, col_chunk)):
            o_ref[:, c0:c0 + col_chunk] = parts[n] * inv * gfin_ref[:, c0:c0 + col_chunk]
    else:
        for n, c0 in enumerate(range(0, D, col_chunk)):
            o_ref[:, c0:c0 + col_chunk] = parts[n]


def _out_ple(og, x, p, layer, w_out, g_pg, w_pg, w_pin, g_final, final):
    T, D = x.shape
    P = p.shape[2]
    tm = _pick(T, (256, 128, 64))
    col_chunk = min(D, 512)
    rows = lambda i: (i, 0)
    fixed = lambda i: (0, 0)
    once = pl.Buffered(1)
    return pl.pallas_call(
        functools.partial(_out_ple_kernel, final=final, col_chunk=col_chunk),
        grid=(T // tm,),
        in_specs=[pl.BlockSpec((tm, D), rows), pl.BlockSpec((tm, D), rows),
                  pl.BlockSpec((None, tm, P), lambda i: (layer, i, 0)),
                  pl.BlockSpec((D, D), fixed, pipeline_mode=once),
                  pl.BlockSpec((1, D), fixed),
                  pl.BlockSpec((D, D), fixed, pipeline_mode=once),
                  pl.BlockSpec((P, D), fixed, pipeline_mode=once),
                  pl.BlockSpec((1, D), fixed)],
        out_specs=pl.BlockSpec((tm, D), rows),
        out_shape=jax.ShapeDtypeStruct((T, D), F32),
        compiler_params=_params(("parallel",)),
        name="out_ple",
    )(og, x, p, w_out, g_pg.reshape(1, D).astype(F32), w_pg, w_pin,
      g_final.reshape(1, D).astype(F32))


def _cumsum_kernel(x_ref, u_ref, o_ref, carry_ref, *, n_sub):
    @pl.when(pl.program_id(0) == 0)
    def _():
        carry_ref[...] = jnp.zeros_like(carry_ref)

    u = u_ref[...]
    c = carry_ref[:, 0:1]
    for s in range(n_sub):
        x = x_ref[:, s * LANES:(s + 1) * LANES]
        hi = x.astype(BF16)
        r1 = x - hi.astype(F32)
        mid = r1.astype(BF16)
        lo = (r1 - mid.astype(F32)).astype(BF16)
        cs = (jnp.dot(hi, u, preferred_element_type=F32)
              + jnp.dot(mid, u, preferred_element_type=F32)
              + jnp.dot(lo, u, preferred_element_type=F32)) + c
        o_ref[:, s * LANES:(s + 1) * LANES] = cs
        c = cs[:, LANES - 1:LANES]
    carry_ref[...] = jnp.broadcast_to(c, carry_ref.shape)


def _cumsum_last(x):
    R, L = x.shape
    cb = _pick(L, (2048, 1024, 512, 256, 128))
    u = jnp.asarray(np.triu(np.ones((LANES, LANES), np.float32)), BF16)
    return pl.pallas_call(
        functools.partial(_cumsum_kernel, n_sub=cb // LANES),
        grid=(L // cb,),
        in_specs=[pl.BlockSpec((R, cb), lambda j: (0, j)),
                  pl.BlockSpec((LANES, LANES), lambda j: (0, 0))],
        out_specs=pl.BlockSpec((R, cb), lambda j: (0, j)),
        out_shape=jax.ShapeDtypeStruct((R, L), F32),
        scratch_shapes=[pltpu.VMEM((R, LANES), F32)],
        compiler_params=_params(("arbitrary",)),
        name="cumsum",
    )(x, u)


def _fox_kernel(q_ref, gate_ref, k_ref, v_ref, dq_ref, dk_ref, o_ref, *, tq, tk, q_off, scale):
    i = pl.program_id(2)
    q = q_ref[...].astype(BF16)
    dq = dq_ref[...]
    first_pos = q_off + i * tq
    n_full = (first_pos + 1) // tk
    n_all = (first_pos + tq - 1) // tk + 1

    def step(j, carry, masked):
        m, l, acc = carry
        start = pl.multiple_of(j * tk, tk)
        k = k_ref[pl.ds(start, tk), :]
        v = v_ref[pl.ds(start, tk), :]
        s = lax.dot_general(q, k, _NT, preferred_element_type=F32) * scale
        s = s + dq - dk_ref[:, pl.ds(start, tk)]
        if masked:
            kpos = start + lax.broadcasted_iota(jnp.int32, (tq, tk), 1)
            qpos = first_pos + lax.broadcasted_iota(jnp.int32, (tq, tk), 0)
            s = jnp.where(kpos <= qpos, s, MASK_VALUE)
        m_new = jnp.maximum(m, jnp.max(s, axis=-1, keepdims=True))
        alpha = jnp.exp(m - m_new)
        p = jnp.exp(s - m_new)
        l = alpha * l + jnp.sum(p, axis=-1, keepdims=True)
        acc = alpha * acc + jnp.dot(p.astype(BF16), v, preferred_element_type=F32)
        return m_new, l, acc

    init = (jnp.full((tq, 1), MASK_VALUE, F32), jnp.zeros((tq, 1), F32),
            jnp.zeros((tq, HEAD), F32))
    carry = lax.fori_loop(0, n_full, functools.partial(step, masked=False), init)
    _, l, acc = lax.fori_loop(n_full, n_all, functools.partial(step, masked=True), carry)
    gate = gate_ref[...]
    o = acc * (1.0 / l)
    o_ref[...] = (o * (gate * _sigmoid(gate))).astype(o_ref.dtype)


def _fox_short_kernel(q_ref, gate_ref, k_ref, v_ref, dq_ref, dk_ref, o_ref, *, n_heads, q_off,
                      scale):
    T, Lk = q_ref.shape[0], k_ref.shape[0]
    heads = range(n_heads)
    cols = [slice(hh * HEAD, (hh + 1) * HEAD) for hh in heads]
    visible = (lax.broadcasted_iota(jnp.int32, (T, Lk), 1)
               <= q_off + lax.broadcasted_iota(jnp.int32, (T, Lk), 0))
    s = []
    for hh in heads:
        sh = lax.dot_general(q_ref[:, cols[hh]].astype(BF16), k_ref[:, cols[hh]], _NT,
                             preferred_element_type=F32) * scale
        s.append(jnp.where(visible, sh + dq_ref[hh] - dk_ref[hh], MASK_VALUE))
    p = [jnp.exp(sh - jnp.max(sh, axis=-1, keepdims=True)) for sh in s]
    o = [jnp.dot(p[hh].astype(BF16), v_ref[:, cols[hh]], preferred_element_type=F32)
         for hh in heads]
    for hh in heads:
        gate = gate_ref[:, cols[hh]]
        oh = o[hh] * (1.0 / jnp.sum(p[hh], axis=-1, keepdims=True))
        o_ref[:, cols[hh]] = (oh * (gate * _sigmoid(gate))).astype(o_ref.dtype)


def _fox_mix_short(z, k_all, v_all, dq, dk, B, T, q_off):
    D = z.shape[1] // 2
    H = D // HEAD
    Lk = k_all.shape[1]
    hb = _pick(H, (FOX_SHORT_HEADS,))
    nh = H // hb
    kv = pl.BlockSpec((None, Lk, hb * HEAD), lambda b, h: (b, 0, h))
    return pl.pallas_call(
        functools.partial(_fox_short_kernel, n_heads=hb, q_off=q_off, scale=HEAD ** -0.5),
        grid=(B, nh),
        in_specs=[pl.BlockSpec((T, hb * HEAD), lambda b, h: (b, h)),
                  pl.BlockSpec((T, hb * HEAD), lambda b, h: (b, nh + h)),
                  kv, kv,
                  pl.BlockSpec((None, hb, T, 1), lambda b, h: (b, h, 0, 0)),
                  pl.BlockSpec((None, hb, 1, Lk), lambda b, h: (b, h, 0, 0))],
        out_specs=pl.BlockSpec((T, hb * HEAD), lambda b, h: (b, h)),
        out_shape=jax.ShapeDtypeStruct((B * T, D), BF16),
        compiler_params=_params(("parallel", "parallel")),
        name="fox_mix_short",
    )(z, z, k_all, v_all, dq, dk)


def _fox_mix(z, k_all, v_all, dq, dk, B, T, q_off):
    D = z.shape[1] // 2
    H = D // HEAD
    Lk = k_all.shape[1]
    if T <= FOX_SHORT_QUERIES and Lk <= FOX_SHORT_KEYS:
        return _fox_mix_short(z, k_all, v_all, dq, dk, B, T, q_off)
    tq = _pick(T, (512, 256, 128, 64))
    tk = Lk if Lk <= 2048 else _pick(Lk, (512, 256, 128))
    nq = T // tq
    kv = pl.BlockSpec((None, Lk, HEAD), lambda b, h, i: (b, 0, h))
    return pl.pallas_call(
        functools.partial(_fox_kernel, tq=tq, tk=tk, q_off=q_off, scale=HEAD ** -0.5),
        grid=(B, H, nq),
        in_specs=[pl.BlockSpec((tq, HEAD), lambda b, h, i: (b * nq + i, h)),
                  pl.BlockSpec((tq, HEAD), lambda b, h, i: (b * nq + i, H + h)),
                  kv, kv,
                  pl.BlockSpec((None, None, tq, 1), lambda b, h, i: (b, h, i, 0)),
                  pl.BlockSpec((None, None, 1, Lk), lambda b, h, i: (b, h, 0, 0))],
        out_specs=pl.BlockSpec((tq, HEAD), lambda b, h, i: (b * nq + i, h)),
        out_shape=jax.ShapeDtypeStruct((B * T, D), BF16),
        compiler_params=_params(("parallel", "parallel", "arbitrary")),
        name="fox_mix",
    )(z, z, k_all, v_all, dq, dk)


FOX_BOUND_MARGIN = 1.02
FOX_BOUND_LIMIT = 32.0
FOX_FAST_TILE = 1024
AUG = 6


def _top16(x):
    bits = lax.bitcast_convert_type(x, jnp.uint32) & jnp.uint32(0xFFFF0000)
    return lax.bitcast_convert_type(bits, F32)


def _split3(x):
    hi = _top16(x)
    r1 = x - hi
    mid = _top16(r1)
    lo = r1 - mid
    return hi.astype(BF16), mid.astype(BF16), lo.astype(BF16)


def _bias_terms(x, shape, axis, value_first):
    idx = lax.broadcasted_iota(jnp.int32, shape, axis)
    v0, o0 = (0, AUG // 2) if value_first else (AUG // 2, 0)
    hi, mid, lo = _split3(x)
    ones = (idx >= o0) & (idx < o0 + AUG // 2)
    return jnp.where(idx == v0, hi.astype(F32), jnp.where(idx == v0 + 1, mid.astype(F32),
           jnp.where(idx == v0 + 2, lo.astype(F32), jnp.where(ones, 1.0, 0.0))))


def _sq_sums_kernel(x_ref, e_ref, o_ref):
    x = x_ref[...]
    o_ref[...] = lax.dot_general(e_ref[...], (x * x).astype(BF16), _NT,
                                 preferred_element_type=F32)


def _head_sq_sums(x, D):
    T = x.shape[0]
    H = D // HEAD
    tm = _pick(T, (512, 256, 128, 64))
    e = np.zeros((LANES, D), np.float32)
    e[np.arange(D) // HEAD, np.arange(D)] = 1.0
    return pl.pallas_call(
        _sq_sums_kernel,
        grid=(T // tm,),
        in_specs=[pl.BlockSpec((tm, D), lambda i: (i, 0)),
                  pl.BlockSpec((LANES, D), lambda i: (0, 0))],
        out_specs=pl.BlockSpec((LANES, tm), lambda i: (0, i)),
        out_shape=jax.ShapeDtypeStruct((LANES, T), F32),
        compiler_params=_params(("parallel",)),
        name="head_sq_sums",
    )(x, jnp.asarray(e, BF16))[:H]


def _fox_fast_kernel(q_ref, c_ref, gate_ref, ka_ref, vt_ref, o_ref, qa_ref, *, tile, scale):
    i = pl.program_id(2)
    qa_ref[:, :HEAD] = (q_ref[...] * scale).astype(BF16)
    qa_ref[:, HEAD:] = _bias_terms(c_ref[...], (HEAD, tile), 0, True).T.astype(BF16)

    def scores(j, masked):
        start = pl.multiple_of(j * tile, tile)
        s = lax.dot_general(ka_ref[pl.ds(start, tile), :], qa_ref[...], _NT,
                            preferred_element_type=F32)
        if masked:
            kpos = lax.broadcasted_iota(jnp.int32, (tile, tile), 0)
            qpos = lax.broadcasted_iota(jnp.int32, (tile, tile), 1)
            s = jnp.where(kpos <= qpos, s, MASK_VALUE)
        return s

    def consume(s, j, lp, acc):
        start = pl.multiple_of(j * tile, tile)
        p = jnp.exp(s)
        for g in range(tile // SUBLANES):
            lp = lp + p[g * SUBLANES:(g + 1) * SUBLANES, :]
        acc = acc + jnp.dot(vt_ref[:, pl.ds(start, tile)], p.astype(BF16),
                            preferred_element_type=F32)
        return lp, acc

    def pair(j0, carry, masked_b):
        lp, acc = carry
        sa = scores(j0, False)
        sb = scores(j0 + 1, masked_b)
        lp, acc = consume(sa, j0, lp, acc)
        return consume(sb, j0 + 1, lp, acc)

    init = (jnp.zeros((SUBLANES, tile), F32), jnp.zeros((HEAD, tile), F32))
    carry = lax.fori_loop(0, i // 2, lambda t, c: pair(2 * t, c, False), init)
    lp, acc = lax.cond(i % 2 == 1, lambda c: pair(i - 1, c, True),
                       lambda c: consume(scores(i, True), i, *c), carry)
    gate = gate_ref[...]
    o = (acc * (1.0 / jnp.sum(lp, axis=0, keepdims=True))).T
    o_ref[...] = (o * (gate * _sigmoid(gate))).astype(o_ref.dtype)


def _fox_mix_fast(z, c, k_aug, v_t, B, T):
    D = z.shape[1] // 2
    H = D // HEAD
    tile = FOX_FAST_TILE
    nq = T // tile
    return pl.pallas_call(
        functools.partial(_fox_fast_kernel, tile=tile, scale=HEAD ** -0.5),
        grid=(B, H, nq),
        in_specs=[pl.BlockSpec((tile, HEAD), lambda b, h, i: (b * nq + i, h)),
                  pl.BlockSpec((None, None, 1, tile), lambda b, h, i: (b, h, 0, i)),
                  pl.BlockSpec((tile, HEAD), lambda b, h, i: (b * nq + i, H + h)),
                  pl.BlockSpec((None, T, 2 * HEAD), lambda b, h, i: (b, 0, h)),
                  pl.BlockSpec((None, None, HEAD, T), lambda b, h, i: (b, h, 0, 0))],
        out_specs=pl.BlockSpec((tile, HEAD), lambda b, h, i: (b * nq + i, h)),
        out_shape=jax.ShapeDtypeStruct((B * T, D), BF16),
        scratch_shapes=[pltpu.VMEM((tile, 2 * HEAD), BF16)],
        compiler_params=_params(("parallel", "parallel", "arbitrary")),
        name="fox_mix_fast",
    )(z, c, z, k_aug, v_t)


def _augment_kernel(k_ref, dk_ref, o_ref, *, n_heads):
    tm = k_ref.shape[0]
    for h in range(n_heads):
        o_ref[:, 2 * h * HEAD:(2 * h + 1) * HEAD] = k_ref[:, h * HEAD:(h + 1) * HEAD].astype(BF16)
        o_ref[:, (2 * h + 1) * HEAD:(2 * h + 2) * HEAD] = _bias_terms(
            -dk_ref[:, h:h + 1], (tm, HEAD), 1, False).astype(BF16)


def _augment_keys(k, dcum, B, Lk, H):
    D = H * HEAD
    T = B * Lk
    tm = _pick(T, (512, 256, 128, 64))
    dk_rows = jnp.transpose(dcum, (0, 2, 1)).reshape(T, H)
    out = pl.pallas_call(
        functools.partial(_augment_kernel, n_heads=H),
        grid=(T // tm,),
        in_specs=[pl.BlockSpec((tm, D), lambda i: (i, 0)),
                  pl.BlockSpec((tm, H), lambda i: (i, 0))],
        out_specs=pl.BlockSpec((tm, 2 * D), lambda i: (i, 0)),
        out_shape=jax.ShapeDtypeStruct((T, 2 * D), BF16),
        compiler_params=_params(("parallel",)),
        name="augment_keys",
    )(k, dk_rows)
    return out.reshape(B, Lk, 2 * D)


def _trunk(x, p, s0, past, W):
    B, T, D = x.shape
    H = D // HEAD
    depth = p.shape[0]
    n_a = W["w_in_a"].shape[0]
    xf = x.reshape(B * T, D)
    pf = p.reshape(depth, B * T, p.shape[-1])
    states = []
    for layer in range(depth):
        final = layer == depth - 1
        if layer < n_a:
            z = _norm_matmul(xf, W["g_norm_a"][layer], W["w_in_a"][layer])
            og, s_fin = _hgrn_mix(z, W["lbs"][layer], W["g_out_a"][layer], s0[layer], B, T)
            states.append(s_fin)
            w_out = W["w_out_a"][layer]
        else:
            j = layer - n_a
            z = _norm_matmul(xf, W["g_norm_b"][j], W["w_in_b"][j])
            if k_aug is None:
                og = _fox_mix(z, k_all, v_all, dq, dk, B, T, q_off)
            else:
                qn = jnp.transpose(jnp.sqrt(_head_sq_sums(z, D)).reshape(H, B, T), (1, 0, 2))
                bound = (FOX_BOUND_MARGIN * HEAD ** -0.5) * qn * kmax
                c = (dcum - bound)[:, :, None, :]
                og = lax.cond(jnp.max(bound) <= FOX_BOUND_LIMIT,
                              lambda: _fox_mix_fast(z, c, k_aug, v_t, B, T),
                              lambda: _fox_mix(z, k_new.reshape(B, T, D).astype(BF16), v_all, dq,
                                               dk, B, T, q_off))
            w_out = W["w_out_b"][j]
        xf = _out_ple(og, xf, pf, layer, w_out, W["g_ple"][layer], W["w_ple_gate"][layer],
                      W["w_ple_in"][layer], W["g_final"], final)
        if layer == n_a - 1:
            fast = past is None and T % FOX_FAST_TILE == 0
            if fast:
                k_new, v_new, v_bf, logf_new = _norm_matmul_kv(xf, W["g_kv"], W["w_kv"], W["w_f"],
                                                               W["b_f"])
                k_bf = None
            else:
                k_new, k_bf = _norm_matmul(xf, W["g_kv"], W["w_kv"][:, :D], with_bf16=True)
                v_new, v_bf = _norm_matmul(xf, W["g_kv"], W["w_kv"][:, D:], with_bf16=True)
                k_bf = k_bf.reshape(B, T, D)
                logf_new = _norm_matmul(xf, W["g_kv"], W["w_f"], bias=W["b_f"])
            logf_new = logf_new[:, :H].reshape(B, T, H)
            v_bf = v_bf.reshape(B, T, D)
            if past is None:
                k_all, v_all, logf_all, q_off = k_bf, v_bf, logf_new, 0
            else:
                past_k, past_v, past_logf = past
                q_off = past_k.shape[1]
                k_all = jnp.concatenate([past_k.reshape(B, q_off, D).astype(BF16), k_bf], axis=1)
                v_all = jnp.concatenate([past_v.reshape(B, q_off, D).astype(BF16), v_bf], axis=1)
                logf_all = jnp.concatenate([past_logf.astype(F32), logf_new], axis=1)
            Lk = v_all.shape[1]
            pad = (-Lk) % LANES
            if pad:
                k_all = jnp.pad(k_all, ((0, 0), (0, pad), (0, 0)))
                v_all = jnp.pad(v_all, ((0, 0), (0, pad), (0, 0)))
                logf_all = jnp.pad(logf_all, ((0, 0), (0, pad), (0, 0)))
            rows = jnp.transpose(logf_all, (0, 2, 1)).reshape(B * H, Lk + pad)
            dcum = _cumsum_last(rows).reshape(B, H, Lk + pad)
            dk = dcum[:, :, None, :]
            dq = dcum[:, :, q_off:q_off + T, None]
            k_aug = None
            if fast:
                kn = jnp.sqrt(_head_sq_sums(k_new, D)).reshape(H, B, T)
                kmax = jnp.transpose(jnp.max(kn, axis=2))[:, :, None]
                k_aug = _augment_keys(k_new, dcum, B, T, H)
                v_t = jnp.transpose(v_bf.reshape(B, T, H, HEAD), (0, 2, 3, 1))
    return (xf.reshape(B, T, D), jnp.stack(states), k_new.reshape(B, T, H, HEAD),
            v_new.reshape(B, T, H, HEAD), logf_new)


def kernel(x_prompt, x_sample, state_hgrn, cache_k, cache_v, cache_logf, p_prompt, p_sample,
           g_norm_a, w_in_a, lb_logits, g_out_a, w_out_a, g_kv, w_kv, b_f,
           g_norm_b, w_in_b, w_out_b, w_ple_in, g_ple, w_ple_gate, g_final):
    D = x_prompt.shape[-1]
    H = D // HEAD
    n_a = w_in_a.shape[0]
    lbs = jnp.cumsum(jax.nn.softmax(lb_logits.astype(F32), axis=0), axis=0)
    lbs = lbs - lbs[:1]
    w_f = jnp.pad(w_kv[:, 2 * D:], ((0, 0), (0, LANES - H)))
    W = dict(
        g_norm_a=g_norm_a, w_in_a=w_in_a.astype(BF16), lbs=lbs, g_out_a=g_out_a,
        w_out_a=w_out_a.astype(BF16), g_kv=g_kv, w_kv=w_kv[:, :2 * D].astype(BF16),
        w_f=w_f.astype(BF16),
        b_f=jnp.pad(b_f, (0, LANES - H)), g_norm_b=g_norm_b, w_in_b=w_in_b.astype(BF16),
        w_out_b=w_out_b.astype(BF16), w_ple_in=w_ple_in.astype(BF16), g_ple=g_ple,
        w_ple_gate=w_ple_gate.astype(BF16), g_final=g_final)
    s0_prompt = jnp.zeros((n_a, x_prompt.shape[0], H, HEAD, HEAD), F32)
    y_p, st_p, k_p, v_p, f_p = _trunk(x_prompt, p_prompt, s0_prompt, None, W)
    y_s, st_s, k_s, v_s, f_s = _trunk(x_sample, p_sample, state_hgrn,
                                      (cache_k, cache_v, cache_logf), W)
    return (y_p, y_s, st_p, st_s, k_p, v_p, f_p, k_s, v_s, f_s)
```

```python
import functools

import jax
import jax.numpy as jnp
import numpy as np
from jax import lax
from jax.experimental import pallas as pl
from jax.experimental.pallas import tpu as pltpu

F32 = jnp.float32
BF16 = jnp.bfloat16

EPS = 1e-6
HEAD = 128
LANES = 128
SUBLANES = 8
CHUNK = 64
N_LEVELS = 6
N_TOP = 3
HGRN_HEADS = 16
FOX_SHORT_HEADS = 4
FOX_SHORT_QUERIES = 128
FOX_SHORT_KEYS = 2048
MASK_VALUE = -1e30
VMEM_LIMIT = 56 * 1024 * 1024

_NT = (((1,), (1,)), ((), ()))


def _params(sem, vmem=VMEM_LIMIT):
    return pltpu.CompilerParams(dimension_semantics=sem, vmem_limit_bytes=vmem)


def _sigmoid(x):
    return 1.0 / (1.0 + jnp.exp(-x))


def _pick(n, prefs):
    for p in prefs:
        if n % p == 0:
            return p
    return n


def _nmm_kernel(*refs, log_sigmoid, with_bf16, row_chunk):
    if log_sigmoid:
        x_ref, g_ref, w_ref, b_ref = refs[:4]
        rest = refs[4:]
    else:
        x_ref, g_ref, w_ref = refs[:3]
        b_ref = None
        rest = refs[3:]
    if with_bf16:
        o_ref, ob_ref, xn_ref = rest
    else:
        o_ref, xn_ref = rest
        ob_ref = None
    tm = x_ref.shape[0]

    @pl.when(pl.program_id(1) == 0)
    def _():
        for r in range(0, tm, row_chunk):
            x = x_ref[r:r + row_chunk, :]
            ms = jnp.mean(x * x, axis=-1, keepdims=True)
            xn_ref[r:r + row_chunk, :] = (x * lax.rsqrt(ms + EPS) * g_ref[...]).astype(BF16)

    z = jnp.dot(xn_ref[...], w_ref[...], preferred_element_type=F32)
    if log_sigmoid:
        z = z + b_ref[...]
        z = jnp.minimum(z, 0.0) - jnp.log(1.0 + jnp.exp(-jnp.abs(z)))
    o_ref[...] = z
    if with_bf16:
        ob_ref[...] = z.astype(BF16)


def _norm_matmul(x, g, w, bias=None, with_bf16=False):
    T, D = x.shape
    N = w.shape[1]
    tm = _pick(T, (1024, 512, 256, 128, 64))
    tn = _pick(N, (1024, 512, 256, 128))
    row_chunk = min(tm, 256)
    in_specs = [
        pl.BlockSpec((tm, D), lambda i, j: (i, 0)),
        pl.BlockSpec((1, D), lambda i, j: (0, 0)),
        pl.BlockSpec((D, tn), lambda i, j: (0, j)),
    ]
    args = [x, g.reshape(1, D).astype(F32), w]
    if bias is not None:
        in_specs.append(pl.BlockSpec((1, tn), lambda i, j: (0, j)))
        args.append(bias.reshape(1, N).astype(F32))
    out_shape = [jax.ShapeDtypeStruct((T, N), F32)]
    out_specs = [pl.BlockSpec((tm, tn), lambda i, j: (i, j))]
    if with_bf16:
        out_shape.append(jax.ShapeDtypeStruct((T, N), BF16))
        out_specs.append(pl.BlockSpec((tm, tn), lambda i, j: (i, j)))
    out = pl.pallas_call(
        functools.partial(_nmm_kernel, log_sigmoid=bias is not None, with_bf16=with_bf16,
                          row_chunk=row_chunk),
        grid=(T // tm, N // tn),
        in_specs=in_specs,
        out_specs=out_specs,
        out_shape=out_shape,
        scratch_shapes=[pltpu.VMEM((tm, D), BF16)],
        compiler_params=_params(("parallel", "arbitrary")),
        name="norm_matmul",
    )(*args)
    return out if with_bf16 else out[0]


def _nmm_kv_kernel(x_ref, g_ref, w_ref, wf_ref, bf_ref, k_ref, v_ref, vb_ref, f_ref, xn_ref, *,
                   half, row_chunk):
    tm = x_ref.shape[0]
    j = pl.program_id(1)

    @pl.when(j == 0)
    def _():
        for r in range(0, tm, row_chunk):
            x = x_ref[r:r + row_chunk, :]
            ms = jnp.mean(x * x, axis=-1, keepdims=True)
            xn_ref[r:r + row_chunk, :] = (x * lax.rsqrt(ms + EPS) * g_ref[...]).astype(BF16)

    z = jnp.dot(xn_ref[...], w_ref[...], preferred_element_type=F32)

    @pl.when(j < half)
    def _():
        k_ref[...] = z

    @pl.when(j >= half)
    def _():
        v_ref[...] = z
        vb_ref[...] = z.astype(BF16)

    @pl.when(j == 2 * half - 1)
    def _():
        t = jnp.dot(xn_ref[...], wf_ref[...], preferred_element_type=F32) + bf_ref[...]
        f_ref[...] = jnp.minimum(t, 0.0) - jnp.log(1.0 + jnp.exp(-jnp.abs(t)))


def _norm_matmul_kv(x, g, w_kv, w_f, b_f):
    T, D = x.shape
    N = w_kv.shape[1] // 2
    tm = _pick(T, (1024, 512, 256, 128, 64))
    tn = _pick(N, (1024, 512, 256, 128))
    half = N // tn
    k_cols = lambda i, j: (i, jnp.minimum(j, half - 1))
    v_cols = lambda i, j: (i, jnp.maximum(j - half, 0))
    return pl.pallas_call(
        functools.partial(_nmm_kv_kernel, half=half, row_chunk=min(tm, 256)),
        grid=(T // tm, 2 * half),
        in_specs=[pl.BlockSpec((tm, D), lambda i, j: (i, 0)),
                  pl.BlockSpec((1, D), lambda i, j: (0, 0)),
                  pl.BlockSpec((D, tn), lambda i, j: (0, j)),
                  pl.BlockSpec((D, LANES), lambda i, j: (0, 0)),
                  pl.BlockSpec((1, LANES), lambda i, j: (0, 0))],
        out_specs=[pl.BlockSpec((tm, tn), k_cols), pl.BlockSpec((tm, tn), v_cols),
                   pl.BlockSpec((tm, tn), v_cols), pl.BlockSpec((tm, LANES), lambda i, j: (i, 0))],
        out_shape=[jax.ShapeDtypeStruct((T, N), F32), jax.ShapeDtypeStruct((T, N), F32),
                   jax.ShapeDtypeStruct((T, N), BF16), jax.ShapeDtypeStruct((T, LANES), F32)],
        scratch_shapes=[pltpu.VMEM((tm, D), BF16)],
        compiler_params=_params(("parallel", "arbitrary")),
        name="norm_matmul_kv",
    )(x, g.reshape(1, D).astype(F32), w_kv, w_f, b_f.reshape(1, LANES).astype(F32))


def _hgrn_constants():
    C = CHUNK
    t = np.arange(C)[:, None]
    u = np.arange(C)[None, :]
    blocks = [(u <= t), (u > t)]
    masks = []
    for lvl in range(N_LEVELS):
        h = C >> (lvl + 1)
        mid = (t // (2 * h)) * (2 * h) + h
        if N_TOP <= lvl < N_LEVELS - 1:
            blocks.append(np.where(t >= mid, (u >= mid) & (u <= t), (u > t) & (u < mid)))
        mid_s = (u // (2 * h)) * (2 * h) + h
        masks.append((t // (2 * h) == u // (2 * h)) & (t >= mid) & (u < mid_s))
    masks.append(t == u)
    sm = np.concatenate(blocks, axis=0).astype(np.float32)
    mk = np.stack(masks, axis=0).astype(np.float32)
    return jnp.asarray(sm, BF16), jnp.asarray(mk, F32)


def _midpoint_rows(G, h):
    rows = [jnp.broadcast_to(G[b + h - 1:b + h, :], (2 * h, HEAD)) for b in range(0, CHUNK, 2 * h)]
    return rows[0] if len(rows) == 1 else jnp.concatenate(rows, axis=0)


def _hgrn_kernel(zq_ref, zf_ref, zi_ref, zg_ref, lb_ref, go_ref, s0_ref, sm_ref, mk_ref,
                 o_ref, sout_ref, st_ref, *, n_chunks, n_heads):
    C = CHUNK
    tb = pl.program_id(2)

    @pl.when(tb == 0)
    def _():
        for hh in range(n_heads):
            st_ref[hh] = s0_ref[hh].T

    odd_row = lax.broadcasted_iota(jnp.int32, (C, HEAD), 0) % 2 == 1

    def chunk(c, carry):
        r0 = pl.multiple_of(c * C, C)
        heads = range(n_heads)
        cols = [slice(hh * HEAD, (hh + 1) * HEAD) for hh in heads]
        q, k, v, f, d2 = [], [], [], [], []
        for hh in heads:
            lb = lb_ref[:, cols[hh]]
            zq = zq_ref[pl.ds(r0, C), cols[hh]]
            zf = zf_ref[pl.ds(r0, C), cols[hh]]
            q.append(zq * _sigmoid(zq))
            f.append(lb + (1.0 - lb) * _sigmoid(zf))
            g = jnp.log(f[hh])
            k.append(1.0 - f[hh])
            v.append(zi_ref[pl.ds(r0, C), cols[hh]])
            g_hi = g.astype(BF16)
            g_lo = (g - g_hi.astype(F32)).astype(BF16)
            d2.append(jnp.dot(sm_ref[...], jnp.concatenate([g_hi, g_lo], axis=1),
                              preferred_element_type=F32))
        e, lv = [], []
        for hh in heads:
            d = d2[hh][:, :HEAD] + d2[hh][:, HEAD:]
            e.append(jnp.exp(d))
            G = d[0:C]
            top = [jnp.exp(-jnp.abs(G - _midpoint_rows(G, C >> (lvl + 1)))) for lvl in range(N_TOP)]
            lv.append(top + [e[hh][2 * C:3 * C], e[hh][3 * C:4 * C],
                             jnp.where(odd_row, f[hh], 1.0)])
        a = []
        for hh in heads:
            parts = [lax.dot_general(q[hh].astype(BF16), k[hh].astype(BF16), _NT,
                                     preferred_element_type=F32)]
            for lvl in range(N_LEVELS):
                el = lv[hh][lvl]
                parts.append(lax.dot_general((q[hh] * el).astype(BF16), (k[hh] * el).astype(BF16),
                                             _NT, preferred_element_type=F32))
            acc = mk_ref[N_LEVELS] * parts[0]
            for lvl in range(N_LEVELS):
                acc = acc + mk_ref[lvl] * parts[1 + lvl]
            a.append(acc)
        o = []
        for hh in heads:
            st = st_ref[hh]
            qg = (q[hh] * e[hh][0:C]).astype(BF16)
            kg = (k[hh] * e[hh][C:2 * C]).astype(BF16)
            vb = v[hh].astype(BF16)
            oh = jnp.dot(a[hh].astype(BF16), vb, preferred_element_type=F32)
            o.append(oh + lax.dot_general(qg, st.astype(BF16), _NT, preferred_element_type=F32))
            st_ref[hh] = st * e[hh][C - 1:C, :] + jnp.dot(v[hh].T.astype(BF16), kg,
                                                          preferred_element_type=F32)
        for hh in heads:
            zg = zg_ref[pl.ds(r0, C), cols[hh]]
            ms = jnp.mean(o[hh] * o[hh], axis=-1, keepdims=True)
            og = o[hh] * lax.rsqrt(ms + EPS) * go_ref[:, cols[hh]] * (zg * _sigmoid(zg))
            o_ref[pl.ds(r0, C), cols[hh]] = og.astype(o_ref.dtype)
        return carry

    lax.fori_loop(0, n_chunks, chunk, 0)

    @pl.when(tb == pl.num_programs(2) - 1)
    def _():
        for hh in range(n_heads):
            sout_ref[hh] = st_ref[hh].T


def _hgrn_mix(z, lb, g_out, s0, B, T):
    D = z.shape[1] // 4
    H = D // HEAD
    hb = _pick(H, (HGRN_HEADS,))
    tb = _pick(T, (256, 128, 64))
    nt = T // tb
    nh = H // hb
    sm, mk = _hgrn_constants()

    def zspec(part):
        return pl.BlockSpec((tb, hb * HEAD), lambda b, h, t: (b * nt + t, part * nh + h))

    head_vec = pl.BlockSpec((1, hb * HEAD), lambda b, h, t: (0, h))
    state = pl.BlockSpec((None, hb, HEAD, HEAD), lambda b, h, t: (b, h, 0, 0))
    og, s_fin = pl.pallas_call(
        functools.partial(_hgrn_kernel, n_chunks=tb // CHUNK, n_heads=hb),
        grid=(B, nh, nt),
        in_specs=[zspec(0), zspec(1), zspec(2), zspec(3), head_vec, head_vec, state,
                  pl.BlockSpec(sm.shape, lambda b, h, t: (0, 0)),
                  pl.BlockSpec(mk.shape, lambda b, h, t: (0, 0, 0))],
        out_specs=[pl.BlockSpec((tb, hb * HEAD), lambda b, h, t: (b * nt + t, h)), state],
        out_shape=[jax.ShapeDtypeStruct((B * T, D), BF16),
                   jax.ShapeDtypeStruct((B, H, HEAD, HEAD), F32)],
        scratch_shapes=[pltpu.VMEM((hb, HEAD, HEAD), F32)],
        compiler_params=_params(("parallel", "parallel", "arbitrary")),
        name="hgrn_mix",
    )(z, z, z, z, lb.reshape(1, D), g_out.reshape(1, D).astype(F32), s0, sm, mk)
    return og, s_fin


def _out_ple_kernel(og_ref, x_ref, p_ref, wo_ref, gpg_ref, wpg_ref, wpin_ref, gfin_ref, o_ref,
                    *, final, col_chunk):
    D = x_ref.shape[1]
    y = x_ref[...] + jnp.dot(og_ref[...], wo_ref[...], preferred_element_type=F32)
    ms = jnp.mean(y * y, axis=-1, keepdims=True)
    yn = (y * lax.rsqrt(ms + EPS) * gpg_ref[...]).astype(BF16)
    pb = p_ref[...].astype(BF16)
    parts = []
    for c0 in range(0, D, col_chunk):
        gate = _sigmoid(jnp.dot(yn, wpg_ref[:, c0:c0 + col_chunk], preferred_element_type=F32))
        emb = jnp.dot(pb, wpin_ref[:, c0:c0 + col_chunk], preferred_element_type=F32)
        parts.append(y[:, c0:c0 + col_chunk] + gate * emb)
    if final:
        ms2 = sum(jnp.sum(t * t, axis=-1, keepdims=True) for t in parts) * (1.0 / D)
        inv = lax.rsqrt(ms2 + EPS)
        for n, c0 in enumerate(range(0, D, col_chunk)):
            o_ref[:, c0:c0 + col_chunk] = parts[n] * inv * gfin_ref[:, c0:c0 + col_chunk]
    else:
        for n, c0 in enumerate(range(0, D, col_chunk)):
            o_ref[:, c0:c0 + col_chunk] = parts[n]


def _out_ple(og, x, p, layer, w_out, g_pg, w_pg, w_pin, g_final, final):
    T, D = x.shape
    P = p.shape[2]
    tm = _pick(T, (256, 128, 64))
    col_chunk = min(D, 512)
    rows = lambda i: (i, 0)
    fixed = lambda i: (0, 0)
    once = pl.Buffered(1)
    return pl.pallas_call(
        functools.partial(_out_ple_kernel, final=final, col_chunk=col_chunk),
        grid=(T // tm,),
        in_specs=[pl.BlockSpec((tm, D), rows), pl.BlockSpec((tm, D), rows),
                  pl.BlockSpec((None, tm, P), lambda i: (layer, i, 0)),
                  pl.BlockSpec((D, D), fixed, pipeline_mode=once),
                  pl.BlockSpec((1, D), fixed),
                  pl.BlockSpec((D, D), fixed, pipeline_mode=once),
                  pl.BlockSpec((P, D), fixed, pipeline_mode=once),
                  pl.BlockSpec((1, D), fixed)],
        out_specs=pl.BlockSpec((tm, D), rows),
        out_shape=jax.ShapeDtypeStruct((T, D), F32),
        compiler_params=_params(("parallel",)),
        name="out_ple",
    )(og, x, p, w_out, g_pg.reshape(1, D).astype(F32), w_pg, w_pin,
      g_final.reshape(1, D).astype(F32))


def _cumsum_kernel(x_ref, u_ref, o_ref, carry_ref, *, n_sub):
    @pl.when(pl.program_id(0) == 0)
    def _():
        carry_ref[...] = jnp.zeros_like(carry_ref)

    u = u_ref[...]
    c = carry_ref[:, 0:1]
    for s in range(n_sub):
        x = x_ref[:, s * LANES:(s + 1) * LANES]
        hi = x.astype(BF16)
        r1 = x - hi.astype(F32)
        mid = r1.astype(BF16)
        lo = (r1 - mid.astype(F32)).astype(BF16)
        cs = (jnp.dot(hi, u, preferred_element_type=F32)
              + jnp.dot(mid, u, preferred_element_type=F32)
              + jnp.dot(lo, u, preferred_element_type=F32)) + c
        o_ref[:, s * LANES:(s + 1) * LANES] = cs
        c = cs[:, LANES - 1:LANES]
    carry_ref[...] = jnp.broadcast_to(c, carry_ref.shape)


def _cumsum_last(x):
    R, L = x.shape
    cb = _pick(L, (2048, 1024, 512, 256, 128))
    u = jnp.asarray(np.triu(np.ones((LANES, LANES), np.float32)), BF16)
    return pl.pallas_call(
        functools.partial(_cumsum_kernel, n_sub=cb // LANES),
        grid=(L // cb,),
        in_specs=[pl.BlockSpec((R, cb), lambda j: (0, j)),
                  pl.BlockSpec((LANES, LANES), lambda j: (0, 0))],
        out_specs=pl.BlockSpec((R, cb), lambda j: (0, j)),
        out_shape=jax.ShapeDtypeStruct((R, L), F32),
        scratch_shapes=[pltpu.VMEM((R, LANES), F32)],
        compiler_params=_params(("arbitrary",)),
        name="cumsum",
    )(x, u)


def _fox_kernel(q_ref, gate_ref, k_ref, v_ref, dq_ref, dk_ref, o_ref, *, tq, tk, q_off, scale):
    i = pl.program_id(2)
    q = q_ref[...].astype(BF16)
    dq = dq_ref[...]
    first_pos = q_off + i * tq
    n_full = (first_pos + 1) // tk
    n_all = (first_pos + tq - 1) // tk + 1

    def step(j, carry, masked):
        m, l, acc = carry
        start = pl.multiple_of(j * tk, tk)
        k = k_ref[pl.ds(start, tk), :]
        v = v_ref[pl.ds(start, tk), :]
        s = lax.dot_general(q, k, _NT, preferred_element_type=F32) * scale
        s = s + dq - dk_ref[:, pl.ds(start, tk)]
        if masked:
            kpos = start + lax.broadcasted_iota(jnp.int32, (tq, tk), 1)
            qpos = first_pos + lax.broadcasted_iota(jnp.int32, (tq, tk), 0)
            s = jnp.where(kpos <= qpos, s, MASK_VALUE)
        m_new = jnp.maximum(m, jnp.max(s, axis=-1, keepdims=True))
        alpha = jnp.exp(m - m_new)
        p = jnp.exp(s - m_new)
        l = alpha * l + jnp.sum(p, axis=-1, keepdims=True)
        acc = alpha * acc + jnp.dot(p.astype(BF16), v, preferred_element_type=F32)
        return m_new, l, acc

    init = (jnp.full((tq, 1), MASK_VALUE, F32), jnp.zeros((tq, 1), F32),
            jnp.zeros((tq, HEAD), F32))
    carry = lax.fori_loop(0, n_full, functools.partial(step, masked=False), init)
    _, l, acc = lax.fori_loop(n_full, n_all, functools.partial(step, masked=True), carry)
    gate = gate_ref[...]
    o = acc * (1.0 / l)
    o_ref[...] = (o * (gate * _sigmoid(gate))).astype(o_ref.dtype)


def _fox_short_kernel(q_ref, gate_ref, k_ref, v_ref, dq_ref, dk_ref, o_ref, *, n_heads, q_off,
                      scale):
    T, Lk = q_ref.shape[0], k_ref.shape[0]
    heads = range(n_heads)
    cols = [slice(hh * HEAD, (hh + 1) * HEAD) for hh in heads]
    visible = (lax.broadcasted_iota(jnp.int32, (T, Lk), 1)
               <= q_off + lax.broadcasted_iota(jnp.int32, (T, Lk), 0))
    s = []
    for hh in heads:
        sh = lax.dot_general(q_ref[:, cols[hh]].astype(BF16), k_ref[:, cols[hh]], _NT,
                             preferred_element_type=F32) * scale
        s.append(jnp.where(visible, sh + dq_ref[hh] - dk_ref[hh], MASK_VALUE))
    p = [jnp.exp(sh - jnp.max(sh, axis=-1, keepdims=True)) for sh in s]
    o = [jnp.dot(p[hh].astype(BF16), v_ref[:, cols[hh]], preferred_element_type=F32)
         for hh in heads]
    for hh in heads:
        gate = gate_ref[:, cols[hh]]
        oh = o[hh] * (1.0 / jnp.sum(p[hh], axis=-1, keepdims=True))
        o_ref[:, cols[hh]] = (oh * (gate * _sigmoid(gate))).astype(o_ref.dtype)


def _fox_mix_short(z, k_all, v_all, dq, dk, B, T, q_off):
    D = z.shape[1] // 2
    H = D // HEAD
    Lk = k_all.shape[1]
    hb = _pick(H, (FOX_SHORT_HEADS,))
    nh = H // hb
    kv = pl.BlockSpec((None, Lk, hb * HEAD), lambda b, h: (b, 0, h))
    return pl.pallas_call(
        functools.partial(_fox_short_kernel, n_heads=hb, q_off=q_off, scale=HEAD ** -0.5),
        grid=(B, nh),
        in_specs=[pl.BlockSpec((T, hb * HEAD), lambda b, h: (b, h)),
                  pl.BlockSpec((T, hb * HEAD), lambda b, h: (b, nh + h)),
                  kv, kv,
                  pl.BlockSpec((None, hb, T, 1), lambda b, h: (b, h, 0, 0)),
                  pl.BlockSpec((None, hb, 1, Lk), lambda b, h: (b, h, 0, 0))],
        out_specs=pl.BlockSpec((T, hb * HEAD), lambda b, h: (b, h)),
        out_shape=jax.ShapeDtypeStruct((B * T, D), BF16),
        compiler_params=_params(("parallel", "parallel")),
        name="fox_mix_short",
    )(z, z, k_all, v_all, dq, dk)


def _fox_mix(z, k_all, v_all, dq, dk, B, T, q_off):
    D = z.shape[1] // 2
    H = D // HEAD
    Lk = k_all.shape[1]
    if T <= FOX_SHORT_QUERIES and Lk <= FOX_SHORT_KEYS:
        return _fox_mix_short(z, k_all, v_all, dq, dk, B, T, q_off)
    tq = _pick(T, (512, 256, 128, 64))
    tk = Lk if Lk <= 2048 else _pick(Lk, (512, 256, 128))
    nq = T // tq
    kv = pl.BlockSpec((None, Lk, HEAD), lambda b, h, i: (b, 0, h))
    return pl.pallas_call(
        functools.partial(_fox_kernel, tq=tq, tk=tk, q_off=q_off, scale=HEAD ** -0.5),
        grid=(B, H, nq),
        in_specs=[pl.BlockSpec((tq, HEAD), lambda b, h, i: (b * nq + i, h)),
                  pl.BlockSpec((tq, HEAD), lambda b, h, i: (b * nq + i, H + h)),
                  kv, kv,
                  pl.BlockSpec((None, None, tq, 1), lambda b, h, i: (b, h, i, 0)),
                  pl.BlockSpec((None, None, 1, Lk), lambda b, h, i: (b, h, 0, 0))],
        out_specs=pl.BlockSpec((tq, HEAD), lambda b, h, i: (b * nq + i, h)),
        out_shape=jax.ShapeDtypeStruct((B * T, D), BF16),
        compiler_params=_params(("parallel", "parallel", "arbitrary")),
        name="fox_mix",
    )(z, z, k_all, v_all, dq, dk)


FOX_BOUND_MARGIN = 1.02
FOX_BOUND_LIMIT = 32.0
FOX_FAST_TILE = 1024
AUG = 6


def _top16(x):
    bits = lax.bitcast_convert_type(x, jnp.uint32) & jnp.uint32(0xFFFF0000)
    return lax.bitcast_convert_type(bits, F32)


def _split3(x):
    hi = _top16(x)
    r1 = x - hi
    mid = _top16(r1)
    lo = r1 - mid
    return hi.astype(BF16), mid.astype(BF16), lo.astype(BF16)


def _bias_terms(x, shape, axis, value_first):
    idx = lax.broadcasted_iota(jnp.int32, shape, axis)
    v0, o0 = (0, AUG // 2) if value_first else (AUG // 2, 0)
    hi, mid, lo = _split3(x)
    ones = (idx >= o0) & (idx < o0 + AUG // 2)
    return jnp.where(idx == v0, hi.astype(F32), jnp.where(idx == v0 + 1, mid.astype(F32),
           jnp.where(idx == v0 + 2, lo.astype(F32), jnp.where(ones, 1.0, 0.0))))


def _head_selector(D):
    e = np.zeros((LANES, D), np.float32)
    e[np.arange(D) // HEAD, np.arange(D)] = 1.0
    return jnp.asarray(e, BF16)


def _sq_sums_kernel(x_ref, e_ref, o_ref):
    x = x_ref[...]
    o_ref[...] = lax.dot_general(e_ref[...], (x * x).astype(BF16), _NT,
                                 preferred_element_type=F32)


def _head_sq_sums(x, D):
    T = x.shape[0]
    H = D // HEAD
    tm = _pick(T, (512, 256, 128, 64))
    return pl.pallas_call(
        _sq_sums_kernel,
        grid=(T // tm,),
        in_specs=[pl.BlockSpec((tm, D), lambda i: (i, 0)),
                  pl.BlockSpec((LANES, D), lambda i: (0, 0))],
        out_specs=pl.BlockSpec((LANES, tm), lambda i: (0, i)),
        out_shape=jax.ShapeDtypeStruct((LANES, T), F32),
        compiler_params=_params(("parallel",)),
        name="head_sq_sums",
    )(x, _head_selector(D))[:H]


def _fox_fast_kernel(q_ref, c_ref, gate_ref, ka_ref, vt_ref, o_ref, qa_ref, *, tile, scale):
    i = pl.program_id(2)
    qa_ref[:, :HEAD] = (q_ref[...] * scale).astype(BF16)
    qa_ref[:, HEAD:] = _bias_terms(c_ref[...], (HEAD, tile), 0, True).T.astype(BF16)

    def scores(j, masked):
        start = pl.multiple_of(j * tile, tile)
        s = lax.dot_general(ka_ref[pl.ds(start, tile), :], qa_ref[...], _NT,
                            preferred_element_type=F32)
        if masked:
            kpos = lax.broadcasted_iota(jnp.int32, (tile, tile), 0)
            qpos = lax.broadcasted_iota(jnp.int32, (tile, tile), 1)
            s = jnp.where(kpos <= qpos, s, MASK_VALUE)
        return s

    def consume(s, j, lp, acc):
        start = pl.multiple_of(j * tile, tile)
        p = jnp.exp(s)
        for g in range(tile // SUBLANES):
            lp = lp + p[g * SUBLANES:(g + 1) * SUBLANES, :]
        acc = acc + jnp.dot(vt_ref[:, pl.ds(start, tile)], p.astype(BF16),
                            preferred_element_type=F32)
        return lp, acc

    def pair(j0, carry, masked_b):
        lp, acc = carry
        sa = scores(j0, False)
        sb = scores(j0 + 1, masked_b)
        lp, acc = consume(sa, j0, lp, acc)
        return consume(sb, j0 + 1, lp, acc)

    init = (jnp.zeros((SUBLANES, tile), F32), jnp.zeros((HEAD, tile), F32))
    carry = lax.fori_loop(0, i // 2, lambda t, c: pair(2 * t, c, False), init)
    lp, acc = lax.cond(i % 2 == 1, lambda c: pair(i - 1, c, True),
                       lambda c: consume(scores(i, True), i, *c), carry)
    gate = gate_ref[...]
    o = (acc * (1.0 / jnp.sum(lp, axis=0, keepdims=True))).T
    o_ref[...] = (o * (gate * _sigmoid(gate))).astype(o_ref.dtype)


def _fox_mix_fast(z, c, k_aug, v_t, B, T):
    D = z.shape[1] // 2
    H = D // HEAD
    tile = FOX_FAST_TILE
    nq = T // tile
    return pl.pallas_call(
        functools.partial(_fox_fast_kernel, tile=tile, scale=HEAD ** -0.5),
        grid=(B, H, nq),
        in_specs=[pl.BlockSpec((tile, HEAD), lambda b, h, i: (b * nq + i, h)),
                  pl.BlockSpec((None, None, 1, tile), lambda b, h, i: (b, h, 0, i)),
                  pl.BlockSpec((tile, HEAD), lambda b, h, i: (b * nq + i, H + h)),
                  pl.BlockSpec((None, T, 2 * HEAD), lambda b, h, i: (b, 0, h)),
                  pl.BlockSpec((None, None, HEAD, T), lambda b, h, i: (b, h, 0, 0))],
        out_specs=pl.BlockSpec((tile, HEAD), lambda b, h, i: (b * nq + i, h)),
        out_shape=jax.ShapeDtypeStruct((B * T, D), BF16),
        scratch_shapes=[pltpu.VMEM((tile, 2 * HEAD), BF16)],
        compiler_params=_params(("parallel", "parallel", "arbitrary")),
        name="fox_mix_fast",
    )(z, c, z, k_aug, v_t)


def _augment_kernel(k_ref, dk_ref, e_ref, o_ref, sq_ref, *, n_heads):
    tm = k_ref.shape[0]
    k = k_ref[...]
    sq_ref[...] = lax.dot_general(e_ref[...], (k * k).astype(BF16), _NT,
                                  preferred_element_type=F32)
    for h in range(n_heads):
        o_ref[:, 2 * h * HEAD:(2 * h + 1) * HEAD] = k_ref[:, h * HEAD:(h + 1) * HEAD].astype(BF16)
        o_ref[:, (2 * h + 1) * HEAD:(2 * h + 2) * HEAD] = _bias_terms(
            -dk_ref[:, h:h + 1], (tm, HEAD), 1, False).astype(BF16)


def _augment_keys(k, dcum, B, Lk, H):
    D = H * HEAD
    T = B * Lk
    tm = _pick(T, (512, 256, 128, 64))
    dk_rows = jnp.transpose(dcum, (0, 2, 1)).reshape(T, H)
    out, sq = pl.pallas_call(
        functools.partial(_augment_kernel, n_heads=H),
        grid=(T // tm,),
        in_specs=[pl.BlockSpec((tm, D), lambda i: (i, 0)),
                  pl.BlockSpec((tm, H), lambda i: (i, 0)),
                  pl.BlockSpec((LANES, D), lambda i: (0, 0))],
        out_specs=[pl.BlockSpec((tm, 2 * D), lambda i: (i, 0)),
                   pl.BlockSpec((LANES, tm), lambda i: (0, i))],
        out_shape=[jax.ShapeDtypeStruct((T, 2 * D), BF16),
                   jax.ShapeDtypeStruct((LANES, T), F32)],
        compiler_params=_params(("parallel",)),
        name="augment_keys",
    )(k, dk_rows, _head_selector(D))
    return out.reshape(B, Lk, 2 * D), sq[:H]


def _trunk(x, p, s0, past, W):
    B, T, D = x.shape
    H = D // HEAD
    depth = p.shape[0]
    n_a = len(W["w_in_a"])
    xf = x.reshape(B * T, D)
    pf = p.reshape(depth, B * T, p.shape[-1])
    states = []
    for layer in range(depth):
        final = layer == depth - 1
        if layer < n_a:
            z = _norm_matmul(xf, W["g_norm_a"][layer], W["w_in_a"][layer])
            og, s_fin = _hgrn_mix(z, W["lbs"][layer], W["g_out_a"][layer], s0[layer], B, T)
            states.append(s_fin)
            w_out = W["w_out_a"][layer]
        else:
            j = layer - n_a
            z = _norm_matmul(xf, W["g_norm_b"][j], W["w_in_b"][j])
            if k_aug is None:
                og = _fox_mix(z, k_all, v_all, dq, dk, B, T, q_off)
            else:
                qn = jnp.transpose(jnp.sqrt(_head_sq_sums(z, D)).reshape(H, B, T), (1, 0, 2))
                bound = (FOX_BOUND_MARGIN * HEAD ** -0.5) * qn * kmax
                c = (dcum - bound)[:, :, None, :]
                og = lax.cond(jnp.max(bound) <= FOX_BOUND_LIMIT,
                              lambda: _fox_mix_fast(z, c, k_aug, v_t, B, T),
                              lambda: _fox_mix(z, k_new.reshape(B, T, D).astype(BF16), v_all, dq,
                                               dk, B, T, q_off))
            w_out = W["w_out_b"][j]
        xf = _out_ple(og, xf, pf, layer, w_out, W["g_ple"][layer], W["w_ple_gate"][layer],
                      W["w_ple_in"][layer], W["g_final"], final)
        if layer == n_a - 1:
            fast = past is None and T % FOX_FAST_TILE == 0
            if fast:
                k_new, v_new, v_bf, logf_new = _norm_matmul_kv(xf, W["g_kv"], W["w_kv"], W["w_f"],
                                                               W["b_f"])
                k_bf = None
            else:
                k_new, k_bf = _norm_matmul(xf, W["g_kv"], W["w_kv"][:, :D], with_bf16=True)
                v_new, v_bf = _norm_matmul(xf, W["g_kv"], W["w_kv"][:, D:], with_bf16=True)
                k_bf = k_bf.reshape(B, T, D)
                logf_new = _norm_matmul(xf, W["g_kv"], W["w_f"], bias=W["b_f"])
            logf_new = logf_new[:, :H].reshape(B, T, H)
            v_bf = v_bf.reshape(B, T, D)
            if past is None:
                k_all, v_all, logf_all, q_off = k_bf, v_bf, logf_new, 0
            else:
                past_k, past_v, past_logf = past
                q_off = past_k.shape[1]
                k_all = jnp.concatenate([past_k.reshape(B, q_off, D).astype(BF16), k_bf], axis=1)
                v_all = jnp.concatenate([past_v.reshape(B, q_off, D).astype(BF16), v_bf], axis=1)
                logf_all = jnp.concatenate([past_logf.astype(F32), logf_new], axis=1)
            Lk = v_all.shape[1]
            pad = (-Lk) % LANES
            if pad:
                k_all = jnp.pad(k_all, ((0, 0), (0, pad), (0, 0)))
                v_all = jnp.pad(v_all, ((0, 0), (0, pad), (0, 0)))
                logf_all = jnp.pad(logf_all, ((0, 0), (0, pad), (0, 0)))
            rows = jnp.transpose(logf_all, (0, 2, 1)).reshape(B * H, Lk + pad)
            dcum = _cumsum_last(rows).reshape(B, H, Lk + pad)
            dk = dcum[:, :, None, :]
            dq = dcum[:, :, q_off:q_off + T, None]
            k_aug = None
            if fast:
                k_aug, k_sq = _augment_keys(k_new, dcum, B, T, H)
                kn = jnp.sqrt(k_sq).reshape(H, B, T)
                kmax = jnp.transpose(jnp.max(kn, axis=2))[:, :, None]
                v_t = jnp.transpose(v_bf.reshape(B, T, H, HEAD), (0, 2, 3, 1))
    return (xf.reshape(B, T, D), jnp.stack(states), k_new.reshape(B, T, H, HEAD),
            v_new.reshape(B, T, H, HEAD), logf_new)


def kernel(x_prompt, x_sample, state_hgrn, cache_k, cache_v, cache_logf, p_prompt, p_sample,
           g_norm_a, w_in_a, lb_logits, g_out_a, w_out_a, g_kv, w_kv, b_f,
           g_norm_b, w_in_b, w_out_b, w_ple_in, g_ple, w_ple_gate, g_final):
    D = x_prompt.shape[-1]
    H = D // HEAD
    n_a = w_in_a.shape[0]
    lbs = jnp.cumsum(jax.nn.softmax(lb_logits.astype(F32), axis=0), axis=0)
    lbs = lbs - lbs[:1]
    w_f = jnp.pad(w_kv[:, 2 * D:], ((0, 0), (0, LANES - H)))
    def layers(w):
        return [w[n].astype(BF16) for n in range(w.shape[0])]

    W = dict(
        g_norm_a=g_norm_a, w_in_a=layers(w_in_a), lbs=lbs, g_out_a=g_out_a,
        w_out_a=layers(w_out_a), g_kv=g_kv, w_kv=w_kv[:, :2 * D].astype(BF16),
        w_f=w_f.astype(BF16),
        b_f=jnp.pad(b_f, (0, LANES - H)), g_norm_b=g_norm_b, w_in_b=layers(w_in_b),
        w_out_b=layers(w_out_b), w_ple_in=layers(w_ple_in), g_ple=g_ple,
        w_ple_gate=layers(w_ple_gate), g_final=g_final)
    s0_prompt = jnp.zeros((n_a, x_prompt.shape[0], H, HEAD, HEAD), F32)
    y_p, st_p, k_p, v_p, f_p = _trunk(x_prompt, p_prompt, s0_prompt, None, W)
    y_s, st_s, k_s, v_s, f_s = _trunk(x_sample, p_sample, state_hgrn,
                                      (cache_k, cache_v, cache_logf), W)
    return (y_p, y_s, st_p, st_s, k_p, v_p, f_p, k_s, v_s, f_s)
```

```python
import functools

import jax
import jax.numpy as jnp
import numpy as np
from jax import lax
from jax.experimental import pallas as pl
from jax.experimental.pallas import tpu as pltpu

F32 = jnp.float32
BF16 = jnp.bfloat16

EPS = 1e-6
HEAD = 128
LANES = 128
SUBLANES = 8
CHUNK = 64
N_LEVELS = 6
N_TOP = 3
HGRN_HEADS = 16
FOX_SHORT_HEADS = 4
FOX_SHORT_QUERIES = 128
FOX_SHORT_KEYS = 2048
MASK_VALUE = -1e30
VMEM_LIMIT = 56 * 1024 * 1024

_NT = (((1,), (1,)), ((), ()))


def _params(sem, vmem=VMEM_LIMIT):
    return pltpu.CompilerParams(dimension_semantics=sem, vmem_limit_bytes=vmem)


def _sigmoid(x):
    return 1.0 / (1.0 + jnp.exp(-x))


def _pick(n, prefs):
    for p in prefs:
        if n % p == 0:
            return p
    return n


def _nmm_kernel(*refs, log_sigmoid, with_bf16, row_chunk):
    if log_sigmoid:
        x_ref, g_ref, w_ref, b_ref = refs[:4]
        rest = refs[4:]
    else:
        x_ref, g_ref, w_ref = refs[:3]
        b_ref = None
        rest = refs[3:]
    if with_bf16:
        o_ref, ob_ref, xn_ref = rest
    else:
        o_ref, xn_ref = rest
        ob_ref = None
    tm = x_ref.shape[0]

    @pl.when(pl.program_id(1) == 0)
    def _():
        for r in range(0, tm, row_chunk):
            x = x_ref[r:r + row_chunk, :]
            ms = jnp.mean(x * x, axis=-1, keepdims=True)
            xn_ref[r:r + row_chunk, :] = (x * lax.rsqrt(ms + EPS) * g_ref[...]).astype(BF16)

    z = jnp.dot(xn_ref[...], w_ref[...], preferred_element_type=F32)
    if log_sigmoid:
        z = z + b_ref[...]
        z = jnp.minimum(z, 0.0) - jnp.log(1.0 + jnp.exp(-jnp.abs(z)))
    o_ref[...] = z
    if with_bf16:
        ob_ref[...] = z.astype(BF16)


def _norm_matmul(x, g, w, bias=None, with_bf16=False):
    T, D = x.shape
    N = w.shape[1]
    tm = _pick(T, (1024, 512, 256, 128, 64))
    tn = _pick(N, (1024, 512, 256, 128))
    row_chunk = min(tm, 256)
    in_specs = [
        pl.BlockSpec((tm, D), lambda i, j: (i, 0)),
        pl.BlockSpec((1, D), lambda i, j: (0, 0)),
        pl.BlockSpec((D, tn), lambda i, j: (0, j)),
    ]
    args = [x, g.reshape(1, D).astype(F32), w]
    if bias is not None:
        in_specs.append(pl.BlockSpec((1, tn), lambda i, j: (0, j)))
        args.append(bias.reshape(1, N).astype(F32))
    out_shape = [jax.ShapeDtypeStruct((T, N), F32)]
    out_specs = [pl.BlockSpec((tm, tn), lambda i, j: (i, j))]
    if with_bf16:
        out_shape.append(jax.ShapeDtypeStruct((T, N), BF16))
        out_specs.append(pl.BlockSpec((tm, tn), lambda i, j: (i, j)))
    out = pl.pallas_call(
        functools.partial(_nmm_kernel, log_sigmoid=bias is not None, with_bf16=with_bf16,
                          row_chunk=row_chunk),
        grid=(T // tm, N // tn),
        in_specs=in_specs,
        out_specs=out_specs,
        out_shape=out_shape,
        scratch_shapes=[pltpu.VMEM((tm, D), BF16)],
        compiler_params=_params(("parallel", "arbitrary")),
        name="norm_matmul",
    )(*args)
    return out if with_bf16 else out[0]


def _nmm_kv_kernel(x_ref, g_ref, w_ref, wf_ref, bf_ref, k_ref, v_ref, vb_ref, f_ref, xn_ref, *,
                   half, row_chunk):
    tm = x_ref.shape[0]
    j = pl.program_id(1)

    @pl.when(j == 0)
    def _():
        for r in range(0, tm, row_chunk):
            x = x_ref[r:r + row_chunk, :]
            ms = jnp.mean(x * x, axis=-1, keepdims=True)
            xn_ref[r:r + row_chunk, :] = (x * lax.rsqrt(ms + EPS) * g_ref[...]).astype(BF16)

    z = jnp.dot(xn_ref[...], w_ref[...], preferred_element_type=F32)

    @pl.when(j < half)
    def _():
        k_ref[...] = z.reshape(k_ref.shape)

    @pl.when(j >= half)
    def _():
        v_ref[...] = z.reshape(v_ref.shape)
        vb_ref[...] = z.astype(BF16)

    @pl.when(j == 2 * half - 1)
    def _():
        t = jnp.dot(xn_ref[...], wf_ref[...], preferred_element_type=F32) + bf_ref[...]
        f_ref[...] = jnp.minimum(t, 0.0) - jnp.log(1.0 + jnp.exp(-jnp.abs(t)))


def _norm_matmul_kv(x, g, w_kv, w_f, b_f):
    T, D = x.shape
    N = w_kv.shape[1] // 2
    tm = _pick(T, (1024, 512, 256, 128, 64))
    tn = _pick(N, (1024, 512, 256, 128))
    half = N // tn
    hn = tn // HEAD
    v_cols = lambda i, j: (i, jnp.maximum(j - half, 0))
    k_heads = lambda i, j: (i, jnp.minimum(j, half - 1), 0)
    v_heads = lambda i, j: (i, jnp.maximum(j - half, 0), 0)
    return pl.pallas_call(
        functools.partial(_nmm_kv_kernel, half=half, row_chunk=min(tm, 256)),
        grid=(T // tm, 2 * half),
        in_specs=[pl.BlockSpec((tm, D), lambda i, j: (i, 0)),
                  pl.BlockSpec((1, D), lambda i, j: (0, 0)),
                  pl.BlockSpec((D, tn), lambda i, j: (0, j)),
                  pl.BlockSpec((D, LANES), lambda i, j: (0, 0)),
                  pl.BlockSpec((1, LANES), lambda i, j: (0, 0))],
        out_specs=[pl.BlockSpec((tm, hn, HEAD), k_heads), pl.BlockSpec((tm, hn, HEAD), v_heads),
                   pl.BlockSpec((tm, tn), v_cols), pl.BlockSpec((tm, LANES), lambda i, j: (i, 0))],
        out_shape=[jax.ShapeDtypeStruct((T, N // HEAD, HEAD), F32),
                   jax.ShapeDtypeStruct((T, N // HEAD, HEAD), F32),
                   jax.ShapeDtypeStruct((T, N), BF16), jax.ShapeDtypeStruct((T, LANES), F32)],
        scratch_shapes=[pltpu.VMEM((tm, D), BF16)],
        compiler_params=_params(("parallel", "arbitrary")),
        name="norm_matmul_kv",
    )(x, g.reshape(1, D).astype(F32), w_kv, w_f, b_f.reshape(1, LANES).astype(F32))


def _hgrn_constants():
    C = CHUNK
    t = np.arange(C)[:, None]
    u = np.arange(C)[None, :]
    blocks = [(u <= t), (u > t)]
    masks = []
    for lvl in range(N_LEVELS):
        h = C >> (lvl + 1)
        mid = (t // (2 * h)) * (2 * h) + h
        if N_TOP <= lvl < N_LEVELS - 1:
            blocks.append(np.where(t >= mid, (u >= mid) & (u <= t), (u > t) & (u < mid)))
        mid_s = (u // (2 * h)) * (2 * h) + h
        masks.append((t // (2 * h) == u // (2 * h)) & (t >= mid) & (u < mid_s))
    masks.append(t == u)
    sm = np.concatenate(blocks, axis=0).astype(np.float32)
    mk = np.stack(masks, axis=0).astype(np.float32)
    return jnp.asarray(sm, BF16), jnp.asarray(mk, F32)


def _midpoint_rows(G, h):
    rows = [jnp.broadcast_to(G[b + h - 1:b + h, :], (2 * h, HEAD)) for b in range(0, CHUNK, 2 * h)]
    return rows[0] if len(rows) == 1 else jnp.concatenate(rows, axis=0)


def _hgrn_kernel(zq_ref, zf_ref, zi_ref, zg_ref, lb_ref, go_ref, s0_ref, sm_ref, mk_ref,
                 o_ref, sout_ref, st_ref, *, n_chunks, n_heads):
    C = CHUNK
    tb = pl.program_id(2)

    @pl.when(tb == 0)
    def _():
        for hh in range(n_heads):
            st_ref[hh] = s0_ref[hh].T

    odd_row = lax.broadcasted_iota(jnp.int32, (C, HEAD), 0) % 2 == 1

    def chunk(c, carry):
        r0 = pl.multiple_of(c * C, C)
        heads = range(n_heads)
        cols = [slice(hh * HEAD, (hh + 1) * HEAD) for hh in heads]
        q, k, v, f, d2 = [], [], [], [], []
        for hh in heads:
            lb = lb_ref[:, cols[hh]]
            zq = zq_ref[pl.ds(r0, C), cols[hh]]
            zf = zf_ref[pl.ds(r0, C), cols[hh]]
            q.append(zq * _sigmoid(zq))
            f.append(lb + (1.0 - lb) * _sigmoid(zf))
            g = jnp.log(f[hh])
            k.append(1.0 - f[hh])
            v.append(zi_ref[pl.ds(r0, C), cols[hh]])
            g_hi = g.astype(BF16)
            g_lo = (g - g_hi.astype(F32)).astype(BF16)
            d2.append(jnp.dot(sm_ref[...], jnp.concatenate([g_hi, g_lo], axis=1),
                              preferred_element_type=F32))
        e, lv = [], []
        for hh in heads:
            d = d2[hh][:, :HEAD] + d2[hh][:, HEAD:]
            e.append(jnp.exp(d))
            G = d[0:C]
            top = [jnp.exp(-jnp.abs(G - _midpoint_rows(G, C >> (lvl + 1)))) for lvl in range(N_TOP)]
            lv.append(top + [e[hh][2 * C:3 * C], e[hh][3 * C:4 * C],
                             jnp.where(odd_row, f[hh], 1.0)])
        a = []
        for hh in heads:
            parts = [lax.dot_general(q[hh].astype(BF16), k[hh].astype(BF16), _NT,
                                     preferred_element_type=F32)]
            for lvl in range(N_LEVELS):
                el = lv[hh][lvl]
                parts.append(lax.dot_general((q[hh] * el).astype(BF16), (k[hh] * el).astype(BF16),
                                             _NT, preferred_element_type=F32))
            acc = mk_ref[N_LEVELS] * parts[0]
            for lvl in range(N_LEVELS):
                acc = acc + mk_ref[lvl] * parts[1 + lvl]
            a.append(acc)
        o = []
        for hh in heads:
            st = st_ref[hh]
            qg = (q[hh] * e[hh][0:C]).astype(BF16)
            kg = (k[hh] * e[hh][C:2 * C]).astype(BF16)
            vb = v[hh].astype(BF16)
            oh = jnp.dot(a[hh].astype(BF16), vb, preferred_element_type=F32)
            o.append(oh + lax.dot_general(qg, st.astype(BF16), _NT, preferred_element_type=F32))
            st_ref[hh] = st * e[hh][C - 1:C, :] + jnp.dot(v[hh].T.astype(BF16), kg,
                                                          preferred_element_type=F32)
        for hh in heads:
            zg = zg_ref[pl.ds(r0, C), cols[hh]]
            ms = jnp.mean(o[hh] * o[hh], axis=-1, keepdims=True)
            og = o[hh] * lax.rsqrt(ms + EPS) * go_ref[:, cols[hh]] * (zg * _sigmoid(zg))
            o_ref[pl.ds(r0, C), cols[hh]] = og.astype(o_ref.dtype)
        return carry

    lax.fori_loop(0, n_chunks, chunk, 0)

    @pl.when(tb == pl.num_programs(2) - 1)
    def _():
        for hh in range(n_heads):
            sout_ref[hh] = st_ref[hh].T


def _hgrn_mix(z, lb, g_out, s0, B, T):
    D = z.shape[1] // 4
    H = D // HEAD
    hb = _pick(H, (HGRN_HEADS,))
    tb = _pick(T, (256, 128, 64))
    nt = T // tb
    nh = H // hb
    sm, mk = _hgrn_constants()

    def zspec(part):
        return pl.BlockSpec((tb, hb * HEAD), lambda b, h, t: (b * nt + t, part * nh + h))

    head_vec = pl.BlockSpec((1, hb * HEAD), lambda b, h, t: (0, h))
    state = pl.BlockSpec((None, hb, HEAD, HEAD), lambda b, h, t: (b, h, 0, 0))
    og, s_fin = pl.pallas_call(
        functools.partial(_hgrn_kernel, n_chunks=tb // CHUNK, n_heads=hb),
        grid=(B, nh, nt),
        in_specs=[zspec(0), zspec(1), zspec(2), zspec(3), head_vec, head_vec, state,
                  pl.BlockSpec(sm.shape, lambda b, h, t: (0, 0)),
                  pl.BlockSpec(mk.shape, lambda b, h, t: (0, 0, 0))],
        out_specs=[pl.BlockSpec((tb, hb * HEAD), lambda b, h, t: (b * nt + t, h)), state],
        out_shape=[jax.ShapeDtypeStruct((B * T, D), BF16),
                   jax.ShapeDtypeStruct((B, H, HEAD, HEAD), F32)],
        scratch_shapes=[pltpu.VMEM((hb, HEAD, HEAD), F32)],
        compiler_params=_params(("parallel", "parallel", "arbitrary")),
        name="hgrn_mix",
    )(z, z, z, z, lb.reshape(1, D), g_out.reshape(1, D).astype(F32), s0, sm, mk)
    return og, s_fin


def _out_ple_kernel(og_ref, x_ref, p_ref, wo_ref, gpg_ref, wpg_ref, wpin_ref, gfin_ref, o_ref,
                    *, final, col_chunk):
    D = x_ref.shape[1]
    y = x_ref[...] + jnp.dot(og_ref[...], wo_ref[...], preferred_element_type=F32)
    ms = jnp.mean(y * y, axis=-1, keepdims=True)
    yn = (y * lax.rsqrt(ms + EPS) * gpg_ref[...]).astype(BF16)
    pb = p_ref[...].astype(BF16)
    parts = []
    for c0 in range(0, D, col_chunk):
        gate = _sigmoid(jnp.dot(yn, wpg_ref[:, c0:c0 + col_chunk], preferred_element_type=F32))
        emb = jnp.dot(pb, wpin_ref[:, c0:c0 + col_chunk], preferred_element_type=F32)
        parts.append(y[:, c0:c0 + col_chunk] + gate * emb)
    if final:
        ms2 = sum(jnp.sum(t * t, axis=-1, keepdims=True) for t in parts) * (1.0 / D)
        inv = lax.rsqrt(ms2 + EPS)
        for n, c0 in enumerate(range(0, D, col_chunk)):
            o_ref[:, c0:c0 + col_chunk] = parts[n] * inv * gfin_ref[:, c0:c0 + col_chunk]
    else:
        for n, c0 in enumerate(range(0, D, col_chunk)):
            o_ref[:, c0:c0 + col_chunk] = parts[n]


def _out_ple(og, x, p, layer, w_out, g_pg, w_pg, w_pin, g_final, final):
    T, D = x.shape
    P = p.shape[2]
    tm = _pick(T, (256, 128, 64))
    col_chunk = min(D, 512)
    rows = lambda i: (i, 0)
    fixed = lambda i: (0, 0)
    once = pl.Buffered(1)
    return pl.pallas_call(
        functools.partial(_out_ple_kernel, final=final, col_chunk=col_chunk),
        grid=(T // tm,),
        in_specs=[pl.BlockSpec((tm, D), rows), pl.BlockSpec((tm, D), rows),
                  pl.BlockSpec((None, tm, P), lambda i: (layer, i, 0)),
                  pl.BlockSpec((D, D), fixed, pipeline_mode=once),
                  pl.BlockSpec((1, D), fixed),
                  pl.BlockSpec((D, D), fixed, pipeline_mode=once),
                  pl.BlockSpec((P, D), fixed, pipeline_mode=once),
                  pl.BlockSpec((1, D), fixed)],
        out_specs=pl.BlockSpec((tm, D), rows),
        out_shape=jax.ShapeDtypeStruct((T, D), F32),
        compiler_params=_params(("parallel",)),
        name="out_ple",
    )(og, x, p, w_out, g_pg.reshape(1, D).astype(F32), w_pg, w_pin,
      g_final.reshape(1, D).astype(F32))


def _cumsum_kernel(x_ref, u_ref, o_ref, carry_ref, *, n_sub):
    @pl.when(pl.program_id(0) == 0)
    def _():
        carry_ref[...] = jnp.zeros_like(carry_ref)

    u = u_ref[...]
    c = carry_ref[:, 0:1]
    for s in range(n_sub):
        x = x_ref[:, s * LANES:(s + 1) * LANES]
        hi = x.astype(BF16)
        r1 = x - hi.astype(F32)
        mid = r1.astype(BF16)
        lo = (r1 - mid.astype(F32)).astype(BF16)
        cs = (jnp.dot(hi, u, preferred_element_type=F32)
              + jnp.dot(mid, u, preferred_element_type=F32)
              + jnp.dot(lo, u, preferred_element_type=F32)) + c
        o_ref[:, s * LANES:(s + 1) * LANES] = cs
        c = cs[:, LANES - 1:LANES]
    carry_ref[...] = jnp.broadcast_to(c, carry_ref.shape)


def _cumsum_last(x):
    R, L = x.shape
    cb = _pick(L, (2048, 1024, 512, 256, 128))
    u = jnp.asarray(np.triu(np.ones((LANES, LANES), np.float32)), BF16)
    return pl.pallas_call(
        functools.partial(_cumsum_kernel, n_sub=cb // LANES),
        grid=(L // cb,),
        in_specs=[pl.BlockSpec((R, cb), lambda j: (0, j)),
                  pl.BlockSpec((LANES, LANES), lambda j: (0, 0))],
        out_specs=pl.BlockSpec((R, cb), lambda j: (0, j)),
        out_shape=jax.ShapeDtypeStruct((R, L), F32),
        scratch_shapes=[pltpu.VMEM((R, LANES), F32)],
        compiler_params=_params(("arbitrary",)),
        name="cumsum",
    )(x, u)


def _fox_kernel(q_ref, gate_ref, k_ref, v_ref, dq_ref, dk_ref, o_ref, *, tq, tk, q_off, scale):
    i = pl.program_id(2)
    q = q_ref[...].astype(BF16)
    dq = dq_ref[...]
    first_pos = q_off + i * tq
    n_full = (first_pos + 1) // tk
    n_all = (first_pos + tq - 1) // tk + 1

    def step(j, carry, masked):
        m, l, acc = carry
        start = pl.multiple_of(j * tk, tk)
        k = k_ref[pl.ds(start, tk), :]
        v = v_ref[pl.ds(start, tk), :]
        s = lax.dot_general(q, k, _NT, preferred_element_type=F32) * scale
        s = s + dq - dk_ref[:, pl.ds(start, tk)]
        if masked:
            kpos = start + lax.broadcasted_iota(jnp.int32, (tq, tk), 1)
            qpos = first_pos + lax.broadcasted_iota(jnp.int32, (tq, tk), 0)
            s = jnp.where(kpos <= qpos, s, MASK_VALUE)
        m_new = jnp.maximum(m, jnp.max(s, axis=-1, keepdims=True))
        alpha = jnp.exp(m - m_new)
        p = jnp.exp(s - m_new)
        l = alpha * l + jnp.sum(p, axis=-1, keepdims=True)
        acc = alpha * acc + jnp.dot(p.astype(BF16), v, preferred_element_type=F32)
        return m_new, l, acc

    init = (jnp.full((tq, 1), MASK_VALUE, F32), jnp.zeros((tq, 1), F32),
            jnp.zeros((tq, HEAD), F32))
    carry = lax.fori_loop(0, n_full, functools.partial(step, masked=False), init)
    _, l, acc = lax.fori_loop(n_full, n_all, functools.partial(step, masked=True), carry)
    gate = gate_ref[...]
    o = acc * (1.0 / l)
    o_ref[...] = (o * (gate * _sigmoid(gate))).astype(o_ref.dtype)


def _fox_short_kernel(q_ref, gate_ref, k_ref, v_ref, dq_ref, dk_ref, o_ref, *, n_heads, q_off,
                      scale):
    T, Lk = q_ref.shape[0], k_ref.shape[0]
    heads = range(n_heads)
    cols = [slice(hh * HEAD, (hh + 1) * HEAD) for hh in heads]
    visible = (lax.broadcasted_iota(jnp.int32, (T, Lk), 1)
               <= q_off + lax.broadcasted_iota(jnp.int32, (T, Lk), 0))
    s = []
    for hh in heads:
        sh = lax.dot_general(q_ref[:, cols[hh]].astype(BF16), k_ref[:, cols[hh]], _NT,
                             preferred_element_type=F32) * scale
        s.append(jnp.where(visible, sh + dq_ref[hh] - dk_ref[hh], MASK_VALUE))
    p = [jnp.exp(sh - jnp.max(sh, axis=-1, keepdims=True)) for sh in s]
    o = [jnp.dot(p[hh].astype(BF16), v_ref[:, cols[hh]], preferred_element_type=F32)
         for hh in heads]
    for hh in heads:
        gate = gate_ref[:, cols[hh]]
        oh = o[hh] * (1.0 / jnp.sum(p[hh], axis=-1, keepdims=True))
        o_ref[:, cols[hh]] = (oh * (gate * _sigmoid(gate))).astype(o_ref.dtype)


def _fox_mix_short(z, k_all, v_all, dq, dk, B, T, q_off):
    D = z.shape[1] // 2
    H = D // HEAD
    Lk = k_all.shape[1]
    hb = _pick(H, (FOX_SHORT_HEADS,))
    nh = H // hb
    kv = pl.BlockSpec((None, Lk, hb * HEAD), lambda b, h: (b, 0, h))
    return pl.pallas_call(
        functools.partial(_fox_short_kernel, n_heads=hb, q_off=q_off, scale=HEAD ** -0.5),
        grid=(B, nh),
        in_specs=[pl.BlockSpec((T, hb * HEAD), lambda b, h: (b, h)),
                  pl.BlockSpec((T, hb * HEAD), lambda b, h: (b, nh + h)),
                  kv, kv,
                  pl.BlockSpec((None, hb, T, 1), lambda b, h: (b, h, 0, 0)),
                  pl.BlockSpec((None, hb, 1, Lk), lambda b, h: (b, h, 0, 0))],
        out_specs=pl.BlockSpec((T, hb * HEAD), lambda b, h: (b, h)),
        out_shape=jax.ShapeDtypeStruct((B * T, D), BF16),
        compiler_params=_params(("parallel", "parallel")),
        name="fox_mix_short",
    )(z, z, k_all, v_all, dq, dk)


def _fox_mix(z, k_all, v_all, dq, dk, B, T, q_off):
    D = z.shape[1] // 2
    H = D // HEAD
    Lk = k_all.shape[1]
    if T <= FOX_SHORT_QUERIES and Lk <= FOX_SHORT_KEYS:
        return _fox_mix_short(z, k_all, v_all, dq, dk, B, T, q_off)
    tq = _pick(T, (512, 256, 128, 64))
    tk = Lk if Lk <= 2048 else _pick(Lk, (512, 256, 128))
    nq = T // tq
    kv = pl.BlockSpec((None, Lk, HEAD), lambda b, h, i: (b, 0, h))
    return pl.pallas_call(
        functools.partial(_fox_kernel, tq=tq, tk=tk, q_off=q_off, scale=HEAD ** -0.5),
        grid=(B, H, nq),
        in_specs=[pl.BlockSpec((tq, HEAD), lambda b, h, i: (b * nq + i, h)),
                  pl.BlockSpec((tq, HEAD), lambda b, h, i: (b * nq + i, H + h)),
                  kv, kv,
                  pl.BlockSpec((None, None, tq, 1), lambda b, h, i: (b, h, i, 0)),
                  pl.BlockSpec((None, None, 1, Lk), lambda b, h, i: (b, h, 0, 0))],
        out_specs=pl.BlockSpec((tq, HEAD), lambda b, h, i: (b * nq + i, h)),
        out_shape=jax.ShapeDtypeStruct((B * T, D), BF16),
        compiler_params=_params(("parallel", "parallel", "arbitrary")),
        name="fox_mix",
    )(z, z, k_all, v_all, dq, dk)


FOX_BOUND_MARGIN = 1.02
FOX_BOUND_LIMIT = 32.0
FOX_FAST_TILE = 1024
AUG = 6


def _top16(x):
    bits = lax.bitcast_convert_type(x, jnp.uint32) & jnp.uint32(0xFFFF0000)
    return lax.bitcast_convert_type(bits, F32)


def _split3(x):
    hi = _top16(x)
    r1 = x - hi
    mid = _top16(r1)
    lo = r1 - mid
    return hi.astype(BF16), mid.astype(BF16), lo.astype(BF16)


def _bias_terms(x, shape, axis, value_first):
    idx = lax.broadcasted_iota(jnp.int32, shape, axis)
    v0, o0 = (0, AUG // 2) if value_first else (AUG // 2, 0)
    hi, mid, lo = _split3(x)
    ones = (idx >= o0) & (idx < o0 + AUG // 2)
    return jnp.where(idx == v0, hi.astype(F32), jnp.where(idx == v0 + 1, mid.astype(F32),
           jnp.where(idx == v0 + 2, lo.astype(F32), jnp.where(ones, 1.0, 0.0))))


def _head_selector(D):
    e = np.zeros((LANES, D), np.float32)
    e[np.arange(D) // HEAD, np.arange(D)] = 1.0
    return jnp.asarray(e, BF16)


def _sq_sums_kernel(x_ref, e_ref, o_ref):
    x = x_ref[...]
    o_ref[...] = lax.dot_general(e_ref[...], (x * x).astype(BF16), _NT,
                                 preferred_element_type=F32)


def _head_sq_sums(x, D):
    T = x.shape[0]
    H = D // HEAD
    tm = _pick(T, (512, 256, 128, 64))
    return pl.pallas_call(
        _sq_sums_kernel,
        grid=(T // tm,),
        in_specs=[pl.BlockSpec((tm, D), lambda i: (i, 0)),
                  pl.BlockSpec((LANES, D), lambda i: (0, 0))],
        out_specs=pl.BlockSpec((LANES, tm), lambda i: (0, i)),
        out_shape=jax.ShapeDtypeStruct((LANES, T), F32),
        compiler_params=_params(("parallel",)),
        name="head_sq_sums",
    )(x, _head_selector(D))[:H]


def _fox_fast_kernel(q_ref, c_ref, gate_ref, ka_ref, vt_ref, o_ref, qa_ref, *, tile, scale):
    i = pl.program_id(2)
    qa_ref[:, :HEAD] = (q_ref[...] * scale).astype(BF16)
    qa_ref[:, HEAD:] = _bias_terms(c_ref[...], (HEAD, tile), 0, True).T.astype(BF16)

    def scores(j, masked):
        start = pl.multiple_of(j * tile, tile)
        s = lax.dot_general(ka_ref[pl.ds(start, tile), :], qa_ref[...], _NT,
                            preferred_element_type=F32)
        if masked:
            kpos = lax.broadcasted_iota(jnp.int32, (tile, tile), 0)
            qpos = lax.broadcasted_iota(jnp.int32, (tile, tile), 1)
            s = jnp.where(kpos <= qpos, s, MASK_VALUE)
        return s

    def consume(s, j, lp, acc):
        start = pl.multiple_of(j * tile, tile)
        p = jnp.exp(s)
        for g in range(tile // SUBLANES):
            lp = lp + p[g * SUBLANES:(g + 1) * SUBLANES, :]
        acc = acc + jnp.dot(vt_ref[:, pl.ds(start, tile)], p.astype(BF16),
                            preferred_element_type=F32)
        return lp, acc

    def pair(j0, carry, masked_b):
        lp, acc = carry
        sa = scores(j0, False)
        sb = scores(j0 + 1, masked_b)
        lp, acc = consume(sa, j0, lp, acc)
        return consume(sb, j0 + 1, lp, acc)

    def quad(j0, carry):
        lp, acc = carry
        s0 = scores(j0, False)
        s1 = scores(j0 + 1, False)
        lp, acc = consume(s0, j0, lp, acc)
        s2 = scores(j0 + 2, False)
        lp, acc = consume(s1, j0 + 1, lp, acc)
        s3 = scores(j0 + 3, False)
        lp, acc = consume(s2, j0 + 2, lp, acc)
        return consume(s3, j0 + 3, lp, acc)

    init = (jnp.zeros((SUBLANES, tile), F32), jnp.zeros((HEAD, tile), F32))
    carry = lax.fori_loop(0, i // 4, lambda t, c: quad(4 * t, c), init)
    carry = lax.fori_loop(2 * (i // 4), i // 2, lambda t, c: pair(2 * t, c, False), carry)
    lp, acc = lax.cond(i % 2 == 1, lambda c: pair(i - 1, c, True),
                       lambda c: consume(scores(i, True), i, *c), carry)
    gate = gate_ref[...]
    o = (acc * (1.0 / jnp.sum(lp, axis=0, keepdims=True))).T
    o_ref[...] = (o * (gate * _sigmoid(gate))).astype(o_ref.dtype)


def _fox_mix_fast(z, c, k_aug, v_t, B, T):
    D = z.shape[1] // 2
    H = D // HEAD
    tile = FOX_FAST_TILE
    nq = T // tile
    return pl.pallas_call(
        functools.partial(_fox_fast_kernel, tile=tile, scale=HEAD ** -0.5),
        grid=(B, H, nq),
        in_specs=[pl.BlockSpec((tile, HEAD), lambda b, h, i: (b * nq + i, h)),
                  pl.BlockSpec((None, None, 1, tile), lambda b, h, i: (b, h, 0, i)),
                  pl.BlockSpec((tile, HEAD), lambda b, h, i: (b * nq + i, H + h)),
                  pl.BlockSpec((None, T, 2 * HEAD), lambda b, h, i: (b, 0, h)),
                  pl.BlockSpec((None, None, HEAD, T), lambda b, h, i: (b, h, 0, 0))],
        out_specs=pl.BlockSpec((tile, HEAD), lambda b, h, i: (b * nq + i, h)),
        out_shape=jax.ShapeDtypeStruct((B * T, D), BF16),
        scratch_shapes=[pltpu.VMEM((tile, 2 * HEAD), BF16)],
        compiler_params=_params(("parallel", "parallel", "arbitrary")),
        name="fox_mix_fast",
    )(z, c, z, k_aug, v_t)


def _augment_kernel(k_ref, dk_ref, e_ref, o_ref, sq_ref, *, n_heads):
    tm = k_ref.shape[0]
    k = k_ref[...].reshape(tm, n_heads * HEAD)
    sq_ref[...] = lax.dot_general(e_ref[...], (k * k).astype(BF16), _NT,
                                  preferred_element_type=F32)
    for h in range(n_heads):
        o_ref[:, 2 * h * HEAD:(2 * h + 1) * HEAD] = k[:, h * HEAD:(h + 1) * HEAD].astype(BF16)
        o_ref[:, (2 * h + 1) * HEAD:(2 * h + 2) * HEAD] = _bias_terms(
            -dk_ref[:, h:h + 1], (tm, HEAD), 1, False).astype(BF16)


def _augment_keys(k, dcum, B, Lk, H):
    D = H * HEAD
    T = B * Lk
    tm = _pick(T, (512, 256, 128, 64))
    dk_rows = jnp.transpose(dcum, (0, 2, 1)).reshape(T, H)
    out, sq = pl.pallas_call(
        functools.partial(_augment_kernel, n_heads=H),
        grid=(T // tm,),
        in_specs=[pl.BlockSpec((tm, H, HEAD), lambda i: (i, 0, 0)),
                  pl.BlockSpec((tm, H), lambda i: (i, 0)),
                  pl.BlockSpec((LANES, D), lambda i: (0, 0))],
        out_specs=[pl.BlockSpec((tm, 2 * D), lambda i: (i, 0)),
                   pl.BlockSpec((LANES, tm), lambda i: (0, i))],
        out_shape=[jax.ShapeDtypeStruct((T, 2 * D), BF16),
                   jax.ShapeDtypeStruct((LANES, T), F32)],
        compiler_params=_params(("parallel",)),
        name="augment_keys",
    )(k, dk_rows, _head_selector(D))
    return out.reshape(B, Lk, 2 * D), sq[:H]


def _trunk(x, p, s0, past, W):
    B, T, D = x.shape
    H = D // HEAD
    depth = p.shape[0]
    n_a = len(W["w_in_a"])
    xf = x.reshape(B * T, D)
    pf = p.reshape(depth, B * T, p.shape[-1])
    states = []
    for layer in range(depth):
        final = layer == depth - 1
        if layer < n_a:
            z = _norm_matmul(xf, W["g_norm_a"][layer], W["w_in_a"][layer])
            og, s_fin = _hgrn_mix(z, W["lbs"][layer], W["g_out_a"][layer], s0[layer], B, T)
            states.append(s_fin)
            w_out = W["w_out_a"][layer]
        else:
            j = layer - n_a
            z = _norm_matmul(xf, W["g_norm_b"][j], W["w_in_b"][j])
            if k_aug is None:
                og = _fox_mix(z, k_all, v_all, dq, dk, B, T, q_off)
            else:
                qn = jnp.transpose(jnp.sqrt(_head_sq_sums(z, D)).reshape(H, B, T), (1, 0, 2))
                bound = (FOX_BOUND_MARGIN * HEAD ** -0.5) * qn * kmax
                c = (dcum - bound)[:, :, None, :]
                og = lax.cond(jnp.max(bound) <= FOX_BOUND_LIMIT,
                              lambda: _fox_mix_fast(z, c, k_aug, v_t, B, T),
                              lambda: _fox_mix(z, k_new.reshape(B, T, D).astype(BF16), v_all, dq,
                                               dk, B, T, q_off))
            w_out = W["w_out_b"][j]
        xf = _out_ple(og, xf, pf, layer, w_out, W["g_ple"][layer], W["w_ple_gate"][layer],
                      W["w_ple_in"][layer], W["g_final"], final)
        if layer == n_a - 1:
            fast = past is None and T % FOX_FAST_TILE == 0
            if fast:
                k_new, v_new, v_bf, logf_new = _norm_matmul_kv(xf, W["g_kv"], W["w_kv"], W["w_f"],
                                                               W["b_f"])
                k_bf = None
            else:
                k_new, k_bf = _norm_matmul(xf, W["g_kv"], W["w_kv"][:, :D], with_bf16=True)
                v_new, v_bf = _norm_matmul(xf, W["g_kv"], W["w_kv"][:, D:], with_bf16=True)
                k_bf = k_bf.reshape(B, T, D)
                logf_new = _norm_matmul(xf, W["g_kv"], W["w_f"], bias=W["b_f"])
            logf_new = logf_new[:, :H].reshape(B, T, H)
            v_bf = v_bf.reshape(B, T, D)
            if past is None:
                k_all, v_all, logf_all, q_off = k_bf, v_bf, logf_new, 0
            else:
                past_k, past_v, past_logf = past
                q_off = past_k.shape[1]
                k_all = jnp.concatenate([past_k.reshape(B, q_off, D).astype(BF16), k_bf], axis=1)
                v_all = jnp.concatenate([past_v.reshape(B, q_off, D).astype(BF16), v_bf], axis=1)
                logf_all = jnp.concatenate([past_logf.astype(F32), logf_new], axis=1)
            Lk = v_all.shape[1]
            pad = (-Lk) % LANES
            if pad:
                k_all = jnp.pad(k_all, ((0, 0), (0, pad), (0, 0)))
                v_all = jnp.pad(v_all, ((0, 0), (0, pad), (0, 0)))
                logf_all = jnp.pad(logf_all, ((0, 0), (0, pad), (0, 0)))
            rows = jnp.transpose(logf_all, (0, 2, 1)).reshape(B * H, Lk + pad)
            dcum = _cumsum_last(rows).reshape(B, H, Lk + pad)
            dk = dcum[:, :, None, :]
            dq = dcum[:, :, q_off:q_off + T, None]
            k_aug = None
            if fast:
                k_aug, k_sq = _augment_keys(k_new, dcum, B, T, H)
                kn = jnp.sqrt(k_sq).reshape(H, B, T)
                kmax = jnp.transpose(jnp.max(kn, axis=2))[:, :, None]
                v_t = jnp.transpose(v_bf.reshape(B, T, H, HEAD), (0, 2, 3, 1))
    return (xf.reshape(B, T, D), jnp.stack(states), k_new.reshape(B, T, H, HEAD),
            v_new.reshape(B, T, H, HEAD), logf_new)


def kernel(x_prompt, x_sample, state_hgrn, cache_k, cache_v, cache_logf, p_prompt, p_sample,
           g_norm_a, w_in_a, lb_logits, g_out_a, w_out_a, g_kv, w_kv, b_f,
           g_norm_b, w_in_b, w_out_b, w_ple_in, g_ple, w_ple_gate, g_final):
    D = x_prompt.shape[-1]
    H = D // HEAD
    n_a = w_in_a.shape[0]
    lbs = jnp.cumsum(jax.nn.softmax(lb_logits.astype(F32), axis=0), axis=0)
    lbs = lbs - lbs[:1]
    w_f = jnp.pad(w_kv[:, 2 * D:], ((0, 0), (0, LANES - H)))
    def layers(w):
        return [w[n].astype(BF16) for n in range(w.shape[0])]

    W = dict(
        g_norm_a=g_norm_a, w_in_a=layers(w_in_a), lbs=lbs, g_out_a=g_out_a,
        w_out_a=layers(w_out_a), g_kv=g_kv, w_kv=w_kv[:, :2 * D].astype(BF16),
        w_f=w_f.astype(BF16),
        b_f=jnp.pad(b_f, (0, LANES - H)), g_norm_b=g_norm_b, w_in_b=layers(w_in_b),
        w_out_b=layers(w_out_b), w_ple_in=layers(w_ple_in), g_ple=g_ple,
        w_ple_gate=layers(w_ple_gate), g_final=g_final)
    s0_prompt = jnp.zeros((n_a, x_prompt.shape[0], H, HEAD, HEAD), F32)
    y_p, st_p, k_p, v_p, f_p = _trunk(x_prompt, p_prompt, s0_prompt, None, W)
    y_s, st_s, k_s, v_s, f_s = _trunk(x_sample, p_sample, state_hgrn,
                                      (cache_k, cache_v, cache_logf), W)
    return (y_p, y_s, st_p, st_s, k_p, v_p, f_p, k_s, v_s, f_s)
```

```python
import functools

import jax
import jax.numpy as jnp
import numpy as np
from jax import lax
from jax.experimental import pallas as pl
from jax.experimental.pallas import tpu as pltpu

F32 = jnp.float32
BF16 = jnp.bfloat16

EPS = 1e-6
HEAD = 128
LANES = 128
SUBLANES = 8
CHUNK = 64
N_LEVELS = 6
N_TOP = 3
HGRN_HEADS = 16
FOX_SHORT_HEADS = 4
FOX_CACHED_HEADS = 8
FOX_SHORT_QUERIES = 128
FOX_SHORT_KEYS = 2048
MASK_VALUE = -1e30
VMEM_LIMIT = 56 * 1024 * 1024

_NT = (((1,), (1,)), ((), ()))


def _params(sem, vmem=VMEM_LIMIT):
    return pltpu.CompilerParams(dimension_semantics=sem, vmem_limit_bytes=vmem)


def _sigmoid(x):
    return 1.0 / (1.0 + jnp.exp(-x))


def _pick(n, prefs):
    for p in prefs:
        if n % p == 0:
            return p
    return n


def _nmm_kernel(*refs, log_sigmoid, with_bf16, row_chunk):
    if log_sigmoid:
        x_ref, g_ref, w_ref, b_ref = refs[:4]
        rest = refs[4:]
    else:
        x_ref, g_ref, w_ref = refs[:3]
        b_ref = None
        rest = refs[3:]
    if with_bf16:
        o_ref, ob_ref, xn_ref = rest
    else:
        o_ref, xn_ref = rest
        ob_ref = None
    tm = x_ref.shape[0]

    @pl.when(pl.program_id(1) == 0)
    def _():
        for r in range(0, tm, row_chunk):
            x = x_ref[r:r + row_chunk, :]
            ms = jnp.mean(x * x, axis=-1, keepdims=True)
            xn_ref[r:r + row_chunk, :] = (x * lax.rsqrt(ms + EPS) * g_ref[...]).astype(BF16)

    z = jnp.dot(xn_ref[...], w_ref[...], preferred_element_type=F32)
    if log_sigmoid:
        z = z + b_ref[...]
        z = jnp.minimum(z, 0.0) - jnp.log(1.0 + jnp.exp(-jnp.abs(z)))
    o_ref[...] = z
    if with_bf16:
        ob_ref[...] = z.astype(BF16)


def _norm_matmul(x, g, w, bias=None, with_bf16=False):
    T, D = x.shape
    N = w.shape[1]
    tm = _pick(T, (1024, 512, 256, 128, 64))
    tn = _pick(N, (1024, 512, 256, 128))
    row_chunk = min(tm, 256)
    in_specs = [
        pl.BlockSpec((tm, D), lambda i, j: (i, 0)),
        pl.BlockSpec((1, D), lambda i, j: (0, 0)),
        pl.BlockSpec((D, tn), lambda i, j: (0, j)),
    ]
    args = [x, g.reshape(1, D).astype(F32), w]
    if bias is not None:
        in_specs.append(pl.BlockSpec((1, tn), lambda i, j: (0, j)))
        args.append(bias.reshape(1, N).astype(F32))
    out_shape = [jax.ShapeDtypeStruct((T, N), F32)]
    out_specs = [pl.BlockSpec((tm, tn), lambda i, j: (i, j))]
    if with_bf16:
        out_shape.append(jax.ShapeDtypeStruct((T, N), BF16))
        out_specs.append(pl.BlockSpec((tm, tn), lambda i, j: (i, j)))
    out = pl.pallas_call(
        functools.partial(_nmm_kernel, log_sigmoid=bias is not None, with_bf16=with_bf16,
                          row_chunk=row_chunk),
        grid=(T // tm, N // tn),
        in_specs=in_specs,
        out_specs=out_specs,
        out_shape=out_shape,
        scratch_shapes=[pltpu.VMEM((tm, D), BF16)],
        compiler_params=_params(("parallel", "arbitrary")),
        name="norm_matmul",
    )(*args)
    return out if with_bf16 else out[0]


def _nmm_kv_kernel(x_ref, g_ref, w_ref, wf_ref, bf_ref, k_ref, v_ref, vb_ref, f_ref, xn_ref, *,
                   half, row_chunk):
    tm = x_ref.shape[0]
    j = pl.program_id(1)

    @pl.when(j == 0)
    def _():
        for r in range(0, tm, row_chunk):
            x = x_ref[r:r + row_chunk, :]
            ms = jnp.mean(x * x, axis=-1, keepdims=True)
            xn_ref[r:r + row_chunk, :] = (x * lax.rsqrt(ms + EPS) * g_ref[...]).astype(BF16)

    z = jnp.dot(xn_ref[...], w_ref[...], preferred_element_type=F32)

    @pl.when(j < half)
    def _():
        k_ref[...] = z.reshape(k_ref.shape)

    @pl.when(j >= half)
    def _():
        v_ref[...] = z.reshape(v_ref.shape)
        vb_ref[...] = z.astype(BF16)

    @pl.when(j == 2 * half - 1)
    def _():
        t = jnp.dot(xn_ref[...], wf_ref[...], preferred_element_type=F32) + bf_ref[...]
        f_ref[...] = jnp.minimum(t, 0.0) - jnp.log(1.0 + jnp.exp(-jnp.abs(t)))


def _norm_matmul_kv(x, g, w_kv, w_f, b_f):
    T, D = x.shape
    N = w_kv.shape[1] // 2
    tm = _pick(T, (1024, 512, 256, 128, 64))
    tn = _pick(N, (1024, 512, 256, 128))
    half = N // tn
    hn = tn // HEAD
    v_cols = lambda i, j: (i, jnp.maximum(j - half, 0))
    k_heads = lambda i, j: (i, jnp.minimum(j, half - 1), 0)
    v_heads = lambda i, j: (i, jnp.maximum(j - half, 0), 0)
    return pl.pallas_call(
        functools.partial(_nmm_kv_kernel, half=half, row_chunk=min(tm, 256)),
        grid=(T // tm, 2 * half),
        in_specs=[pl.BlockSpec((tm, D), lambda i, j: (i, 0)),
                  pl.BlockSpec((1, D), lambda i, j: (0, 0)),
                  pl.BlockSpec((D, tn), lambda i, j: (0, j)),
                  pl.BlockSpec((D, LANES), lambda i, j: (0, 0)),
                  pl.BlockSpec((1, LANES), lambda i, j: (0, 0))],
        out_specs=[pl.BlockSpec((tm, hn, HEAD), k_heads), pl.BlockSpec((tm, hn, HEAD), v_heads),
                   pl.BlockSpec((tm, tn), v_cols), pl.BlockSpec((tm, LANES), lambda i, j: (i, 0))],
        out_shape=[jax.ShapeDtypeStruct((T, N // HEAD, HEAD), F32),
                   jax.ShapeDtypeStruct((T, N // HEAD, HEAD), F32),
                   jax.ShapeDtypeStruct((T, N), BF16), jax.ShapeDtypeStruct((T, LANES), F32)],
        scratch_shapes=[pltpu.VMEM((tm, D), BF16)],
        compiler_params=_params(("parallel", "arbitrary")),
        name="norm_matmul_kv",
    )(x, g.reshape(1, D).astype(F32), w_kv, w_f, b_f.reshape(1, LANES).astype(F32))


def _hgrn_constants():
    C = CHUNK
    t = np.arange(C)[:, None]
    u = np.arange(C)[None, :]
    blocks = [(u <= t), (u > t)]
    masks = []
    for lvl in range(N_LEVELS):
        h = C >> (lvl + 1)
        mid = (t // (2 * h)) * (2 * h) + h
        if N_TOP <= lvl < N_LEVELS - 1:
            blocks.append(np.where(t >= mid, (u >= mid) & (u <= t), (u > t) & (u < mid)))
        mid_s = (u // (2 * h)) * (2 * h) + h
        masks.append((t // (2 * h) == u // (2 * h)) & (t >= mid) & (u < mid_s))
    masks.append(t == u)
    sm = np.concatenate(blocks, axis=0).astype(np.float32)
    mk = np.stack(masks, axis=0).astype(np.float32)
    return jnp.asarray(sm, BF16), jnp.asarray(mk, F32)


def _midpoint_rows(G, h):
    rows = [jnp.broadcast_to(G[b + h - 1:b + h, :], (2 * h, HEAD)) for b in range(0, CHUNK, 2 * h)]
    return rows[0] if len(rows) == 1 else jnp.concatenate(rows, axis=0)


def _hgrn_kernel(zq_ref, zf_ref, zi_ref, zg_ref, lb_ref, go_ref, s0_ref, sm_ref, mk_ref,
                 o_ref, sout_ref, st_ref, *, n_chunks, n_heads):
    C = CHUNK
    tb = pl.program_id(2)

    @pl.when(tb == 0)
    def _():
        for hh in range(n_heads):
            st_ref[hh] = s0_ref[hh].T

    odd_row = lax.broadcasted_iota(jnp.int32, (C, HEAD), 0) % 2 == 1

    def chunk(c, carry):
        r0 = pl.multiple_of(c * C, C)
        heads = range(n_heads)
        cols = [slice(hh * HEAD, (hh + 1) * HEAD) for hh in heads]
        q, k, v, f, d2 = [], [], [], [], []
        for hh in heads:
            lb = lb_ref[:, cols[hh]]
            zq = zq_ref[pl.ds(r0, C), cols[hh]]
            zf = zf_ref[pl.ds(r0, C), cols[hh]]
            q.append(zq * _sigmoid(zq))
            f.append(lb + (1.0 - lb) * _sigmoid(zf))
            g = jnp.log(f[hh])
            k.append(1.0 - f[hh])
            v.append(zi_ref[pl.ds(r0, C), cols[hh]])
            g_hi = g.astype(BF16)
            g_lo = (g - g_hi.astype(F32)).astype(BF16)
            d2.append(jnp.dot(sm_ref[...], jnp.concatenate([g_hi, g_lo], axis=1),
                              preferred_element_type=F32))
        e, lv = [], []
        for hh in heads:
            d = d2[hh][:, :HEAD] + d2[hh][:, HEAD:]
            e.append(jnp.exp(d))
            G = d[0:C]
            top = [jnp.exp(-jnp.abs(G - _midpoint_rows(G, C >> (lvl + 1)))) for lvl in range(N_TOP)]
            lv.append(top + [e[hh][2 * C:3 * C], e[hh][3 * C:4 * C],
                             jnp.where(odd_row, f[hh], 1.0)])
        a = []
        for hh in heads:
            parts = [lax.dot_general(q[hh].astype(BF16), k[hh].astype(BF16), _NT,
                                     preferred_element_type=F32)]
            for lvl in range(N_LEVELS):
                el = lv[hh][lvl]
                parts.append(lax.dot_general((q[hh] * el).astype(BF16), (k[hh] * el).astype(BF16),
                                             _NT, preferred_element_type=F32))
            acc = mk_ref[N_LEVELS] * parts[0]
            for lvl in range(N_LEVELS):
                acc = acc + mk_ref[lvl] * parts[1 + lvl]
            a.append(acc)
        o = []
        for hh in heads:
            st = st_ref[hh]
            qg = (q[hh] * e[hh][0:C]).astype(BF16)
            kg = (k[hh] * e[hh][C:2 * C]).astype(BF16)
            vb = v[hh].astype(BF16)
            oh = jnp.dot(a[hh].astype(BF16), vb, preferred_element_type=F32)
            o.append(oh + lax.dot_general(qg, st.astype(BF16), _NT, preferred_element_type=F32))
            st_ref[hh] = st * e[hh][C - 1:C, :] + jnp.dot(v[hh].T.astype(BF16), kg,
                                                          preferred_element_type=F32)
        for hh in heads:
            zg = zg_ref[pl.ds(r0, C), cols[hh]]
            ms = jnp.mean(o[hh] * o[hh], axis=-1, keepdims=True)
            og = o[hh] * lax.rsqrt(ms + EPS) * go_ref[:, cols[hh]] * (zg * _sigmoid(zg))
            o_ref[pl.ds(r0, C), cols[hh]] = og.astype(o_ref.dtype)
        return carry

    lax.fori_loop(0, n_chunks, chunk, 0)

    @pl.when(tb == pl.num_programs(2) - 1)
    def _():
        for hh in range(n_heads):
            sout_ref[hh] = st_ref[hh].T


def _hgrn_mix(z, lb, g_out, s0, B, T):
    D = z.shape[1] // 4
    H = D // HEAD
    hb = _pick(H, (HGRN_HEADS,))
    tb = _pick(T, (256, 128, 64))
    nt = T // tb
    nh = H // hb
    sm, mk = _hgrn_constants()

    def zspec(part):
        return pl.BlockSpec((tb, hb * HEAD), lambda b, h, t: (b * nt + t, part * nh + h))

    head_vec = pl.BlockSpec((1, hb * HEAD), lambda b, h, t: (0, h))
    state = pl.BlockSpec((None, hb, HEAD, HEAD), lambda b, h, t: (b, h, 0, 0))
    og, s_fin = pl.pallas_call(
        functools.partial(_hgrn_kernel, n_chunks=tb // CHUNK, n_heads=hb),
        grid=(B, nh, nt),
        in_specs=[zspec(0), zspec(1), zspec(2), zspec(3), head_vec, head_vec, state,
                  pl.BlockSpec(sm.shape, lambda b, h, t: (0, 0)),
                  pl.BlockSpec(mk.shape, lambda b, h, t: (0, 0, 0))],
        out_specs=[pl.BlockSpec((tb, hb * HEAD), lambda b, h, t: (b * nt + t, h)), state],
        out_shape=[jax.ShapeDtypeStruct((B * T, D), BF16),
                   jax.ShapeDtypeStruct((B, H, HEAD, HEAD), F32)],
        scratch_shapes=[pltpu.VMEM((hb, HEAD, HEAD), F32)],
        compiler_params=_params(("parallel", "parallel", "arbitrary")),
        name="hgrn_mix",
    )(z, z, z, z, lb.reshape(1, D), g_out.reshape(1, D).astype(F32), s0, sm, mk)
    return og, s_fin


def _out_ple_kernel(og_ref, x_ref, p_ref, wo_ref, gpg_ref, wpg_ref, wpin_ref, gfin_ref, o_ref,
                    *, final, col_chunk):
    D = x_ref.shape[1]
    y = x_ref[...] + jnp.dot(og_ref[...], wo_ref[...], preferred_element_type=F32)
    ms = jnp.mean(y * y, axis=-1, keepdims=True)
    yn = (y * lax.rsqrt(ms + EPS) * gpg_ref[...]).astype(BF16)
    pb = p_ref[...].astype(BF16)
    parts = []
    for c0 in range(0, D, col_chunk):
        gate = _sigmoid(jnp.dot(yn, wpg_ref[:, c0:c0 + col_chunk], preferred_element_type=F32))
        emb = jnp.dot(pb, wpin_ref[:, c0:c0 + col_chunk], preferred_element_type=F32)
        parts.append(y[:, c0:c0 + col_chunk] + gate * emb)
    if final:
        ms2 = sum(jnp.sum(t * t, axis=-1, keepdims=True) for t in parts) * (1.0 / D)
        inv = lax.rsqrt(ms2 + EPS)
        for n, c0 in enumerate(range(0, D, col_chunk)):
            o_ref[:, c0:c0 + col_chunk] = parts[n] * inv * gfin_ref[:, c0:c0 + col_chunk]
    else:
        for n, c0 in enumerate(range(0, D, col_chunk)):
            o_ref[:, c0:c0 + col_chunk] = parts[n]


def _out_ple(og, x, p, layer, w_out, g_pg, w_pg, w_pin, g_final, final):
    T, D = x.shape
    P = p.shape[2]
    tm = _pick(T, (256, 128, 64))
    col_chunk = min(D, 512)
    rows = lambda i: (i, 0)
    fixed = lambda i: (0, 0)
    once = pl.Buffered(1)
    return pl.pallas_call(
        functools.partial(_out_ple_kernel, final=final, col_chunk=col_chunk),
        grid=(T // tm,),
        in_specs=[pl.BlockSpec((tm, D), rows), pl.BlockSpec((tm, D), rows),
                  pl.BlockSpec((None, tm, P), lambda i: (layer, i, 0)),
                  pl.BlockSpec((D, D), fixed, pipeline_mode=once),
                  pl.BlockSpec((1, D), fixed),
                  pl.BlockSpec((D, D), fixed, pipeline_mode=once),
                  pl.BlockSpec((P, D), fixed, pipeline_mode=once),
                  pl.BlockSpec((1, D), fixed)],
        out_specs=pl.BlockSpec((tm, D), rows),
        out_shape=jax.ShapeDtypeStruct((T, D), F32),
        compiler_params=_params(("parallel",)),
        name="out_ple",
    )(og, x, p, w_out, g_pg.reshape(1, D).astype(F32), w_pg, w_pin,
      g_final.reshape(1, D).astype(F32))


def _cumsum_kernel(x_ref, u_ref, o_ref, carry_ref, *, n_sub):
    @pl.when(pl.program_id(0) == 0)
    def _():
        carry_ref[...] = jnp.zeros_like(carry_ref)

    u = u_ref[...]
    c = carry_ref[:, 0:1]
    for s in range(n_sub):
        x = x_ref[:, s * LANES:(s + 1) * LANES]
        hi = x.astype(BF16)
        r1 = x - hi.astype(F32)
        mid = r1.astype(BF16)
        lo = (r1 - mid.astype(F32)).astype(BF16)
        cs = (jnp.dot(hi, u, preferred_element_type=F32)
              + jnp.dot(mid, u, preferred_element_type=F32)
              + jnp.dot(lo, u, preferred_element_type=F32)) + c
        o_ref[:, s * LANES:(s + 1) * LANES] = cs
        c = cs[:, LANES - 1:LANES]
    carry_ref[...] = jnp.broadcast_to(c, carry_ref.shape)


def _cumsum_last(x):
    R, L = x.shape
    cb = _pick(L, (2048, 1024, 512, 256, 128))
    u = jnp.asarray(np.triu(np.ones((LANES, LANES), np.float32)), BF16)
    return pl.pallas_call(
        functools.partial(_cumsum_kernel, n_sub=cb // LANES),
        grid=(L // cb,),
        in_specs=[pl.BlockSpec((R, cb), lambda j: (0, j)),
                  pl.BlockSpec((LANES, LANES), lambda j: (0, 0))],
        out_specs=pl.BlockSpec((R, cb), lambda j: (0, j)),
        out_shape=jax.ShapeDtypeStruct((R, L), F32),
        scratch_shapes=[pltpu.VMEM((R, LANES), F32)],
        compiler_params=_params(("arbitrary",)),
        name="cumsum",
    )(x, u)


def _fox_kernel(q_ref, gate_ref, k_ref, v_ref, dq_ref, dk_ref, o_ref, *, tq, tk, q_off, scale):
    i = pl.program_id(2)
    q = q_ref[...].astype(BF16)
    dq = dq_ref[...]
    first_pos = q_off + i * tq
    n_full = (first_pos + 1) // tk
    n_all = (first_pos + tq - 1) // tk + 1

    def step(j, carry, masked):
        m, l, acc = carry
        start = pl.multiple_of(j * tk, tk)
        k = k_ref[pl.ds(start, tk), :]
        v = v_ref[pl.ds(start, tk), :]
        s = lax.dot_general(q, k, _NT, preferred_element_type=F32) * scale
        s = s + dq - dk_ref[:, pl.ds(start, tk)]
        if masked:
            kpos = start + lax.broadcasted_iota(jnp.int32, (tq, tk), 1)
            qpos = first_pos + lax.broadcasted_iota(jnp.int32, (tq, tk), 0)
            s = jnp.where(kpos <= qpos, s, MASK_VALUE)
        m_new = jnp.maximum(m, jnp.max(s, axis=-1, keepdims=True))
        alpha = jnp.exp(m - m_new)
        p = jnp.exp(s - m_new)
        l = alpha * l + jnp.sum(p, axis=-1, keepdims=True)
        acc = alpha * acc + jnp.dot(p.astype(BF16), v, preferred_element_type=F32)
        return m_new, l, acc

    init = (jnp.full((tq, 1), MASK_VALUE, F32), jnp.zeros((tq, 1), F32),
            jnp.zeros((tq, HEAD), F32))
    carry = lax.fori_loop(0, n_full, functools.partial(step, masked=False), init)
    _, l, acc = lax.fori_loop(n_full, n_all, functools.partial(step, masked=True), carry)
    gate = gate_ref[...]
    o = acc * (1.0 / l)
    o_ref[...] = (o * (gate * _sigmoid(gate))).astype(o_ref.dtype)


def _fox_short_kernel(q_ref, gate_ref, k_ref, v_ref, dq_ref, dk_ref, o_ref, *, n_heads, q_off,
                      scale):
    T, Lk = q_ref.shape[0], k_ref.shape[0]
    heads = range(n_heads)
    cols = [slice(hh * HEAD, (hh + 1) * HEAD) for hh in heads]
    visible = (lax.broadcasted_iota(jnp.int32, (T, Lk), 1)
               <= q_off + lax.broadcasted_iota(jnp.int32, (T, Lk), 0))
    s = []
    for hh in heads:
        sh = lax.dot_general(q_ref[:, cols[hh]].astype(BF16), k_ref[:, cols[hh]], _NT,
                             preferred_element_type=F32) * scale
        s.append(jnp.where(visible, sh + dq_ref[hh] - dk_ref[hh], MASK_VALUE))
    p = [jnp.exp(sh - jnp.max(sh, axis=-1, keepdims=True)) for sh in s]
    o = [jnp.dot(p[hh].astype(BF16), v_ref[:, cols[hh]], preferred_element_type=F32)
         for hh in heads]
    for hh in heads:
        gate = gate_ref[:, cols[hh]]
        oh = o[hh] * (1.0 / jnp.sum(p[hh], axis=-1, keepdims=True))
        o_ref[:, cols[hh]] = (oh * (gate * _sigmoid(gate))).astype(o_ref.dtype)


def _fox_mix_short(z, k_all, v_all, dq, dk, B, T, q_off):
    D = z.shape[1] // 2
    H = D // HEAD
    Lk = k_all.shape[1]
    hb = _pick(H, (FOX_SHORT_HEADS,))
    nh = H // hb
    kv = pl.BlockSpec((None, Lk, hb * HEAD), lambda b, h: (b, 0, h))
    return pl.pallas_call(
        functools.partial(_fox_short_kernel, n_heads=hb, q_off=q_off, scale=HEAD ** -0.5),
        grid=(B, nh),
        in_specs=[pl.BlockSpec((T, hb * HEAD), lambda b, h: (b, h)),
                  pl.BlockSpec((T, hb * HEAD), lambda b, h: (b, nh + h)),
                  kv, kv,
                  pl.BlockSpec((None, hb, T, 1), lambda b, h: (b, h, 0, 0)),
                  pl.BlockSpec((None, hb, 1, Lk), lambda b, h: (b, h, 0, 0))],
        out_specs=pl.BlockSpec((T, hb * HEAD), lambda b, h: (b, h)),
        out_shape=jax.ShapeDtypeStruct((B * T, D), BF16),
        compiler_params=_params(("parallel", "parallel")),
        name="fox_mix_short",
    )(z, z, k_all, v_all, dq, dk)


def _fox_cached_kernel(q_ref, gate_ref, kc_ref, vc_ref, kn_ref, vn_ref, dq_ref, dkc_ref, dkn_ref,
                       o_ref, *, n_heads, scale):
    T, Lc = q_ref.shape[0], kc_ref.shape[0]
    width = n_heads * HEAD
    kc = kc_ref[...].reshape(Lc, width).astype(BF16)
    vc = vc_ref[...].reshape(Lc, width).astype(BF16)
    kn = kn_ref[...].reshape(T, width).astype(BF16)
    vn = vn_ref[...].reshape(T, width).astype(BF16)
    heads = range(n_heads)
    cols = [slice(hh * HEAD, (hh + 1) * HEAD) for hh in heads]
    causal = (lax.broadcasted_iota(jnp.int32, (T, T), 1)
              <= lax.broadcasted_iota(jnp.int32, (T, T), 0))
    sc, sn = [], []
    for hh in heads:
        q = q_ref[:, cols[hh]].astype(BF16)
        c = lax.dot_general(q, kc[:, cols[hh]], _NT, preferred_element_type=F32) * scale
        n = lax.dot_general(q, kn[:, cols[hh]], _NT, preferred_element_type=F32) * scale
        sc.append(c + dq_ref[hh] - dkc_ref[hh])
        sn.append(jnp.where(causal, n + dq_ref[hh] - dkn_ref[hh], MASK_VALUE))
    pc, pn = [], []
    for hh in heads:
        m = jnp.maximum(jnp.max(sc[hh], axis=-1, keepdims=True),
                        jnp.max(sn[hh], axis=-1, keepdims=True))
        pc.append(jnp.exp(sc[hh] - m))
        pn.append(jnp.exp(sn[hh] - m))
    o = [jnp.dot(pc[hh].astype(BF16), vc[:, cols[hh]], preferred_element_type=F32)
         + jnp.dot(pn[hh].astype(BF16), vn[:, cols[hh]], preferred_element_type=F32)
         for hh in heads]
    for hh in heads:
        gate = gate_ref[:, cols[hh]]
        l = (jnp.sum(pc[hh], axis=-1, keepdims=True) + jnp.sum(pn[hh], axis=-1, keepdims=True))
        o_ref[:, cols[hh]] = (o[hh] * (1.0 / l) * (gate * _sigmoid(gate))).astype(o_ref.dtype)


def _fox_mix_cached(z, k_cache, v_cache, k_new, v_new, dq, dk_cache, dk_new, B, T):
    D = z.shape[1] // 2
    H = D // HEAD
    Lc = k_cache.shape[1]
    hb = _pick(H, (FOX_CACHED_HEADS,))
    nh = H // hb
    cache = pl.BlockSpec((None, Lc, hb, HEAD), lambda b, h: (b, 0, h, 0))
    new = pl.BlockSpec((None, T, hb, HEAD), lambda b, h: (b, 0, h, 0))
    return pl.pallas_call(
        functools.partial(_fox_cached_kernel, n_heads=hb, scale=HEAD ** -0.5),
        grid=(B, nh),
        in_specs=[pl.BlockSpec((T, hb * HEAD), lambda b, h: (b, h)),
                  pl.BlockSpec((T, hb * HEAD), lambda b, h: (b, nh + h)),
                  cache, cache, new, new,
                  pl.BlockSpec((None, hb, T, 1), lambda b, h: (b, h, 0, 0)),
                  pl.BlockSpec((None, hb, 1, Lc), lambda b, h: (b, h, 0, 0)),
                  pl.BlockSpec((None, hb, 1, T), lambda b, h: (b, h, 0, 0))],
        out_specs=pl.BlockSpec((T, hb * HEAD), lambda b, h: (b, h)),
        out_shape=jax.ShapeDtypeStruct((B * T, D), BF16),
        compiler_params=_params(("parallel", "parallel")),
        name="fox_mix_cached",
    )(z, z, k_cache, v_cache, k_new, v_new, dq, dk_cache, dk_new)


def _fox_mix(z, k_all, v_all, dq, dk, B, T, q_off):
    D = z.shape[1] // 2
    H = D // HEAD
    Lk = k_all.shape[1]
    if T <= FOX_SHORT_QUERIES and Lk <= FOX_SHORT_KEYS:
        return _fox_mix_short(z, k_all, v_all, dq, dk, B, T, q_off)
    tq = _pick(T, (512, 256, 128, 64))
    tk = Lk if Lk <= 2048 else _pick(Lk, (512, 256, 128))
    nq = T // tq
    kv = pl.BlockSpec((None, Lk, HEAD), lambda b, h, i: (b, 0, h))
    return pl.pallas_call(
        functools.partial(_fox_kernel, tq=tq, tk=tk, q_off=q_off, scale=HEAD ** -0.5),
        grid=(B, H, nq),
        in_specs=[pl.BlockSpec((tq, HEAD), lambda b, h, i: (b * nq + i, h)),
                  pl.BlockSpec((tq, HEAD), lambda b, h, i: (b * nq + i, H + h)),
                  kv, kv,
                  pl.BlockSpec((None, None, tq, 1), lambda b, h, i: (b, h, i, 0)),
                  pl.BlockSpec((None, None, 1, Lk), lambda b, h, i: (b, h, 0, 0))],
        out_specs=pl.BlockSpec((tq, HEAD), lambda b, h, i: (b * nq + i, h)),
        out_shape=jax.ShapeDtypeStruct((B * T, D), BF16),
        compiler_params=_params(("parallel", "parallel", "arbitrary")),
        name="fox_mix",
    )(z, z, k_all, v_all, dq, dk)


FOX_BOUND_MARGIN = 1.02
FOX_BOUND_LIMIT = 32.0
FOX_FAST_TILE = 1024
AUG = 6


def _top16(x):
    bits = lax.bitcast_convert_type(x, jnp.uint32) & jnp.uint32(0xFFFF0000)
    return lax.bitcast_convert_type(bits, F32)


def _split3(x):
    hi = _top16(x)
    r1 = x - hi
    mid = _top16(r1)
    lo = r1 - mid
    return hi.astype(BF16), mid.astype(BF16), lo.astype(BF16)


def _bias_terms(x, shape, axis, value_first):
    idx = lax.broadcasted_iota(jnp.int32, shape, axis)
    v0, o0 = (0, AUG // 2) if value_first else (AUG // 2, 0)
    hi, mid, lo = _split3(x)
    ones = (idx >= o0) & (idx < o0 + AUG // 2)
    return jnp.where(idx == v0, hi.astype(F32), jnp.where(idx == v0 + 1, mid.astype(F32),
           jnp.where(idx == v0 + 2, lo.astype(F32), jnp.where(ones, 1.0, 0.0))))


def _head_selector(D):
    e = np.zeros((LANES, D), np.float32)
    e[np.arange(D) // HEAD, np.arange(D)] = 1.0
    return jnp.asarray(e, BF16)


def _sq_sums_kernel(x_ref, e_ref, o_ref):
    x = x_ref[...]
    o_ref[...] = lax.dot_general(e_ref[...], (x * x).astype(BF16), _NT,
                                 preferred_element_type=F32)


def _head_sq_sums(x, D):
    T = x.shape[0]
    H = D // HEAD
    tm = _pick(T, (512, 256, 128, 64))
    return pl.pallas_call(
        _sq_sums_kernel,
        grid=(T // tm,),
        in_specs=[pl.BlockSpec((tm, D), lambda i: (i, 0)),
                  pl.BlockSpec((LANES, D), lambda i: (0, 0))],
        out_specs=pl.BlockSpec((LANES, tm), lambda i: (0, i)),
        out_shape=jax.ShapeDtypeStruct((LANES, T), F32),
        compiler_params=_params(("parallel",)),
        name="head_sq_sums",
    )(x, _head_selector(D))[:H]


def _fox_fast_kernel(q_ref, c_ref, gate_ref, ka_ref, vt_ref, o_ref, qa_ref, *, tile, scale):
    i = pl.program_id(2)
    qa_ref[:, :HEAD] = (q_ref[...] * scale).astype(BF16)
    qa_ref[:, HEAD:] = _bias_terms(c_ref[...], (HEAD, tile), 0, True).T.astype(BF16)

    def scores(j, masked):
        start = pl.multiple_of(j * tile, tile)
        s = lax.dot_general(ka_ref[pl.ds(start, tile), :], qa_ref[...], _NT,
                            preferred_element_type=F32)
        if masked:
            kpos = lax.broadcasted_iota(jnp.int32, (tile, tile), 0)
            qpos = lax.broadcasted_iota(jnp.int32, (tile, tile), 1)
            s = jnp.where(kpos <= qpos, s, MASK_VALUE)
        return s

    def consume(s, j, lp, acc):
        start = pl.multiple_of(j * tile, tile)
        p = jnp.exp(s)
        for g in range(tile // SUBLANES):
            lp = lp + p[g * SUBLANES:(g + 1) * SUBLANES, :]
        acc = acc + jnp.dot(vt_ref[:, pl.ds(start, tile)], p.astype(BF16),
                            preferred_element_type=F32)
        return lp, acc

    def pair(j0, carry, masked_b):
        lp, acc = carry
        sa = scores(j0, False)
        sb = scores(j0 + 1, masked_b)
        lp, acc = consume(sa, j0, lp, acc)
        return consume(sb, j0 + 1, lp, acc)

    def quad(j0, carry):
        lp, acc = carry
        s0 = scores(j0, False)
        s1 = scores(j0 + 1, False)
        lp, acc = consume(s0, j0, lp, acc)
        s2 = scores(j0 + 2, False)
        lp, acc = consume(s1, j0 + 1, lp, acc)
        s3 = scores(j0 + 3, False)
        lp, acc = consume(s2, j0 + 2, lp, acc)
        return consume(s3, j0 + 3, lp, acc)

    init = (jnp.zeros((SUBLANES, tile), F32), jnp.zeros((HEAD, tile), F32))
    carry = lax.fori_loop(0, i // 4, lambda t, c: quad(4 * t, c), init)
    carry = lax.fori_loop(2 * (i // 4), i // 2, lambda t, c: pair(2 * t, c, False), carry)
    lp, acc = lax.cond(i % 2 == 1, lambda c: pair(i - 1, c, True),
                       lambda c: consume(scores(i, True), i, *c), carry)
    gate = gate_ref[...]
    o = (acc * (1.0 / jnp.sum(lp, axis=0, keepdims=True))).T
    o_ref[...] = (o * (gate * _sigmoid(gate))).astype(o_ref.dtype)


def _fox_mix_fast(z, c, k_aug, v_t, B, T):
    D = z.shape[1] // 2
    H = D // HEAD
    tile = FOX_FAST_TILE
    nq = T // tile
    return pl.pallas_call(
        functools.partial(_fox_fast_kernel, tile=tile, scale=HEAD ** -0.5),
        grid=(B, H, nq),
        in_specs=[pl.BlockSpec((tile, HEAD), lambda b, h, i: (b * nq + i, h)),
                  pl.BlockSpec((None, None, 1, tile), lambda b, h, i: (b, h, 0, i)),
                  pl.BlockSpec((tile, HEAD), lambda b, h, i: (b * nq + i, H + h)),
                  pl.BlockSpec((None, T, 2 * HEAD), lambda b, h, i: (b, 0, h)),
                  pl.BlockSpec((None, None, HEAD, T), lambda b, h, i: (b, h, 0, 0))],
        out_specs=pl.BlockSpec((tile, HEAD), lambda b, h, i: (b * nq + i, h)),
        out_shape=jax.ShapeDtypeStruct((B * T, D), BF16),
        scratch_shapes=[pltpu.VMEM((tile, 2 * HEAD), BF16)],
        compiler_params=_params(("parallel", "parallel", "arbitrary")),
        name="fox_mix_fast",
    )(z, c, z, k_aug, v_t)


def _augment_kernel(k_ref, dk_ref, e_ref, o_ref, sq_ref, *, n_heads):
    tm = k_ref.shape[0]
    k = k_ref[...].reshape(tm, n_heads * HEAD)
    sq_ref[...] = lax.dot_general(e_ref[...], (k * k).astype(BF16), _NT,
                                  preferred_element_type=F32)
    for h in range(n_heads):
        o_ref[:, 2 * h * HEAD:(2 * h + 1) * HEAD] = k[:, h * HEAD:(h + 1) * HEAD].astype(BF16)
        o_ref[:, (2 * h + 1) * HEAD:(2 * h + 2) * HEAD] = _bias_terms(
            -dk_ref[:, h:h + 1], (tm, HEAD), 1, False).astype(BF16)


def _augment_keys(k, dcum, B, Lk, H):
    D = H * HEAD
    T = B * Lk
    tm = _pick(T, (512, 256, 128, 64))
    dk_rows = jnp.transpose(dcum, (0, 2, 1)).reshape(T, H)
    out, sq = pl.pallas_call(
        functools.partial(_augment_kernel, n_heads=H),
        grid=(T // tm,),
        in_specs=[pl.BlockSpec((tm, H, HEAD), lambda i: (i, 0, 0)),
                  pl.BlockSpec((tm, H), lambda i: (i, 0)),
                  pl.BlockSpec((LANES, D), lambda i: (0, 0))],
        out_specs=[pl.BlockSpec((tm, 2 * D), lambda i: (i, 0)),
                   pl.BlockSpec((LANES, tm), lambda i: (0, i))],
        out_shape=[jax.ShapeDtypeStruct((T, 2 * D), BF16),
                   jax.ShapeDtypeStruct((LANES, T), F32)],
        compiler_params=_params(("parallel",)),
        name="augment_keys",
    )(k, dk_rows, _head_selector(D))
    return out.reshape(B, Lk, 2 * D), sq[:H]


def _trunk(x, p, s0, past, W):
    B, T, D = x.shape
    H = D // HEAD
    depth = p.shape[0]
    n_a = len(W["w_in_a"])
    xf = x.reshape(B * T, D)
    pf = p.reshape(depth, B * T, p.shape[-1])
    states = []
    for layer in range(depth):
        final = layer == depth - 1
        if layer < n_a:
            z = _norm_matmul(xf, W["g_norm_a"][layer], W["w_in_a"][layer])
            og, s_fin = _hgrn_mix(z, W["lbs"][layer], W["g_out_a"][layer], s0[layer], B, T)
            states.append(s_fin)
            w_out = W["w_out_a"][layer]
        else:
            j = layer - n_a
            z = _norm_matmul(xf, W["g_norm_b"][j], W["w_in_b"][j])
            if cached:
                og = _fox_mix_cached(z, past[0].astype(F32), past[1].astype(F32),
                                     k_new.reshape(B, T, H, HEAD), v_new.reshape(B, T, H, HEAD),
                                     dq, dk[..., :q_off], dk[..., q_off:q_off + T], B, T)
            elif k_aug is None:
                og = _fox_mix(z, k_all, v_all, dq, dk, B, T, q_off)
            else:
                qn = jnp.transpose(jnp.sqrt(_head_sq_sums(z, D)).reshape(H, B, T), (1, 0, 2))
                bound = (FOX_BOUND_MARGIN * HEAD ** -0.5) * qn * kmax
                c = (dcum - bound)[:, :, None, :]
                og = lax.cond(jnp.max(bound) <= FOX_BOUND_LIMIT,
                              lambda: _fox_mix_fast(z, c, k_aug, v_t, B, T),
                              lambda: _fox_mix(z, k_new.reshape(B, T, D).astype(BF16), v_all, dq,
                                               dk, B, T, q_off))
            w_out = W["w_out_b"][j]
        xf = _out_ple(og, xf, pf, layer, w_out, W["g_ple"][layer], W["w_ple_gate"][layer],
                      W["w_ple_in"][layer], W["g_final"], final)
        if layer == n_a - 1:
            fast = past is None and T % FOX_FAST_TILE == 0
            cached = (past is not None and T <= FOX_SHORT_QUERIES and T % SUBLANES == 0
                      and past[0].shape[1] % LANES == 0
                      and past[0].shape[1] + T <= FOX_SHORT_KEYS)
            if fast or cached:
                k_new, v_new, v_bf, logf_new = _norm_matmul_kv(xf, W["g_kv"], W["w_kv"], W["w_f"],
                                                               W["b_f"])
                k_bf = None
            else:
                k_new, k_bf = _norm_matmul(xf, W["g_kv"], W["w_kv"][:, :D], with_bf16=True)
                v_new, v_bf = _norm_matmul(xf, W["g_kv"], W["w_kv"][:, D:], with_bf16=True)
                k_bf = k_bf.reshape(B, T, D)
                logf_new = _norm_matmul(xf, W["g_kv"], W["w_f"], bias=W["b_f"])
            logf_new = logf_new[:, :H].reshape(B, T, H)
            v_bf = v_bf.reshape(B, T, D)
            if past is None:
                k_all, v_all, logf_all, q_off = k_bf, v_bf, logf_new, 0
            elif cached:
                k_all, v_all, q_off = None, None, past[0].shape[1]
                logf_all = jnp.concatenate([past[2].astype(F32), logf_new], axis=1)
            else:
                past_k, past_v, past_logf = past
                q_off = past_k.shape[1]
                k_all = jnp.concatenate([past_k.reshape(B, q_off, D).astype(BF16), k_bf], axis=1)
                v_all = jnp.concatenate([past_v.reshape(B, q_off, D).astype(BF16), v_bf], axis=1)
                logf_all = jnp.concatenate([past_logf.astype(F32), logf_new], axis=1)
            Lk = logf_all.shape[1]
            pad = (-Lk) % LANES
            if pad:
                if not cached:
                    k_all = jnp.pad(k_all, ((0, 0), (0, pad), (0, 0)))
                    v_all = jnp.pad(v_all, ((0, 0), (0, pad), (0, 0)))
                logf_all = jnp.pad(logf_all, ((0, 0), (0, pad), (0, 0)))
            rows = jnp.transpose(logf_all, (0, 2, 1)).reshape(B * H, Lk + pad)
            dcum = _cumsum_last(rows).reshape(B, H, Lk + pad)
            dk = dcum[:, :, None, :]
            dq = dcum[:, :, q_off:q_off + T, None]
            k_aug = None
            if fast:
                k_aug, k_sq = _augment_keys(k_new, dcum, B, T, H)
                kn = jnp.sqrt(k_sq).reshape(H, B, T)
                kmax = jnp.transpose(jnp.max(kn, axis=2))[:, :, None]
                v_t = jnp.transpose(v_bf.reshape(B, T, H, HEAD), (0, 2, 3, 1))
    return (xf.reshape(B, T, D), jnp.stack(states), k_new.reshape(B, T, H, HEAD),
            v_new.reshape(B, T, H, HEAD), logf_new)


def kernel(x_prompt, x_sample, state_hgrn, cache_k, cache_v, cache_logf, p_prompt, p_sample,
           g_norm_a, w_in_a, lb_logits, g_out_a, w_out_a, g_kv, w_kv, b_f,
           g_norm_b, w_in_b, w_out_b, w_ple_in, g_ple, w_ple_gate, g_final):
    D = x_prompt.shape[-1]
    H = D // HEAD
    n_a = w_in_a.shape[0]
    lbs = jnp.cumsum(jax.nn.softmax(lb_logits.astype(F32), axis=0), axis=0)
    lbs = lbs - lbs[:1]
    w_f = jnp.pad(w_kv[:, 2 * D:], ((0, 0), (0, LANES - H)))
    def layers(w):
        return [w[n].astype(BF16) for n in range(w.shape[0])]

    W = dict(
        g_norm_a=g_norm_a, w_in_a=layers(w_in_a), lbs=lbs, g_out_a=g_out_a,
        w_out_a=layers(w_out_a), g_kv=g_kv, w_kv=w_kv[:, :2 * D].astype(BF16),
        w_f=w_f.astype(BF16),
        b_f=jnp.pad(b_f, (0, LANES - H)), g_norm_b=g_norm_b, w_in_b=layers(w_in_b),
        w_out_b=layers(w_out_b), w_ple_in=layers(w_ple_in), g_ple=g_ple,
        w_ple_gate=layers(w_ple_gate), g_final=g_final)
    s0_prompt = jnp.zeros((n_a, x_prompt.shape[0], H, HEAD, HEAD), F32)
    y_p, st_p, k_p, v_p, f_p = _trunk(x_prompt, p_prompt, s0_prompt, None, W)
    y_s, st_s, k_s, v_s, f_s = _trunk(x_sample, p_sample, state_hgrn,
                                      (cache_k, cache_v, cache_logf), W)
    return (y_p, y_s, st_p, st_s, k_p, v_p, f_p, k_s, v_s, f_s)
```

```python
import functools

import jax
import jax.numpy as jnp
import numpy as np
from jax import lax
from jax.experimental import pallas as pl
from jax.experimental.pallas import tpu as pltpu

F32 = jnp.float32
BF16 = jnp.bfloat16

EPS = 1e-6
HEAD = 128
LANES = 128
SUBLANES = 8
CHUNK = 64
N_LEVELS = 6
N_TOP = 3
HGRN_HEADS = 16
FOX_SHORT_HEADS = 4
FOX_CACHED_HEADS = 8
FOX_SHORT_QUERIES = 128
FOX_SHORT_KEYS = 2048
MASK_VALUE = -1e30
VMEM_LIMIT = 56 * 1024 * 1024

_NT = (((1,), (1,)), ((), ()))


def _params(sem, vmem=VMEM_LIMIT):
    return pltpu.CompilerParams(dimension_semantics=sem, vmem_limit_bytes=vmem)


def _sigmoid(x):
    return 1.0 / (1.0 + jnp.exp(-x))


def _pick(n, prefs):
    for p in prefs:
        if n % p == 0:
            return p
    return n


def _nmm_kernel(*refs, log_sigmoid, with_bf16, sq_blocks, row_chunk):
    refs = list(refs)
    x_ref, g_ref, w_ref = refs[:3]
    del refs[:3]
    b_ref = refs.pop(0) if log_sigmoid else None
    e_ref = refs.pop(0) if sq_blocks else None
    o_ref = refs.pop(0)
    ob_ref = refs.pop(0) if with_bf16 else None
    sq_ref = refs.pop(0) if sq_blocks else None
    xn_ref, = refs
    tm = x_ref.shape[0]

    @pl.when(pl.program_id(1) == 0)
    def _():
        for r in range(0, tm, row_chunk):
            x = x_ref[r:r + row_chunk, :]
            ms = jnp.mean(x * x, axis=-1, keepdims=True)
            xn_ref[r:r + row_chunk, :] = (x * lax.rsqrt(ms + EPS) * g_ref[...]).astype(BF16)

    z = jnp.dot(xn_ref[...], w_ref[...], preferred_element_type=F32)
    if log_sigmoid:
        z = z + b_ref[...]
        z = jnp.minimum(z, 0.0) - jnp.log(1.0 + jnp.exp(-jnp.abs(z)))
    o_ref[...] = z
    if with_bf16:
        ob_ref[...] = z.astype(BF16)
    if sq_blocks:
        @pl.when(pl.program_id(1) < sq_blocks)
        def _():
            sq_ref[...] = lax.dot_general(e_ref[...], (z * z).astype(BF16), _NT,
                                          preferred_element_type=F32)


def _norm_matmul(x, g, w, bias=None, with_bf16=False, sq_cols=0):
    T, D = x.shape
    N = w.shape[1]
    tm = _pick(T, (1024, 512, 256, 128, 64))
    tn = _pick(sq_cols or N, (1024, 512, 256, 128))
    sq_blocks = sq_cols // tn
    row_chunk = min(tm, 256)
    in_specs = [
        pl.BlockSpec((tm, D), lambda i, j: (i, 0)),
        pl.BlockSpec((1, D), lambda i, j: (0, 0)),
        pl.BlockSpec((D, tn), lambda i, j: (0, j)),
    ]
    args = [x, g.reshape(1, D).astype(F32), w]
    if bias is not None:
        in_specs.append(pl.BlockSpec((1, tn), lambda i, j: (0, j)))
        args.append(bias.reshape(1, N).astype(F32))
    if sq_blocks:
        in_specs.append(pl.BlockSpec((LANES, tn), lambda i, j: (0, 0)))
        args.append(_head_selector(tn))
    out_shape = [jax.ShapeDtypeStruct((T, N), F32)]
    out_specs = [pl.BlockSpec((tm, tn), lambda i, j: (i, j))]
    if with_bf16:
        out_shape.append(jax.ShapeDtypeStruct((T, N), BF16))
        out_specs.append(pl.BlockSpec((tm, tn), lambda i, j: (i, j)))
    if sq_blocks:
        out_shape.append(jax.ShapeDtypeStruct((sq_blocks * LANES, T), F32))
        out_specs.append(pl.BlockSpec((LANES, tm),
                                      lambda i, j: (jnp.minimum(j, sq_blocks - 1), i)))
    out = pl.pallas_call(
        functools.partial(_nmm_kernel, log_sigmoid=bias is not None, with_bf16=with_bf16,
                          sq_blocks=sq_blocks, row_chunk=row_chunk),
        grid=(T // tm, N // tn),
        in_specs=in_specs,
        out_specs=out_specs,
        out_shape=out_shape,
        scratch_shapes=[pltpu.VMEM((tm, D), BF16)],
        compiler_params=_params(("parallel", "arbitrary")),
        name="norm_matmul",
    )(*args)
    if sq_blocks:
        sq = out[-1].reshape(sq_blocks, LANES, T)[:, :tn // HEAD].reshape(sq_cols // HEAD, T)
        out = list(out[:-1]) + [sq]
    return out if (with_bf16 or sq_blocks) else out[0]


def _nmm_kv_kernel(x_ref, g_ref, w_ref, wf_ref, bf_ref, k_ref, v_ref, vb_ref, f_ref, xn_ref, *,
                   half, row_chunk):
    tm = x_ref.shape[0]
    j = pl.program_id(1)

    @pl.when(j == 0)
    def _():
        for r in range(0, tm, row_chunk):
            x = x_ref[r:r + row_chunk, :]
            ms = jnp.mean(x * x, axis=-1, keepdims=True)
            xn_ref[r:r + row_chunk, :] = (x * lax.rsqrt(ms + EPS) * g_ref[...]).astype(BF16)

    z = jnp.dot(xn_ref[...], w_ref[...], preferred_element_type=F32)

    @pl.when(j < half)
    def _():
        k_ref[...] = z.reshape(k_ref.shape)

    @pl.when(j >= half)
    def _():
        v_ref[...] = z.reshape(v_ref.shape)
        vb_ref[...] = z.astype(BF16)

    @pl.when(j == 2 * half - 1)
    def _():
        t = jnp.dot(xn_ref[...], wf_ref[...], preferred_element_type=F32) + bf_ref[...]
        f_ref[...] = jnp.minimum(t, 0.0) - jnp.log(1.0 + jnp.exp(-jnp.abs(t)))


def _norm_matmul_kv(x, g, w_kv, w_f, b_f):
    T, D = x.shape
    N = w_kv.shape[1] // 2
    tm = _pick(T, (1024, 512, 256, 128, 64))
    tn = _pick(N, (1024, 512, 256, 128))
    half = N // tn
    hn = tn // HEAD
    v_cols = lambda i, j: (i, jnp.maximum(j - half, 0))
    k_heads = lambda i, j: (i, jnp.minimum(j, half - 1), 0)
    v_heads = lambda i, j: (i, jnp.maximum(j - half, 0), 0)
    return pl.pallas_call(
        functools.partial(_nmm_kv_kernel, half=half, row_chunk=min(tm, 256)),
        grid=(T // tm, 2 * half),
        in_specs=[pl.BlockSpec((tm, D), lambda i, j: (i, 0)),
                  pl.BlockSpec((1, D), lambda i, j: (0, 0)),
                  pl.BlockSpec((D, tn), lambda i, j: (0, j)),
                  pl.BlockSpec((D, LANES), lambda i, j: (0, 0)),
                  pl.BlockSpec((1, LANES), lambda i, j: (0, 0))],
        out_specs=[pl.BlockSpec((tm, hn, HEAD), k_heads), pl.BlockSpec((tm, hn, HEAD), v_heads),
                   pl.BlockSpec((tm, tn), v_cols), pl.BlockSpec((tm, LANES), lambda i, j: (i, 0))],
        out_shape=[jax.ShapeDtypeStruct((T, N // HEAD, HEAD), F32),
                   jax.ShapeDtypeStruct((T, N // HEAD, HEAD), F32),
                   jax.ShapeDtypeStruct((T, N), BF16), jax.ShapeDtypeStruct((T, LANES), F32)],
        scratch_shapes=[pltpu.VMEM((tm, D), BF16)],
        compiler_params=_params(("parallel", "arbitrary")),
        name="norm_matmul_kv",
    )(x, g.reshape(1, D).astype(F32), w_kv, w_f, b_f.reshape(1, LANES).astype(F32))


def _hgrn_constants():
    C = CHUNK
    t = np.arange(C)[:, None]
    u = np.arange(C)[None, :]
    blocks = [(u <= t), (u > t)]
    masks = []
    for lvl in range(N_LEVELS):
        h = C >> (lvl + 1)
        mid = (t // (2 * h)) * (2 * h) + h
        if N_TOP <= lvl < N_LEVELS - 1:
            blocks.append(np.where(t >= mid, (u >= mid) & (u <= t), (u > t) & (u < mid)))
        mid_s = (u // (2 * h)) * (2 * h) + h
        masks.append((t // (2 * h) == u // (2 * h)) & (t >= mid) & (u < mid_s))
    masks.append(t == u)
    sm = np.concatenate(blocks, axis=0).astype(np.float32)
    mk = np.stack(masks, axis=0).astype(np.float32)
    return jnp.asarray(sm, BF16), jnp.asarray(mk, F32)


def _midpoint_rows(G, h):
    rows = [jnp.broadcast_to(G[b + h - 1:b + h, :], (2 * h, HEAD)) for b in range(0, CHUNK, 2 * h)]
    return rows[0] if len(rows) == 1 else jnp.concatenate(rows, axis=0)


def _hgrn_kernel(zq_ref, zf_ref, zi_ref, zg_ref, lb_ref, go_ref, s0_ref, sm_ref, mk_ref,
                 o_ref, sout_ref, st_ref, *, n_chunks, n_heads):
    C = CHUNK
    tb = pl.program_id(2)

    @pl.when(tb == 0)
    def _():
        for hh in range(n_heads):
            st_ref[hh] = s0_ref[hh].T

    odd_row = lax.broadcasted_iota(jnp.int32, (C, HEAD), 0) % 2 == 1

    def chunk(c, carry):
        r0 = pl.multiple_of(c * C, C)
        heads = range(n_heads)
        cols = [slice(hh * HEAD, (hh + 1) * HEAD) for hh in heads]
        q, k, v, f, d2 = [], [], [], [], []
        for hh in heads:
            lb = lb_ref[:, cols[hh]]
            zq = zq_ref[pl.ds(r0, C), cols[hh]]
            zf = zf_ref[pl.ds(r0, C), cols[hh]]
            q.append(zq * _sigmoid(zq))
            f.append(lb + (1.0 - lb) * _sigmoid(zf))
            g = jnp.log(f[hh])
            k.append(1.0 - f[hh])
            v.append(zi_ref[pl.ds(r0, C), cols[hh]])
            g_hi = g.astype(BF16)
            g_lo = (g - g_hi.astype(F32)).astype(BF16)
            d2.append(jnp.dot(sm_ref[...], jnp.concatenate([g_hi, g_lo], axis=1),
                              preferred_element_type=F32))
        e, lv = [], []
        for hh in heads:
            d = d2[hh][:, :HEAD] + d2[hh][:, HEAD:]
            e.append(jnp.exp(d))
            G = d[0:C]
            top = [jnp.exp(-jnp.abs(G - _midpoint_rows(G, C >> (lvl + 1)))) for lvl in range(N_TOP)]
            lv.append(top + [e[hh][2 * C:3 * C], e[hh][3 * C:4 * C],
                             jnp.where(odd_row, f[hh], 1.0)])
        a = []
        for hh in heads:
            parts = [lax.dot_general(q[hh].astype(BF16), k[hh].astype(BF16), _NT,
                                     preferred_element_type=F32)]
            for lvl in range(N_LEVELS):
                el = lv[hh][lvl]
                parts.append(lax.dot_general((q[hh] * el).astype(BF16), (k[hh] * el).astype(BF16),
                                             _NT, preferred_element_type=F32))
            acc = mk_ref[N_LEVELS] * parts[0]
            for lvl in range(N_LEVELS):
                acc = acc + mk_ref[lvl] * parts[1 + lvl]
            a.append(acc)
        o = []
        for hh in heads:
            st = st_ref[hh]
            qg = (q[hh] * e[hh][0:C]).astype(BF16)
            kg = (k[hh] * e[hh][C:2 * C]).astype(BF16)
            vb = v[hh].astype(BF16)
            oh = jnp.dot(a[hh].astype(BF16), vb, preferred_element_type=F32)
            o.append(oh + lax.dot_general(qg, st.astype(BF16), _NT, preferred_element_type=F32))
            st_ref[hh] = st * e[hh][C - 1:C, :] + jnp.dot(v[hh].T.astype(BF16), kg,
                                                          preferred_element_type=F32)
        for hh in heads:
            zg = zg_ref[pl.ds(r0, C), cols[hh]]
            ms = jnp.mean(o[hh] * o[hh], axis=-1, keepdims=True)
            og = o[hh] * lax.rsqrt(ms + EPS) * go_ref[:, cols[hh]] * (zg * _sigmoid(zg))
            o_ref[pl.ds(r0, C), cols[hh]] = og.astype(o_ref.dtype)
        return carry

    lax.fori_loop(0, n_chunks, chunk, 0)

    @pl.when(tb == pl.num_programs(2) - 1)
    def _():
        for hh in range(n_heads):
            sout_ref[hh] = st_ref[hh].T


def _hgrn_mix(z, lb, g_out, s0, B, T):
    D = z.shape[1] // 4
    H = D // HEAD
    hb = _pick(H, (HGRN_HEADS,))
    tb = _pick(T, (256, 128, 64))
    nt = T // tb
    nh = H // hb
    sm, mk = _hgrn_constants()

    def zspec(part):
        return pl.BlockSpec((tb, hb * HEAD), lambda b, h, t: (b * nt + t, part * nh + h))

    head_vec = pl.BlockSpec((1, hb * HEAD), lambda b, h, t: (0, h))
    state = pl.BlockSpec((None, hb, HEAD, HEAD), lambda b, h, t: (b, h, 0, 0))
    og, s_fin = pl.pallas_call(
        functools.partial(_hgrn_kernel, n_chunks=tb // CHUNK, n_heads=hb),
        grid=(B, nh, nt),
        in_specs=[zspec(0), zspec(1), zspec(2), zspec(3), head_vec, head_vec, state,
                  pl.BlockSpec(sm.shape, lambda b, h, t: (0, 0)),
                  pl.BlockSpec(mk.shape, lambda b, h, t: (0, 0, 0))],
        out_specs=[pl.BlockSpec((tb, hb * HEAD), lambda b, h, t: (b * nt + t, h)), state],
        out_shape=[jax.ShapeDtypeStruct((B * T, D), BF16),
                   jax.ShapeDtypeStruct((B, H, HEAD, HEAD), F32)],
        scratch_shapes=[pltpu.VMEM((hb, HEAD, HEAD), F32)],
        compiler_params=_params(("parallel", "parallel", "arbitrary")),
        name="hgrn_mix",
    )(z, z, z, z, lb.reshape(1, D), g_out.reshape(1, D).astype(F32), s0, sm, mk)
    return og, s_fin


def _out_ple_kernel(og_ref, x_ref, p_ref, wo_ref, gpg_ref, wpg_ref, wpin_ref, gfin_ref, o_ref,
                    *, final, col_chunk):
    D = x_ref.shape[1]
    y = x_ref[...] + jnp.dot(og_ref[...], wo_ref[...], preferred_element_type=F32)
    ms = jnp.mean(y * y, axis=-1, keepdims=True)
    yn = (y * lax.rsqrt(ms + EPS) * gpg_ref[...]).astype(BF16)
    pb = p_ref[...].astype(BF16)
    parts = []
    for c0 in range(0, D, col_chunk):
        gate = _sigmoid(jnp.dot(yn, wpg_ref[:, c0:c0 + col_chunk], preferred_element_type=F32))
        emb = jnp.dot(pb, wpin_ref[:, c0:c0 + col_chunk], preferred_element_type=F32)
        parts.append(y[:, c0:c0 + col_chunk] + gate * emb)
    if final:
        ms2 = sum(jnp.sum(t * t, axis=-1, keepdims=True) for t in parts) * (1.0 / D)
        inv = lax.rsqrt(ms2 + EPS)
        for n, c0 in enumerate(range(0, D, col_chunk)):
            o_ref[:, c0:c0 + col_chunk] = parts[n] * inv * gfin_ref[:, c0:c0 + col_chunk]
    else:
        for n, c0 in enumerate(range(0, D, col_chunk)):
            o_ref[:, c0:c0 + col_chunk] = parts[n]


def _out_ple(og, x, p, layer, w_out, g_pg, w_pg, w_pin, g_final, final):
    T, D = x.shape
    P = p.shape[2]
    tm = _pick(T, (256, 128, 64))
    col_chunk = min(D, 512)
    rows = lambda i: (i, 0)
    fixed = lambda i: (0, 0)
    once = pl.Buffered(1)
    return pl.pallas_call(
        functools.partial(_out_ple_kernel, final=final, col_chunk=col_chunk),
        grid=(T // tm,),
        in_specs=[pl.BlockSpec((tm, D), rows), pl.BlockSpec((tm, D), rows),
                  pl.BlockSpec((None, tm, P), lambda i: (layer, i, 0)),
                  pl.BlockSpec((D, D), fixed, pipeline_mode=once),
                  pl.BlockSpec((1, D), fixed),
                  pl.BlockSpec((D, D), fixed, pipeline_mode=once),
                  pl.BlockSpec((P, D), fixed, pipeline_mode=once),
                  pl.BlockSpec((1, D), fixed)],
        out_specs=pl.BlockSpec((tm, D), rows),
        out_shape=jax.ShapeDtypeStruct((T, D), F32),
        compiler_params=_params(("parallel",)),
        name="out_ple",
    )(og, x, p, w_out, g_pg.reshape(1, D).astype(F32), w_pg, w_pin,
      g_final.reshape(1, D).astype(F32))


def _cumsum_kernel(x_ref, u_ref, o_ref, carry_ref, *, n_sub):
    @pl.when(pl.program_id(0) == 0)
    def _():
        carry_ref[...] = jnp.zeros_like(carry_ref)

    u = u_ref[...]
    c = carry_ref[:, 0:1]
    for s in range(n_sub):
        x = x_ref[:, s * LANES:(s + 1) * LANES]
        hi = x.astype(BF16)
        r1 = x - hi.astype(F32)
        mid = r1.astype(BF16)
        lo = (r1 - mid.astype(F32)).astype(BF16)
        cs = (jnp.dot(hi, u, preferred_element_type=F32)
              + jnp.dot(mid, u, preferred_element_type=F32)
              + jnp.dot(lo, u, preferred_element_type=F32)) + c
        o_ref[:, s * LANES:(s + 1) * LANES] = cs
        c = cs[:, LANES - 1:LANES]
    carry_ref[...] = jnp.broadcast_to(c, carry_ref.shape)


def _cumsum_last(x):
    R, L = x.shape
    cb = _pick(L, (2048, 1024, 512, 256, 128))
    u = jnp.asarray(np.triu(np.ones((LANES, LANES), np.float32)), BF16)
    return pl.pallas_call(
        functools.partial(_cumsum_kernel, n_sub=cb // LANES),
        grid=(L // cb,),
        in_specs=[pl.BlockSpec((R, cb), lambda j: (0, j)),
                  pl.BlockSpec((LANES, LANES), lambda j: (0, 0))],
        out_specs=pl.BlockSpec((R, cb), lambda j: (0, j)),
        out_shape=jax.ShapeDtypeStruct((R, L), F32),
        scratch_shapes=[pltpu.VMEM((R, LANES), F32)],
        compiler_params=_params(("arbitrary",)),
        name="cumsum",
    )(x, u)


def _fox_kernel(q_ref, gate_ref, k_ref, v_ref, dq_ref, dk_ref, o_ref, *, tq, tk, q_off, scale):
    i = pl.program_id(2)
    q = q_ref[...].astype(BF16)
    dq = dq_ref[...]
    first_pos = q_off + i * tq
    n_full = (first_pos + 1) // tk
    n_all = (first_pos + tq - 1) // tk + 1

    def step(j, carry, masked):
        m, l, acc = carry
        start = pl.multiple_of(j * tk, tk)
        k = k_ref[pl.ds(start, tk), :]
        v = v_ref[pl.ds(start, tk), :]
        s = lax.dot_general(q, k, _NT, preferred_element_type=F32) * scale
        s = s + dq - dk_ref[:, pl.ds(start, tk)]
        if masked:
            kpos = start + lax.broadcasted_iota(jnp.int32, (tq, tk), 1)
            qpos = first_pos + lax.broadcasted_iota(jnp.int32, (tq, tk), 0)
            s = jnp.where(kpos <= qpos, s, MASK_VALUE)
        m_new = jnp.maximum(m, jnp.max(s, axis=-1, keepdims=True))
        alpha = jnp.exp(m - m_new)
        p = jnp.exp(s - m_new)
        l = alpha * l + jnp.sum(p, axis=-1, keepdims=True)
        acc = alpha * acc + jnp.dot(p.astype(BF16), v, preferred_element_type=F32)
        return m_new, l, acc

    init = (jnp.full((tq, 1), MASK_VALUE, F32), jnp.zeros((tq, 1), F32),
            jnp.zeros((tq, HEAD), F32))
    carry = lax.fori_loop(0, n_full, functools.partial(step, masked=False), init)
    _, l, acc = lax.fori_loop(n_full, n_all, functools.partial(step, masked=True), carry)
    gate = gate_ref[...]
    o = acc * (1.0 / l)
    o_ref[...] = (o * (gate * _sigmoid(gate))).astype(o_ref.dtype)


def _fox_short_kernel(q_ref, gate_ref, k_ref, v_ref, dq_ref, dk_ref, o_ref, *, n_heads, q_off,
                      scale):
    T, Lk = q_ref.shape[0], k_ref.shape[0]
    heads = range(n_heads)
    cols = [slice(hh * HEAD, (hh + 1) * HEAD) for hh in heads]
    visible = (lax.broadcasted_iota(jnp.int32, (T, Lk), 1)
               <= q_off + lax.broadcasted_iota(jnp.int32, (T, Lk), 0))
    s = []
    for hh in heads:
        sh = lax.dot_general(q_ref[:, cols[hh]].astype(BF16), k_ref[:, cols[hh]], _NT,
                             preferred_element_type=F32) * scale
        s.append(jnp.where(visible, sh + dq_ref[hh] - dk_ref[hh], MASK_VALUE))
    p = [jnp.exp(sh - jnp.max(sh, axis=-1, keepdims=True)) for sh in s]
    o = [jnp.dot(p[hh].astype(BF16), v_ref[:, cols[hh]], preferred_element_type=F32)
         for hh in heads]
    for hh in heads:
        gate = gate_ref[:, cols[hh]]
        oh = o[hh] * (1.0 / jnp.sum(p[hh], axis=-1, keepdims=True))
        o_ref[:, cols[hh]] = (oh * (gate * _sigmoid(gate))).astype(o_ref.dtype)


def _fox_mix_short(z, k_all, v_all, dq, dk, B, T, q_off):
    D = z.shape[1] // 2
    H = D // HEAD
    Lk = k_all.shape[1]
    hb = _pick(H, (FOX_SHORT_HEADS,))
    nh = H // hb
    kv = pl.BlockSpec((None, Lk, hb * HEAD), lambda b, h: (b, 0, h))
    return pl.pallas_call(
        functools.partial(_fox_short_kernel, n_heads=hb, q_off=q_off, scale=HEAD ** -0.5),
        grid=(B, nh),
        in_specs=[pl.BlockSpec((T, hb * HEAD), lambda b, h: (b, h)),
                  pl.BlockSpec((T, hb * HEAD), lambda b, h: (b, nh + h)),
                  kv, kv,
                  pl.BlockSpec((None, hb, T, 1), lambda b, h: (b, h, 0, 0)),
                  pl.BlockSpec((None, hb, 1, Lk), lambda b, h: (b, h, 0, 0))],
        out_specs=pl.BlockSpec((T, hb * HEAD), lambda b, h: (b, h)),
        out_shape=jax.ShapeDtypeStruct((B * T, D), BF16),
        compiler_params=_params(("parallel", "parallel")),
        name="fox_mix_short",
    )(z, z, k_all, v_all, dq, dk)


def _fox_cached_kernel(q_ref, gate_ref, kc_ref, vc_ref, kn_ref, vn_ref, dq_ref, dkc_ref, dkn_ref,
                       o_ref, *, n_heads, scale):
    T, Lc = q_ref.shape[0], kc_ref.shape[0]
    width = n_heads * HEAD
    kc = kc_ref[...].reshape(Lc, width).astype(BF16)
    vc = vc_ref[...].reshape(Lc, width).astype(BF16)
    kn = kn_ref[...].reshape(T, width).astype(BF16)
    vn = vn_ref[...].reshape(T, width).astype(BF16)
    heads = range(n_heads)
    cols = [slice(hh * HEAD, (hh + 1) * HEAD) for hh in heads]
    causal = (lax.broadcasted_iota(jnp.int32, (T, T), 1)
              <= lax.broadcasted_iota(jnp.int32, (T, T), 0))
    sc, sn = [], []
    for hh in heads:
        q = q_ref[:, cols[hh]].astype(BF16)
        c = lax.dot_general(q, kc[:, cols[hh]], _NT, preferred_element_type=F32) * scale
        n = lax.dot_general(q, kn[:, cols[hh]], _NT, preferred_element_type=F32) * scale
        sc.append(c + dq_ref[hh] - dkc_ref[hh])
        sn.append(jnp.where(causal, n + dq_ref[hh] - dkn_ref[hh], MASK_VALUE))
    pc, pn = [], []
    for hh in heads:
        m = jnp.maximum(jnp.max(sc[hh], axis=-1, keepdims=True),
                        jnp.max(sn[hh], axis=-1, keepdims=True))
        pc.append(jnp.exp(sc[hh] - m))
        pn.append(jnp.exp(sn[hh] - m))
    o = [jnp.dot(pc[hh].astype(BF16), vc[:, cols[hh]], preferred_element_type=F32)
         + jnp.dot(pn[hh].astype(BF16), vn[:, cols[hh]], preferred_element_type=F32)
         for hh in heads]
    for hh in heads:
        gate = gate_ref[:, cols[hh]]
        l = (jnp.sum(pc[hh], axis=-1, keepdims=True) + jnp.sum(pn[hh], axis=-1, keepdims=True))
        o_ref[:, cols[hh]] = (o[hh] * (1.0 / l) * (gate * _sigmoid(gate))).astype(o_ref.dtype)


def _fox_mix_cached(z, k_cache, v_cache, k_new, v_new, dq, dk_cache, dk_new, B, T):
    D = z.shape[1] // 2
    H = D // HEAD
    Lc = k_cache.shape[1]
    hb = _pick(H, (FOX_CACHED_HEADS,))
    nh = H // hb
    cache = pl.BlockSpec((None, Lc, hb, HEAD), lambda b, h: (b, 0, h, 0))
    new = pl.BlockSpec((None, T, hb, HEAD), lambda b, h: (b, 0, h, 0))
    return pl.pallas_call(
        functools.partial(_fox_cached_kernel, n_heads=hb, scale=HEAD ** -0.5),
        grid=(B, nh),
        in_specs=[pl.BlockSpec((T, hb * HEAD), lambda b, h: (b, h)),
                  pl.BlockSpec((T, hb * HEAD), lambda b, h: (b, nh + h)),
                  cache, cache, new, new,
                  pl.BlockSpec((None, hb, T, 1), lambda b, h: (b, h, 0, 0)),
                  pl.BlockSpec((None, hb, 1, Lc), lambda b, h: (b, h, 0, 0)),
                  pl.BlockSpec((None, hb, 1, T), lambda b, h: (b, h, 0, 0))],
        out_specs=pl.BlockSpec((T, hb * HEAD), lambda b, h: (b, h)),
        out_shape=jax.ShapeDtypeStruct((B * T, D), BF16),
        compiler_params=_params(("parallel", "parallel")),
        name="fox_mix_cached",
    )(z, z, k_cache, v_cache, k_new, v_new, dq, dk_cache, dk_new)


def _fox_mix(z, k_all, v_all, dq, dk, B, T, q_off):
    D = z.shape[1] // 2
    H = D // HEAD
    Lk = k_all.shape[1]
    if T <= FOX_SHORT_QUERIES and Lk <= FOX_SHORT_KEYS:
        return _fox_mix_short(z, k_all, v_all, dq, dk, B, T, q_off)
    tq = _pick(T, (512, 256, 128, 64))
    tk = Lk if Lk <= 2048 else _pick(Lk, (512, 256, 128))
    nq = T // tq
    kv = pl.BlockSpec((None, Lk, HEAD), lambda b, h, i: (b, 0, h))
    return pl.pallas_call(
        functools.partial(_fox_kernel, tq=tq, tk=tk, q_off=q_off, scale=HEAD ** -0.5),
        grid=(B, H, nq),
        in_specs=[pl.BlockSpec((tq, HEAD), lambda b, h, i: (b * nq + i, h)),
                  pl.BlockSpec((tq, HEAD), lambda b, h, i: (b * nq + i, H + h)),
                  kv, kv,
                  pl.BlockSpec((None, None, tq, 1), lambda b, h, i: (b, h, i, 0)),
                  pl.BlockSpec((None, None, 1, Lk), lambda b, h, i: (b, h, 0, 0))],
        out_specs=pl.BlockSpec((tq, HEAD), lambda b, h, i: (b * nq + i, h)),
        out_shape=jax.ShapeDtypeStruct((B * T, D), BF16),
        compiler_params=_params(("parallel", "parallel", "arbitrary")),
        name="fox_mix",
    )(z, z, k_all, v_all, dq, dk)


FOX_BOUND_MARGIN = 1.02
FOX_BOUND_LIMIT = 32.0
FOX_FAST_TILE = 1024
AUG = 6


def _top16(x):
    bits = lax.bitcast_convert_type(x, jnp.uint32) & jnp.uint32(0xFFFF0000)
    return lax.bitcast_convert_type(bits, F32)


def _split3(x):
    hi = _top16(x)
    r1 = x - hi
    mid = _top16(r1)
    lo = r1 - mid
    return hi.astype(BF16), mid.astype(BF16), lo.astype(BF16)


def _bias_terms(x, shape, axis, value_first):
    idx = lax.broadcasted_iota(jnp.int32, shape, axis)
    v0, o0 = (0, AUG // 2) if value_first else (AUG // 2, 0)
    hi, mid, lo = _split3(x)
    ones = (idx >= o0) & (idx < o0 + AUG // 2)
    return jnp.where(idx == v0, hi.astype(F32), jnp.where(idx == v0 + 1, mid.astype(F32),
           jnp.where(idx == v0 + 2, lo.astype(F32), jnp.where(ones, 1.0, 0.0))))


def _head_selector(D):
    e = np.zeros((LANES, D), np.float32)
    e[np.arange(D) // HEAD, np.arange(D)] = 1.0
    return jnp.asarray(e, BF16)


def _fox_fast_kernel(q_ref, c_ref, gate_ref, ka_ref, vt_ref, o_ref, qa_ref, *, tile, scale):
    i = pl.program_id(2)
    qa_ref[:, :HEAD] = (q_ref[...] * scale).astype(BF16)
    qa_ref[:, HEAD:] = _bias_terms(c_ref[...], (HEAD, tile), 0, True).T.astype(BF16)

    def scores(j, masked):
        start = pl.multiple_of(j * tile, tile)
        s = lax.dot_general(ka_ref[pl.ds(start, tile), :], qa_ref[...], _NT,
                            preferred_element_type=F32)
        if masked:
            kpos = lax.broadcasted_iota(jnp.int32, (tile, tile), 0)
            qpos = lax.broadcasted_iota(jnp.int32, (tile, tile), 1)
            s = jnp.where(kpos <= qpos, s, MASK_VALUE)
        return s

    def consume(s, j, lp, acc):
        start = pl.multiple_of(j * tile, tile)
        p = jnp.exp(s)
        for g in range(tile // SUBLANES):
            lp = lp + p[g * SUBLANES:(g + 1) * SUBLANES, :]
        acc = acc + jnp.dot(vt_ref[:, pl.ds(start, tile)], p.astype(BF16),
                            preferred_element_type=F32)
        return lp, acc

    def pair(j0, carry, masked_b):
        lp, acc = carry
        sa = scores(j0, False)
        sb = scores(j0 + 1, masked_b)
        lp, acc = consume(sa, j0, lp, acc)
        return consume(sb, j0 + 1, lp, acc)

    def quad(j0, carry):
        lp, acc = carry
        s0 = scores(j0, False)
        s1 = scores(j0 + 1, False)
        lp, acc = consume(s0, j0, lp, acc)
        s2 = scores(j0 + 2, False)
        lp, acc = consume(s1, j0 + 1, lp, acc)
        s3 = scores(j0 + 3, False)
        lp, acc = consume(s2, j0 + 2, lp, acc)
        return consume(s3, j0 + 3, lp, acc)

    init = (jnp.zeros((SUBLANES, tile), F32), jnp.zeros((HEAD, tile), F32))
    carry = lax.fori_loop(0, i // 4, lambda t, c: quad(4 * t, c), init)
    carry = lax.fori_loop(2 * (i // 4), i // 2, lambda t, c: pair(2 * t, c, False), carry)
    lp, acc = lax.cond(i % 2 == 1, lambda c: pair(i - 1, c, True),
                       lambda c: consume(scores(i, True), i, *c), carry)
    gate = gate_ref[...]
    o = (acc * (1.0 / jnp.sum(lp, axis=0, keepdims=True))).T
    o_ref[...] = (o * (gate * _sigmoid(gate))).astype(o_ref.dtype)


def _fox_mix_fast(z, c, k_aug, v_t, B, T):
    D = z.shape[1] // 2
    H = D // HEAD
    tile = FOX_FAST_TILE
    nq = T // tile
    return pl.pallas_call(
        functools.partial(_fox_fast_kernel, tile=tile, scale=HEAD ** -0.5),
        grid=(B, H, nq),
        in_specs=[pl.BlockSpec((tile, HEAD), lambda b, h, i: (b * nq + i, h)),
                  pl.BlockSpec((None, None, 1, tile), lambda b, h, i: (b, h, 0, i)),
                  pl.BlockSpec((tile, HEAD), lambda b, h, i: (b * nq + i, H + h)),
                  pl.BlockSpec((None, T, 2 * HEAD), lambda b, h, i: (b, 0, h)),
                  pl.BlockSpec((None, None, HEAD, T), lambda b, h, i: (b, h, 0, 0))],
        out_specs=pl.BlockSpec((tile, HEAD), lambda b, h, i: (b * nq + i, h)),
        out_shape=jax.ShapeDtypeStruct((B * T, D), BF16),
        scratch_shapes=[pltpu.VMEM((tile, 2 * HEAD), BF16)],
        compiler_params=_params(("parallel", "parallel", "arbitrary")),
        name="fox_mix_fast",
    )(z, c, z, k_aug, v_t)


def _augment_kernel(k_ref, dk_ref, e_ref, o_ref, sq_ref, *, n_heads):
    tm = k_ref.shape[0]
    k = k_ref[...].reshape(tm, n_heads * HEAD)
    sq_ref[...] = lax.dot_general(e_ref[...], (k * k).astype(BF16), _NT,
                                  preferred_element_type=F32)
    for h in range(n_heads):
        o_ref[:, 2 * h * HEAD:(2 * h + 1) * HEAD] = k[:, h * HEAD:(h + 1) * HEAD].astype(BF16)
        o_ref[:, (2 * h + 1) * HEAD:(2 * h + 2) * HEAD] = _bias_terms(
            -dk_ref[:, h:h + 1], (tm, HEAD), 1, False).astype(BF16)


def _augment_keys(k, dcum, B, Lk, H):
    D = H * HEAD
    T = B * Lk
    tm = _pick(T, (512, 256, 128, 64))
    dk_rows = jnp.transpose(dcum, (0, 2, 1)).reshape(T, H)
    out, sq = pl.pallas_call(
        functools.partial(_augment_kernel, n_heads=H),
        grid=(T // tm,),
        in_specs=[pl.BlockSpec((tm, H, HEAD), lambda i: (i, 0, 0)),
                  pl.BlockSpec((tm, H), lambda i: (i, 0)),
                  pl.BlockSpec((LANES, D), lambda i: (0, 0))],
        out_specs=[pl.BlockSpec((tm, 2 * D), lambda i: (i, 0)),
                   pl.BlockSpec((LANES, tm), lambda i: (0, i))],
        out_shape=[jax.ShapeDtypeStruct((T, 2 * D), BF16),
                   jax.ShapeDtypeStruct((LANES, T), F32)],
        compiler_params=_params(("parallel",)),
        name="augment_keys",
    )(k, dk_rows, _head_selector(D))
    return out.reshape(B, Lk, 2 * D), sq[:H]


def _trunk(x, p, s0, past, W):
    B, T, D = x.shape
    H = D // HEAD
    depth = p.shape[0]
    n_a = len(W["w_in_a"])
    xf = x.reshape(B * T, D)
    pf = p.reshape(depth, B * T, p.shape[-1])
    states = []
    for layer in range(depth):
        final = layer == depth - 1
        if layer < n_a:
            z = _norm_matmul(xf, W["g_norm_a"][layer], W["w_in_a"][layer])
            og, s_fin = _hgrn_mix(z, W["lbs"][layer], W["g_out_a"][layer], s0[layer], B, T)
            states.append(s_fin)
            w_out = W["w_out_a"][layer]
        else:
            j = layer - n_a
            if k_aug is None:
                z = _norm_matmul(xf, W["g_norm_b"][j], W["w_in_b"][j])
            else:
                z, q_sq = _norm_matmul(xf, W["g_norm_b"][j], W["w_in_b"][j], sq_cols=D)
            if cached:
                og = _fox_mix_cached(z, past[0].astype(F32), past[1].astype(F32),
                                     k_new.reshape(B, T, H, HEAD), v_new.reshape(B, T, H, HEAD),
                                     dq, dk[..., :q_off], dk[..., q_off:q_off + T], B, T)
            elif k_aug is None:
                og = _fox_mix(z, k_all, v_all, dq, dk, B, T, q_off)
            else:
                qn = jnp.transpose(jnp.sqrt(q_sq).reshape(H, B, T), (1, 0, 2))
                bound = (FOX_BOUND_MARGIN * HEAD ** -0.5) * qn * kmax
                c = (dcum - bound)[:, :, None, :]
                og = lax.cond(jnp.max(bound) <= FOX_BOUND_LIMIT,
                              lambda: _fox_mix_fast(z, c, k_aug, v_t, B, T),
                              lambda: _fox_mix(z, k_new.reshape(B, T, D).astype(BF16), v_all, dq,
                                               dk, B, T, q_off))
            w_out = W["w_out_b"][j]
        xf = _out_ple(og, xf, pf, layer, w_out, W["g_ple"][layer], W["w_ple_gate"][layer],
                      W["w_ple_in"][layer], W["g_final"], final)
        if layer == n_a - 1:
            fast = past is None and T % FOX_FAST_TILE == 0
            cached = (past is not None and T <= FOX_SHORT_QUERIES and T % SUBLANES == 0
                      and past[0].shape[1] % LANES == 0
                      and past[0].shape[1] + T <= FOX_SHORT_KEYS)
            if fast or cached:
                k_new, v_new, v_bf, logf_new = _norm_matmul_kv(xf, W["g_kv"], W["w_kv"], W["w_f"],
                                                               W["b_f"])
                k_bf = None
            else:
                k_new, k_bf = _norm_matmul(xf, W["g_kv"], W["w_kv"][:, :D], with_bf16=True)
                v_new, v_bf = _norm_matmul(xf, W["g_kv"], W["w_kv"][:, D:], with_bf16=True)
                k_bf = k_bf.reshape(B, T, D)
                logf_new = _norm_matmul(xf, W["g_kv"], W["w_f"], bias=W["b_f"])
            logf_new = logf_new[:, :H].reshape(B, T, H)
            v_bf = v_bf.reshape(B, T, D)
            if past is None:
                k_all, v_all, logf_all, q_off = k_bf, v_bf, logf_new, 0
            elif cached:
                k_all, v_all, q_off = None, None, past[0].shape[1]
                logf_all = jnp.concatenate([past[2].astype(F32), logf_new], axis=1)
            else:
                past_k, past_v, past_logf = past
                q_off = past_k.shape[1]
                k_all = jnp.concatenate([past_k.reshape(B, q_off, D).astype(BF16), k_bf], axis=1)
                v_all = jnp.concatenate([past_v.reshape(B, q_off, D).astype(BF16), v_bf], axis=1)
                logf_all = jnp.concatenate([past_logf.astype(F32), logf_new], axis=1)
            Lk = logf_all.shape[1]
            pad = (-Lk) % LANES
            if pad:
                if not cached:
                    k_all = jnp.pad(k_all, ((0, 0), (0, pad), (0, 0)))
                    v_all = jnp.pad(v_all, ((0, 0), (0, pad), (0, 0)))
                logf_all = jnp.pad(logf_all, ((0, 0), (0, pad), (0, 0)))
            rows = jnp.transpose(logf_all, (0, 2, 1)).reshape(B * H, Lk + pad)
            dcum = _cumsum_last(rows).reshape(B, H, Lk + pad)
            dk = dcum[:, :, None, :]
            dq = dcum[:, :, q_off:q_off + T, None]
            k_aug = None
            if fast:
                k_aug, k_sq = _augment_keys(k_new, dcum, B, T, H)
                kn = jnp.sqrt(k_sq).reshape(H, B, T)
                kmax = jnp.transpose(jnp.max(kn, axis=2))[:, :, None]
                v_t = jnp.transpose(v_bf.reshape(B, T, H, HEAD), (0, 2, 3, 1))
    return (xf.reshape(B, T, D), jnp.stack(states), k_new.reshape(B, T, H, HEAD),
            v_new.reshape(B, T, H, HEAD), logf_new)


def kernel(x_prompt, x_sample, state_hgrn, cache_k, cache_v, cache_logf, p_prompt, p_sample,
           g_norm_a, w_in_a, lb_logits, g_out_a, w_out_a, g_kv, w_kv, b_f,
           g_norm_b, w_in_b, w_out_b, w_ple_in, g_ple, w_ple_gate, g_final):
    D = x_prompt.shape[-1]
    H = D // HEAD
    n_a = w_in_a.shape[0]
    lbs = jnp.cumsum(jax.nn.softmax(lb_logits.astype(F32), axis=0), axis=0)
    lbs = lbs - lbs[:1]
    w_f = jnp.pad(w_kv[:, 2 * D:], ((0, 0), (0, LANES - H)))
    def layers(w):
        return [w[n].astype(BF16) for n in range(w.shape[0])]

    W = dict(
        g_norm_a=g_norm_a, w_in_a=layers(w_in_a), lbs=lbs, g_out_a=g_out_a,
        w_out_a=layers(w_out_a), g_kv=g_kv, w_kv=w_kv[:, :2 * D].astype(BF16),
        w_f=w_f.astype(BF16),
        b_f=jnp.pad(b_f, (0, LANES - H)), g_norm_b=g_norm_b, w_in_b=layers(w_in_b),
        w_out_b=layers(w_out_b), w_ple_in=layers(w_ple_in), g_ple=g_ple,
        w_ple_gate=layers(w_ple_gate), g_final=g_final)
    s0_prompt = jnp.zeros((n_a, x_prompt.shape[0], H, HEAD, HEAD), F32)
    y_p, st_p, k_p, v_p, f_p = _trunk(x_prompt, p_prompt, s0_prompt, None, W)
    y_s, st_s, k_s, v_s, f_s = _trunk(x_sample, p_sample, state_hgrn,
                                      (cache_k, cache_v, cache_logf), W)
    return (y_p, y_s, st_p, st_s, k_p, v_p, f_p, k_s, v_s, f_s)
```

```python
import functools

import jax
import jax.numpy as jnp
import numpy as np
from jax import lax
from jax.experimental import pallas as pl
from jax.experimental.pallas import tpu as pltpu

F32 = jnp.float32
BF16 = jnp.bfloat16

EPS = 1e-6
HEAD = 128
LANES = 128
SUBLANES = 8
CHUNK = 64
N_LEVELS = 6
N_TOP = 3
HGRN_HEADS = 16
FOX_SHORT_HEADS = 4
FOX_CACHED_HEADS = 8
FOX_SHORT_QUERIES = 128
FOX_SHORT_KEYS = 2048
MASK_VALUE = -1e30
VMEM_LIMIT = 56 * 1024 * 1024

_NT = (((1,), (1,)), ((), ()))


def _params(sem, vmem=VMEM_LIMIT):
    return pltpu.CompilerParams(dimension_semantics=sem, vmem_limit_bytes=vmem)


def _sigmoid(x):
    return 1.0 / (1.0 + jnp.exp(-x))


def _pick(n, prefs):
    for p in prefs:
        if n % p == 0:
            return p
    return n


def _nmm_kernel(*refs, log_sigmoid, with_bf16, sq_blocks, row_chunk):
    refs = list(refs)
    x_ref, g_ref, w_ref = refs[:3]
    del refs[:3]
    b_ref = refs.pop(0) if log_sigmoid else None
    e_ref = refs.pop(0) if sq_blocks else None
    o_ref = refs.pop(0)
    ob_ref = refs.pop(0) if with_bf16 else None
    sq_ref = refs.pop(0) if sq_blocks else None
    xn_ref, = refs
    tm = x_ref.shape[0]

    @pl.when(pl.program_id(1) == 0)
    def _():
        for r in range(0, tm, row_chunk):
            x = x_ref[r:r + row_chunk, :]
            ms = jnp.mean(x * x, axis=-1, keepdims=True)
            xn_ref[r:r + row_chunk, :] = (x * lax.rsqrt(ms + EPS) * g_ref[...]).astype(BF16)

    z = jnp.dot(xn_ref[...], w_ref[...], preferred_element_type=F32)
    if log_sigmoid:
        z = z + b_ref[...]
        z = jnp.minimum(z, 0.0) - jnp.log(1.0 + jnp.exp(-jnp.abs(z)))
    o_ref[...] = z
    if with_bf16:
        ob_ref[...] = z.astype(BF16)
    if sq_blocks:
        @pl.when(pl.program_id(1) < sq_blocks)
        def _():
            sq_ref[...] = lax.dot_general(e_ref[...], (z * z).astype(BF16), _NT,
                                          preferred_element_type=F32)


def _norm_matmul(x, g, w, bias=None, with_bf16=False, sq_cols=0):
    T, D = x.shape
    N = w.shape[1]
    tm = _pick(T, (1024, 512, 256, 128, 64))
    tn = _pick(sq_cols or N, (1024, 512, 256, 128))
    sq_blocks = sq_cols // tn
    row_chunk = min(tm, 256)
    in_specs = [
        pl.BlockSpec((tm, D), lambda i, j: (i, 0)),
        pl.BlockSpec((1, D), lambda i, j: (0, 0)),
        pl.BlockSpec((D, tn), lambda i, j: (0, j)),
    ]
    args = [x, g.reshape(1, D).astype(F32), w]
    if bias is not None:
        in_specs.append(pl.BlockSpec((1, tn), lambda i, j: (0, j)))
        args.append(bias.reshape(1, N).astype(F32))
    if sq_blocks:
        in_specs.append(pl.BlockSpec((LANES, tn), lambda i, j: (0, 0)))
        args.append(_head_selector(tn))
    out_shape = [jax.ShapeDtypeStruct((T, N), F32)]
    out_specs = [pl.BlockSpec((tm, tn), lambda i, j: (i, j))]
    if with_bf16:
        out_shape.append(jax.ShapeDtypeStruct((T, N), BF16))
        out_specs.append(pl.BlockSpec((tm, tn), lambda i, j: (i, j)))
    if sq_blocks:
        out_shape.append(jax.ShapeDtypeStruct((sq_blocks * LANES, T), F32))
        out_specs.append(pl.BlockSpec((LANES, tm),
                                      lambda i, j: (jnp.minimum(j, sq_blocks - 1), i)))
    out = pl.pallas_call(
        functools.partial(_nmm_kernel, log_sigmoid=bias is not None, with_bf16=with_bf16,
                          sq_blocks=sq_blocks, row_chunk=row_chunk),
        grid=(T // tm, N // tn),
        in_specs=in_specs,
        out_specs=out_specs,
        out_shape=out_shape,
        scratch_shapes=[pltpu.VMEM((tm, D), BF16)],
        compiler_params=_params(("parallel", "arbitrary")),
        name="norm_matmul",
    )(*args)
    if sq_blocks:
        sq = out[-1].reshape(sq_blocks, LANES, T)[:, :tn // HEAD].reshape(sq_cols // HEAD, T)
        out = list(out[:-1]) + [sq]
    return out if (with_bf16 or sq_blocks) else out[0]


def _nmm_kv_kernel(x_ref, g_ref, w_ref, wf_ref, bf_ref, k_ref, v_ref, vb_ref, f_ref, xn_ref, *,
                   half, row_chunk):
    tm = x_ref.shape[0]
    j = pl.program_id(1)

    @pl.when(j == 0)
    def _():
        for r in range(0, tm, row_chunk):
            x = x_ref[r:r + row_chunk, :]
            ms = jnp.mean(x * x, axis=-1, keepdims=True)
            xn_ref[r:r + row_chunk, :] = (x * lax.rsqrt(ms + EPS) * g_ref[...]).astype(BF16)

    z = jnp.dot(xn_ref[...], w_ref[...], preferred_element_type=F32)

    @pl.when(j < half)
    def _():
        k_ref[...] = z.reshape(k_ref.shape)

    @pl.when(j >= half)
    def _():
        v_ref[...] = z.reshape(v_ref.shape)
        vb_ref[...] = z.astype(BF16)

    @pl.when(j == 2 * half - 1)
    def _():
        t = jnp.dot(xn_ref[...], wf_ref[...], preferred_element_type=F32) + bf_ref[...]
        f_ref[...] = jnp.minimum(t, 0.0) - jnp.log(1.0 + jnp.exp(-jnp.abs(t)))


def _norm_matmul_kv(x, g, w_kv, w_f, b_f):
    T, D = x.shape
    N = w_kv.shape[1] // 2
    tm = _pick(T, (1024, 512, 256, 128, 64))
    tn = _pick(N, (1024, 512, 256, 128))
    half = N // tn
    hn = tn // HEAD
    v_cols = lambda i, j: (i, jnp.maximum(j - half, 0))
    k_heads = lambda i, j: (i, jnp.minimum(j, half - 1), 0)
    v_heads = lambda i, j: (i, jnp.maximum(j - half, 0), 0)
    return pl.pallas_call(
        functools.partial(_nmm_kv_kernel, half=half, row_chunk=min(tm, 256)),
        grid=(T // tm, 2 * half),
        in_specs=[pl.BlockSpec((tm, D), lambda i, j: (i, 0)),
                  pl.BlockSpec((1, D), lambda i, j: (0, 0)),
                  pl.BlockSpec((D, tn), lambda i, j: (0, j)),
                  pl.BlockSpec((D, LANES), lambda i, j: (0, 0)),
                  pl.BlockSpec((1, LANES), lambda i, j: (0, 0))],
        out_specs=[pl.BlockSpec((tm, hn, HEAD), k_heads), pl.BlockSpec((tm, hn, HEAD), v_heads),
                   pl.BlockSpec((tm, tn), v_cols), pl.BlockSpec((tm, LANES), lambda i, j: (i, 0))],
        out_shape=[jax.ShapeDtypeStruct((T, N // HEAD, HEAD), F32),
                   jax.ShapeDtypeStruct((T, N // HEAD, HEAD), F32),
                   jax.ShapeDtypeStruct((T, N), BF16), jax.ShapeDtypeStruct((T, LANES), F32)],
        scratch_shapes=[pltpu.VMEM((tm, D), BF16)],
        compiler_params=_params(("parallel", "arbitrary")),
        name="norm_matmul_kv",
    )(x, g.reshape(1, D).astype(F32), w_kv, w_f, b_f.reshape(1, LANES).astype(F32))


def _hgrn_constants():
    C = CHUNK
    t = np.arange(C)[:, None]
    u = np.arange(C)[None, :]
    blocks = [(u <= t), (u > t)]
    masks = []
    for lvl in range(N_LEVELS):
        h = C >> (lvl + 1)
        mid = (t // (2 * h)) * (2 * h) + h
        if N_TOP <= lvl < N_LEVELS - 1:
            blocks.append(np.where(t >= mid, (u >= mid) & (u <= t), (u > t) & (u < mid)))
        mid_s = (u // (2 * h)) * (2 * h) + h
        masks.append((t // (2 * h) == u // (2 * h)) & (t >= mid) & (u < mid_s))
    masks.append(t == u)
    sm = np.concatenate(blocks, axis=0).astype(np.float32)
    mk = np.stack(masks, axis=0).astype(np.float32)
    return jnp.asarray(sm, BF16), jnp.asarray(mk, F32)


def _midpoint_rows(G, h):
    rows = [jnp.broadcast_to(G[b + h - 1:b + h, :], (2 * h, HEAD)) for b in range(0, CHUNK, 2 * h)]
    return rows[0] if len(rows) == 1 else jnp.concatenate(rows, axis=0)


def _hgrn_kernel(zq_ref, zf_ref, zi_ref, zg_ref, lb_ref, go_ref, s0_ref, sm_ref, mk_ref,
                 o_ref, sout_ref, st_ref, *, n_chunks, n_heads):
    C = CHUNK
    tb = pl.program_id(2)

    @pl.when(tb == 0)
    def _():
        for hh in range(n_heads):
            st_ref[hh] = s0_ref[hh].T

    odd_row = lax.broadcasted_iota(jnp.int32, (C, HEAD), 0) % 2 == 1

    def chunk(c, carry):
        r0 = pl.multiple_of(c * C, C)
        heads = range(n_heads)
        cols = [slice(hh * HEAD, (hh + 1) * HEAD) for hh in heads]
        q, k, v, f, d2 = [], [], [], [], []
        for hh in heads:
            lb = lb_ref[:, cols[hh]]
            zq = zq_ref[pl.ds(r0, C), cols[hh]]
            zf = zf_ref[pl.ds(r0, C), cols[hh]]
            q.append(zq * _sigmoid(zq))
            f.append(lb + (1.0 - lb) * _sigmoid(zf))
            g = jnp.log(f[hh])
            k.append(1.0 - f[hh])
            v.append(zi_ref[pl.ds(r0, C), cols[hh]])
            g_hi = g.astype(BF16)
            g_lo = (g - g_hi.astype(F32)).astype(BF16)
            d2.append(jnp.dot(sm_ref[...], jnp.concatenate([g_hi, g_lo], axis=1),
                              preferred_element_type=F32))
        e, lv = [], []
        for hh in heads:
            d = d2[hh][:, :HEAD] + d2[hh][:, HEAD:]
            e.append(jnp.exp(d))
            G = d[0:C]
            top = [jnp.exp(-jnp.abs(G - _midpoint_rows(G, C >> (lvl + 1)))) for lvl in range(N_TOP)]
            lv.append(top + [e[hh][2 * C:3 * C], e[hh][3 * C:4 * C],
                             jnp.where(odd_row, f[hh], 1.0)])
        a = []
        for hh in heads:
            parts = [lax.dot_general(q[hh].astype(BF16), k[hh].astype(BF16), _NT,
                                     preferred_element_type=F32)]
            for lvl in range(N_LEVELS):
                el = lv[hh][lvl]
                parts.append(lax.dot_general((q[hh] * el).astype(BF16), (k[hh] * el).astype(BF16),
                                             _NT, preferred_element_type=F32))
            acc = mk_ref[N_LEVELS] * parts[0]
            for lvl in range(N_LEVELS):
                acc = acc + mk_ref[lvl] * parts[1 + lvl]
            a.append(acc)
        o = []
        for hh in heads:
            st = st_ref[hh]
            qg = (q[hh] * e[hh][0:C]).astype(BF16)
            kg = (k[hh] * e[hh][C:2 * C]).astype(BF16)
            vb = v[hh].astype(BF16)
            oh = jnp.dot(a[hh].astype(BF16), vb, preferred_element_type=F32)
            o.append(oh + lax.dot_general(qg, st.astype(BF16), _NT, preferred_element_type=F32))
            st_ref[hh] = st * e[hh][C - 1:C, :] + jnp.dot(v[hh].T.astype(BF16), kg,
                                                          preferred_element_type=F32)
        for hh in heads:
            zg = zg_ref[pl.ds(r0, C), cols[hh]]
            ms = jnp.mean(o[hh] * o[hh], axis=-1, keepdims=True)
            og = o[hh] * lax.rsqrt(ms + EPS) * go_ref[:, cols[hh]] * (zg * _sigmoid(zg))
            o_ref[pl.ds(r0, C), cols[hh]] = og.astype(o_ref.dtype)
        return carry

    lax.fori_loop(0, n_chunks, chunk, 0)

    @pl.when(tb == pl.num_programs(2) - 1)
    def _():
        for hh in range(n_heads):
            sout_ref[hh] = st_ref[hh].T


def _hgrn_mix(z, lb, g_out, s0, B, T):
    D = z.shape[1] // 4
    H = D // HEAD
    hb = _pick(H, (HGRN_HEADS,))
    tb = _pick(T, (256, 128, 64))
    nt = T // tb
    nh = H // hb
    sm, mk = _hgrn_constants()

    def zspec(part):
        return pl.BlockSpec((tb, hb * HEAD), lambda b, h, t: (b * nt + t, part * nh + h))

    head_vec = pl.BlockSpec((1, hb * HEAD), lambda b, h, t: (0, h))
    state = pl.BlockSpec((None, hb, HEAD, HEAD), lambda b, h, t: (b, h, 0, 0))
    og, s_fin = pl.pallas_call(
        functools.partial(_hgrn_kernel, n_chunks=tb // CHUNK, n_heads=hb),
        grid=(B, nh, nt),
        in_specs=[zspec(0), zspec(1), zspec(2), zspec(3), head_vec, head_vec, state,
                  pl.BlockSpec(sm.shape, lambda b, h, t: (0, 0)),
                  pl.BlockSpec(mk.shape, lambda b, h, t: (0, 0, 0))],
        out_specs=[pl.BlockSpec((tb, hb * HEAD), lambda b, h, t: (b * nt + t, h)), state],
        out_shape=[jax.ShapeDtypeStruct((B * T, D), BF16),
                   jax.ShapeDtypeStruct((B, H, HEAD, HEAD), F32)],
        scratch_shapes=[pltpu.VMEM((hb, HEAD, HEAD), F32)],
        compiler_params=_params(("parallel", "parallel", "arbitrary")),
        name="hgrn_mix",
    )(z, z, z, z, lb.reshape(1, D), g_out.reshape(1, D).astype(F32), s0, sm, mk)
    return og, s_fin


def _out_ple_kernel(og_ref, x_ref, p_ref, wo_ref, gpg_ref, wpg_ref, wpin_ref, gfin_ref, o_ref,
                    *, final, col_chunk):
    D = x_ref.shape[1]
    y = x_ref[...] + jnp.dot(og_ref[...], wo_ref[...], preferred_element_type=F32)
    ms = jnp.mean(y * y, axis=-1, keepdims=True)
    yn = (y * lax.rsqrt(ms + EPS) * gpg_ref[...]).astype(BF16)
    pb = p_ref[...].astype(BF16)
    parts = []
    for c0 in range(0, D, col_chunk):
        gate = _sigmoid(jnp.dot(yn, wpg_ref[:, c0:c0 + col_chunk], preferred_element_type=F32))
        emb = jnp.dot(pb, wpin_ref[:, c0:c0 + col_chunk], preferred_element_type=F32)
        parts.append(y[:, c0:c0 + col_chunk] + gate * emb)
    if final:
        ms2 = sum(jnp.sum(t * t, axis=-1, keepdims=True) for t in parts) * (1.0 / D)
        inv = lax.rsqrt(ms2 + EPS)
        for n, c0 in enumerate(range(0, D, col_chunk)):
            o_ref[:, c0:c0 + col_chunk] = parts[n] * inv * gfin_ref[:, c0:c0 + col_chunk]
    else:
        for n, c0 in enumerate(range(0, D, col_chunk)):
            o_ref[:, c0:c0 + col_chunk] = parts[n]


def _out_ple(og, x, p, layer, w_out, g_pg, w_pg, w_pin, g_final, final):
    T, D = x.shape
    P = p.shape[2]
    tm = _pick(T, (512, 256, 128, 64))
    col_chunk = min(D, 512)
    rows = lambda i: (i, 0)
    fixed = lambda i: (0, 0)
    once = pl.Buffered(1)
    return pl.pallas_call(
        functools.partial(_out_ple_kernel, final=final, col_chunk=col_chunk),
        grid=(T // tm,),
        in_specs=[pl.BlockSpec((tm, D), rows), pl.BlockSpec((tm, D), rows),
                  pl.BlockSpec((None, tm, P), lambda i: (layer, i, 0)),
                  pl.BlockSpec((D, D), fixed, pipeline_mode=once),
                  pl.BlockSpec((1, D), fixed),
                  pl.BlockSpec((D, D), fixed, pipeline_mode=once),
                  pl.BlockSpec((P, D), fixed, pipeline_mode=once),
                  pl.BlockSpec((1, D), fixed)],
        out_specs=pl.BlockSpec((tm, D), rows),
        out_shape=jax.ShapeDtypeStruct((T, D), F32),
        compiler_params=_params(("parallel",)),
        name="out_ple",
    )(og, x, p, w_out, g_pg.reshape(1, D).astype(F32), w_pg, w_pin,
      g_final.reshape(1, D).astype(F32))


def _cumsum_kernel(x_ref, u_ref, o_ref, carry_ref, *, n_sub):
    @pl.when(pl.program_id(0) == 0)
    def _():
        carry_ref[...] = jnp.zeros_like(carry_ref)

    u = u_ref[...]
    c = carry_ref[:, 0:1]
    for s in range(n_sub):
        x = x_ref[:, s * LANES:(s + 1) * LANES]
        hi = x.astype(BF16)
        r1 = x - hi.astype(F32)
        mid = r1.astype(BF16)
        lo = (r1 - mid.astype(F32)).astype(BF16)
        cs = (jnp.dot(hi, u, preferred_element_type=F32)
              + jnp.dot(mid, u, preferred_element_type=F32)
              + jnp.dot(lo, u, preferred_element_type=F32)) + c
        o_ref[:, s * LANES:(s + 1) * LANES] = cs
        c = cs[:, LANES - 1:LANES]
    carry_ref[...] = jnp.broadcast_to(c, carry_ref.shape)


def _cumsum_last(x):
    R, L = x.shape
    cb = _pick(L, (2048, 1024, 512, 256, 128))
    u = jnp.asarray(np.triu(np.ones((LANES, LANES), np.float32)), BF16)
    return pl.pallas_call(
        functools.partial(_cumsum_kernel, n_sub=cb // LANES),
        grid=(L // cb,),
        in_specs=[pl.BlockSpec((R, cb), lambda j: (0, j)),
                  pl.BlockSpec((LANES, LANES), lambda j: (0, 0))],
        out_specs=pl.BlockSpec((R, cb), lambda j: (0, j)),
        out_shape=jax.ShapeDtypeStruct((R, L), F32),
        scratch_shapes=[pltpu.VMEM((R, LANES), F32)],
        compiler_params=_params(("arbitrary",)),
        name="cumsum",
    )(x, u)


def _fox_kernel(q_ref, gate_ref, k_ref, v_ref, dq_ref, dk_ref, o_ref, *, tq, tk, q_off, scale):
    i = pl.program_id(2)
    q = q_ref[...].astype(BF16)
    dq = dq_ref[...]
    first_pos = q_off + i * tq
    n_full = (first_pos + 1) // tk
    n_all = (first_pos + tq - 1) // tk + 1

    def step(j, carry, masked):
        m, l, acc = carry
        start = pl.multiple_of(j * tk, tk)
        k = k_ref[pl.ds(start, tk), :]
        v = v_ref[pl.ds(start, tk), :]
        s = lax.dot_general(q, k, _NT, preferred_element_type=F32) * scale
        s = s + dq - dk_ref[:, pl.ds(start, tk)]
        if masked:
            kpos = start + lax.broadcasted_iota(jnp.int32, (tq, tk), 1)
            qpos = first_pos + lax.broadcasted_iota(jnp.int32, (tq, tk), 0)
            s = jnp.where(kpos <= qpos, s, MASK_VALUE)
        m_new = jnp.maximum(m, jnp.max(s, axis=-1, keepdims=True))
        alpha = jnp.exp(m - m_new)
        p = jnp.exp(s - m_new)
        l = alpha * l + jnp.sum(p, axis=-1, keepdims=True)
        acc = alpha * acc + jnp.dot(p.astype(BF16), v, preferred_element_type=F32)
        return m_new, l, acc

    init = (jnp.full((tq, 1), MASK_VALUE, F32), jnp.zeros((tq, 1), F32),
            jnp.zeros((tq, HEAD), F32))
    carry = lax.fori_loop(0, n_full, functools.partial(step, masked=False), init)
    _, l, acc = lax.fori_loop(n_full, n_all, functools.partial(step, masked=True), carry)
    gate = gate_ref[...]
    o = acc * (1.0 / l)
    o_ref[...] = (o * (gate * _sigmoid(gate))).astype(o_ref.dtype)


def _fox_short_kernel(q_ref, gate_ref, k_ref, v_ref, dq_ref, dk_ref, o_ref, *, n_heads, q_off,
                      scale):
    T, Lk = q_ref.shape[0], k_ref.shape[0]
    heads = range(n_heads)
    cols = [slice(hh * HEAD, (hh + 1) * HEAD) for hh in heads]
    visible = (lax.broadcasted_iota(jnp.int32, (T, Lk), 1)
               <= q_off + lax.broadcasted_iota(jnp.int32, (T, Lk), 0))
    s = []
    for hh in heads:
        sh = lax.dot_general(q_ref[:, cols[hh]].astype(BF16), k_ref[:, cols[hh]], _NT,
                             preferred_element_type=F32) * scale
        s.append(jnp.where(visible, sh + dq_ref[hh] - dk_ref[hh], MASK_VALUE))
    p = [jnp.exp(sh - jnp.max(sh, axis=-1, keepdims=True)) for sh in s]
    o = [jnp.dot(p[hh].astype(BF16), v_ref[:, cols[hh]], preferred_element_type=F32)
         for hh in heads]
    for hh in heads:
        gate = gate_ref[:, cols[hh]]
        oh = o[hh] * (1.0 / jnp.sum(p[hh], axis=-1, keepdims=True))
        o_ref[:, cols[hh]] = (oh * (gate * _sigmoid(gate))).astype(o_ref.dtype)


def _fox_mix_short(z, k_all, v_all, dq, dk, B, T, q_off):
    D = z.shape[1] // 2
    H = D // HEAD
    Lk = k_all.shape[1]
    hb = _pick(H, (FOX_SHORT_HEADS,))
    nh = H // hb
    kv = pl.BlockSpec((None, Lk, hb * HEAD), lambda b, h: (b, 0, h))
    return pl.pallas_call(
        functools.partial(_fox_short_kernel, n_heads=hb, q_off=q_off, scale=HEAD ** -0.5),
        grid=(B, nh),
        in_specs=[pl.BlockSpec((T, hb * HEAD), lambda b, h: (b, h)),
                  pl.BlockSpec((T, hb * HEAD), lambda b, h: (b, nh + h)),
                  kv, kv,
                  pl.BlockSpec((None, hb, T, 1), lambda b, h: (b, h, 0, 0)),
                  pl.BlockSpec((None, hb, 1, Lk), lambda b, h: (b, h, 0, 0))],
        out_specs=pl.BlockSpec((T, hb * HEAD), lambda b, h: (b, h)),
        out_shape=jax.ShapeDtypeStruct((B * T, D), BF16),
        compiler_params=_params(("parallel", "parallel")),
        name="fox_mix_short",
    )(z, z, k_all, v_all, dq, dk)


def _fox_cached_kernel(q_ref, gate_ref, kc_ref, vc_ref, kn_ref, vn_ref, dq_ref, dkc_ref, dkn_ref,
                       o_ref, *, n_heads, scale):
    T, Lc = q_ref.shape[0], kc_ref.shape[0]
    width = n_heads * HEAD
    kc = kc_ref[...].reshape(Lc, width).astype(BF16)
    vc = vc_ref[...].reshape(Lc, width).astype(BF16)
    kn = kn_ref[...].reshape(T, width).astype(BF16)
    vn = vn_ref[...].reshape(T, width).astype(BF16)
    heads = range(n_heads)
    cols = [slice(hh * HEAD, (hh + 1) * HEAD) for hh in heads]
    causal = (lax.broadcasted_iota(jnp.int32, (T, T), 1)
              <= lax.broadcasted_iota(jnp.int32, (T, T), 0))
    sc, sn = [], []
    for hh in heads:
        q = q_ref[:, cols[hh]].astype(BF16)
        c = lax.dot_general(q, kc[:, cols[hh]], _NT, preferred_element_type=F32) * scale
        n = lax.dot_general(q, kn[:, cols[hh]], _NT, preferred_element_type=F32) * scale
        sc.append(c + dq_ref[hh] - dkc_ref[hh])
        sn.append(jnp.where(causal, n + dq_ref[hh] - dkn_ref[hh], MASK_VALUE))
    pc, pn = [], []
    for hh in heads:
        m = jnp.maximum(jnp.max(sc[hh], axis=-1, keepdims=True),
                        jnp.max(sn[hh], axis=-1, keepdims=True))
        pc.append(jnp.exp(sc[hh] - m))
        pn.append(jnp.exp(sn[hh] - m))
    o = [jnp.dot(pc[hh].astype(BF16), vc[:, cols[hh]], preferred_element_type=F32)
         + jnp.dot(pn[hh].astype(BF16), vn[:, cols[hh]], preferred_element_type=F32)
         for hh in heads]
    for hh in heads:
        gate = gate_ref[:, cols[hh]]
        l = (jnp.sum(pc[hh], axis=-1, keepdims=True) + jnp.sum(pn[hh], axis=-1, keepdims=True))
        o_ref[:, cols[hh]] = (o[hh] * (1.0 / l) * (gate * _sigmoid(gate))).astype(o_ref.dtype)


def _fox_mix_cached(z, k_cache, v_cache, k_new, v_new, dq, dk_cache, dk_new, B, T):
    D = z.shape[1] // 2
    H = D // HEAD
    Lc = k_cache.shape[1]
    hb = _pick(H, (FOX_CACHED_HEADS,))
    nh = H // hb
    cache = pl.BlockSpec((None, Lc, hb, HEAD), lambda b, h: (b, 0, h, 0))
    new = pl.BlockSpec((None, T, hb, HEAD), lambda b, h: (b, 0, h, 0))
    return pl.pallas_call(
        functools.partial(_fox_cached_kernel, n_heads=hb, scale=HEAD ** -0.5),
        grid=(B, nh),
        in_specs=[pl.BlockSpec((T, hb * HEAD), lambda b, h: (b, h)),
                  pl.BlockSpec((T, hb * HEAD), lambda b, h: (b, nh + h)),
                  cache, cache, new, new,
                  pl.BlockSpec((None, hb, T, 1), lambda b, h: (b, h, 0, 0)),
                  pl.BlockSpec((None, hb, 1, Lc), lambda b, h: (b, h, 0, 0)),
                  pl.BlockSpec((None, hb, 1, T), lambda b, h: (b, h, 0, 0))],
        out_specs=pl.BlockSpec((T, hb * HEAD), lambda b, h: (b, h)),
        out_shape=jax.ShapeDtypeStruct((B * T, D), BF16),
        compiler_params=_params(("parallel", "parallel")),
        name="fox_mix_cached",
    )(z, z, k_cache, v_cache, k_new, v_new, dq, dk_cache, dk_new)


def _fox_mix(z, k_all, v_all, dq, dk, B, T, q_off):
    D = z.shape[1] // 2
    H = D // HEAD
    Lk = k_all.shape[1]
    if T <= FOX_SHORT_QUERIES and Lk <= FOX_SHORT_KEYS:
        return _fox_mix_short(z, k_all, v_all, dq, dk, B, T, q_off)
    tq = _pick(T, (512, 256, 128, 64))
    tk = Lk if Lk <= 2048 else _pick(Lk, (512, 256, 128))
    nq = T // tq
    kv = pl.BlockSpec((None, Lk, HEAD), lambda b, h, i: (b, 0, h))
    return pl.pallas_call(
        functools.partial(_fox_kernel, tq=tq, tk=tk, q_off=q_off, scale=HEAD ** -0.5),
        grid=(B, H, nq),
        in_specs=[pl.BlockSpec((tq, HEAD), lambda b, h, i: (b * nq + i, h)),
                  pl.BlockSpec((tq, HEAD), lambda b, h, i: (b * nq + i, H + h)),
                  kv, kv,
                  pl.BlockSpec((None, None, tq, 1), lambda b, h, i: (b, h, i, 0)),
                  pl.BlockSpec((None, None, 1, Lk), lambda b, h, i: (b, h, 0, 0))],
        out_specs=pl.BlockSpec((tq, HEAD), lambda b, h, i: (b * nq + i, h)),
        out_shape=jax.ShapeDtypeStruct((B * T, D), BF16),
        compiler_params=_params(("parallel", "parallel", "arbitrary")),
        name="fox_mix",
    )(z, z, k_all, v_all, dq, dk)


FOX_BOUND_MARGIN = 1.02
FOX_BOUND_LIMIT = 32.0
FOX_FAST_TILE = 1024
AUG = 6


def _top16(x):
    bits = lax.bitcast_convert_type(x, jnp.uint32) & jnp.uint32(0xFFFF0000)
    return lax.bitcast_convert_type(bits, F32)


def _split3(x):
    hi = _top16(x)
    r1 = x - hi
    mid = _top16(r1)
    lo = r1 - mid
    return hi.astype(BF16), mid.astype(BF16), lo.astype(BF16)


def _bias_terms(x, shape, axis, value_first):
    idx = lax.broadcasted_iota(jnp.int32, shape, axis)
    v0, o0 = (0, AUG // 2) if value_first else (AUG // 2, 0)
    hi, mid, lo = _split3(x)
    ones = (idx >= o0) & (idx < o0 + AUG // 2)
    return jnp.where(idx == v0, hi.astype(F32), jnp.where(idx == v0 + 1, mid.astype(F32),
           jnp.where(idx == v0 + 2, lo.astype(F32), jnp.where(ones, 1.0, 0.0))))


def _head_selector(D):
    e = np.zeros((LANES, D), np.float32)
    e[np.arange(D) // HEAD, np.arange(D)] = 1.0
    return jnp.asarray(e, BF16)


def _fox_fast_kernel(q_ref, c_ref, gate_ref, ka_ref, vt_ref, o_ref, qa_ref, *, tile, scale):
    i = pl.program_id(2)
    qa_ref[:, :HEAD] = (q_ref[...] * scale).astype(BF16)
    qa_ref[:, HEAD:] = _bias_terms(c_ref[...], (HEAD, tile), 0, True).T.astype(BF16)

    def scores(j, masked):
        start = pl.multiple_of(j * tile, tile)
        s = lax.dot_general(ka_ref[pl.ds(start, tile), :], qa_ref[...], _NT,
                            preferred_element_type=F32)
        if masked:
            kpos = lax.broadcasted_iota(jnp.int32, (tile, tile), 0)
            qpos = lax.broadcasted_iota(jnp.int32, (tile, tile), 1)
            s = jnp.where(kpos <= qpos, s, MASK_VALUE)
        return s

    def consume(s, j, lp, acc):
        start = pl.multiple_of(j * tile, tile)
        p = jnp.exp(s)
        for g in range(tile // SUBLANES):
            lp = lp + p[g * SUBLANES:(g + 1) * SUBLANES, :]
        acc = acc + jnp.dot(vt_ref[:, pl.ds(start, tile)], p.astype(BF16),
                            preferred_element_type=F32)
        return lp, acc

    def pair(j0, carry, masked_b):
        lp, acc = carry
        sa = scores(j0, False)
        sb = scores(j0 + 1, masked_b)
        lp, acc = consume(sa, j0, lp, acc)
        return consume(sb, j0 + 1, lp, acc)

    def quad(j0, carry):
        lp, acc = carry
        s0 = scores(j0, False)
        s1 = scores(j0 + 1, False)
        lp, acc = consume(s0, j0, lp, acc)
        s2 = scores(j0 + 2, False)
        lp, acc = consume(s1, j0 + 1, lp, acc)
        s3 = scores(j0 + 3, False)
        lp, acc = consume(s2, j0 + 2, lp, acc)
        return consume(s3, j0 + 3, lp, acc)

    init = (jnp.zeros((SUBLANES, tile), F32), jnp.zeros((HEAD, tile), F32))
    carry = lax.fori_loop(0, i // 4, lambda t, c: quad(4 * t, c), init)
    carry = lax.fori_loop(2 * (i // 4), i // 2, lambda t, c: pair(2 * t, c, False), carry)
    lp, acc = lax.cond(i % 2 == 1, lambda c: pair(i - 1, c, True),
                       lambda c: consume(scores(i, True), i, *c), carry)
    gate = gate_ref[...]
    o = (acc * (1.0 / jnp.sum(lp, axis=0, keepdims=True))).T
    o_ref[...] = (o * (gate * _sigmoid(gate))).astype(o_ref.dtype)


def _fox_mix_fast(z, c, k_aug, v_t, B, T):
    D = z.shape[1] // 2
    H = D // HEAD
    tile = FOX_FAST_TILE
    nq = T // tile
    return pl.pallas_call(
        functools.partial(_fox_fast_kernel, tile=tile, scale=HEAD ** -0.5),
        grid=(B, H, nq),
        in_specs=[pl.BlockSpec((tile, HEAD), lambda b, h, i: (b * nq + i, h)),
                  pl.BlockSpec((None, None, 1, tile), lambda b, h, i: (b, h, 0, i)),
                  pl.BlockSpec((tile, HEAD), lambda b, h, i: (b * nq + i, H + h)),
                  pl.BlockSpec((None, T, 2 * HEAD), lambda b, h, i: (b, 0, h)),
                  pl.BlockSpec((None, None, HEAD, T), lambda b, h, i: (b, h, 0, 0))],
        out_specs=pl.BlockSpec((tile, HEAD), lambda b, h, i: (b * nq + i, h)),
        out_shape=jax.ShapeDtypeStruct((B * T, D), BF16),
        scratch_shapes=[pltpu.VMEM((tile, 2 * HEAD), BF16)],
        compiler_params=_params(("parallel", "parallel", "arbitrary")),
        name="fox_mix_fast",
    )(z, c, z, k_aug, v_t)


def _augment_kernel(k_ref, dk_ref, e_ref, o_ref, sq_ref, *, n_heads):
    tm = k_ref.shape[0]
    k = k_ref[...].reshape(tm, n_heads * HEAD)
    sq_ref[...] = lax.dot_general(e_ref[...], (k * k).astype(BF16), _NT,
                                  preferred_element_type=F32)
    for h in range(n_heads):
        o_ref[:, 2 * h * HEAD:(2 * h + 1) * HEAD] = k[:, h * HEAD:(h + 1) * HEAD].astype(BF16)
        o_ref[:, (2 * h + 1) * HEAD:(2 * h + 2) * HEAD] = _bias_terms(
            -dk_ref[:, h:h + 1], (tm, HEAD), 1, False).astype(BF16)


def _augment_keys(k, dcum, B, Lk, H):
    D = H * HEAD
    T = B * Lk
    tm = _pick(T, (512, 256, 128, 64))
    dk_rows = jnp.transpose(dcum, (0, 2, 1)).reshape(T, H)
    out, sq = pl.pallas_call(
        functools.partial(_augment_kernel, n_heads=H),
        grid=(T // tm,),
        in_specs=[pl.BlockSpec((tm, H, HEAD), lambda i: (i, 0, 0)),
                  pl.BlockSpec((tm, H), lambda i: (i, 0)),
                  pl.BlockSpec((LANES, D), lambda i: (0, 0))],
        out_specs=[pl.BlockSpec((tm, 2 * D), lambda i: (i, 0)),
                   pl.BlockSpec((LANES, tm), lambda i: (0, i))],
        out_shape=[jax.ShapeDtypeStruct((T, 2 * D), BF16),
                   jax.ShapeDtypeStruct((LANES, T), F32)],
        compiler_params=_params(("parallel",)),
        name="augment_keys",
    )(k, dk_rows, _head_selector(D))
    return out.reshape(B, Lk, 2 * D), sq[:H]


def _trunk(x, p, s0, past, W):
    B, T, D = x.shape
    H = D // HEAD
    depth = p.shape[0]
    n_a = len(W["w_in_a"])
    xf = x.reshape(B * T, D)
    pf = p.reshape(depth, B * T, p.shape[-1])
    states = []
    for layer in range(depth):
        final = layer == depth - 1
        if layer < n_a:
            z = _norm_matmul(xf, W["g_norm_a"][layer], W["w_in_a"][layer])
            og, s_fin = _hgrn_mix(z, W["lbs"][layer], W["g_out_a"][layer], s0[layer], B, T)
            states.append(s_fin)
            w_out = W["w_out_a"][layer]
        else:
            j = layer - n_a
            if k_aug is None:
                z = _norm_matmul(xf, W["g_norm_b"][j], W["w_in_b"][j])
            else:
                z, q_sq = _norm_matmul(xf, W["g_norm_b"][j], W["w_in_b"][j], sq_cols=D)
            if cached:
                og = _fox_mix_cached(z, past[0].astype(F32), past[1].astype(F32),
                                     k_new.reshape(B, T, H, HEAD), v_new.reshape(B, T, H, HEAD),
                                     dq, dk[..., :q_off], dk[..., q_off:q_off + T], B, T)
            elif k_aug is None:
                og = _fox_mix(z, k_all, v_all, dq, dk, B, T, q_off)
            else:
                qn = jnp.transpose(jnp.sqrt(q_sq).reshape(H, B, T), (1, 0, 2))
                bound = (FOX_BOUND_MARGIN * HEAD ** -0.5) * qn * kmax
                c = (dcum - bound)[:, :, None, :]
                og = lax.cond(jnp.max(bound) <= FOX_BOUND_LIMIT,
                              lambda: _fox_mix_fast(z, c, k_aug, v_t, B, T),
                              lambda: _fox_mix(z, k_new.reshape(B, T, D).astype(BF16), v_all, dq,
                                               dk, B, T, q_off))
            w_out = W["w_out_b"][j]
        xf = _out_ple(og, xf, pf, layer, w_out, W["g_ple"][layer], W["w_ple_gate"][layer],
                      W["w_ple_in"][layer], W["g_final"], final)
        if layer == n_a - 1:
            fast = past is None and T % FOX_FAST_TILE == 0
            cached = (past is not None and T <= FOX_SHORT_QUERIES and T % SUBLANES == 0
                      and past[0].shape[1] % LANES == 0
                      and past[0].shape[1] + T <= FOX_SHORT_KEYS)
            if fast or cached:
                k_new, v_new, v_bf, logf_new = _norm_matmul_kv(xf, W["g_kv"], W["w_kv"], W["w_f"],
                                                               W["b_f"])
                k_bf = None
            else:
                k_new, k_bf = _norm_matmul(xf, W["g_kv"], W["w_kv"][:, :D], with_bf16=True)
                v_new, v_bf = _norm_matmul(xf, W["g_kv"], W["w_kv"][:, D:], with_bf16=True)
                k_bf = k_bf.reshape(B, T, D)
                logf_new = _norm_matmul(xf, W["g_kv"], W["w_f"], bias=W["b_f"])
            logf_new = logf_new[:, :H].reshape(B, T, H)
            v_bf = v_bf.reshape(B, T, D)
            if past is None:
                k_all, v_all, logf_all, q_off = k_bf, v_bf, logf_new, 0
            elif cached:
                k_all, v_all, q_off = None, None, past[0].shape[1]
                logf_all = jnp.concatenate([past[2].astype(F32), logf_new], axis=1)
            else:
                past_k, past_v, past_logf = past
                q_off = past_k.shape[1]
                k_all = jnp.concatenate([past_k.reshape(B, q_off, D).astype(BF16), k_bf], axis=1)
                v_all = jnp.concatenate([past_v.reshape(B, q_off, D).astype(BF16), v_bf], axis=1)
                logf_all = jnp.concatenate([past_logf.astype(F32), logf_new], axis=1)
            Lk = logf_all.shape[1]
            pad = (-Lk) % LANES
            if pad:
                if not cached:
                    k_all = jnp.pad(k_all, ((0, 0), (0, pad), (0, 0)))
                    v_all = jnp.pad(v_all, ((0, 0), (0, pad), (0, 0)))
                logf_all = jnp.pad(logf_all, ((0, 0), (0, pad), (0, 0)))
            rows = jnp.transpose(logf_all, (0, 2, 1)).reshape(B * H, Lk + pad)
            dcum = _cumsum_last(rows).reshape(B, H, Lk + pad)
            dk = dcum[:, :, None, :]
            dq = dcum[:, :, q_off:q_off + T, None]
            k_aug = None
            if fast:
                k_aug, k_sq = _augment_keys(k_new, dcum, B, T, H)
                kn = jnp.sqrt(k_sq).reshape(H, B, T)
                kmax = jnp.transpose(jnp.max(kn, axis=2))[:, :, None]
                v_t = jnp.transpose(v_bf.reshape(B, T, H, HEAD), (0, 2, 3, 1))
    return (xf.reshape(B, T, D), jnp.stack(states), k_new.reshape(B, T, H, HEAD),
            v_new.reshape(B, T, H, HEAD), logf_new)


def kernel(x_prompt, x_sample, state_hgrn, cache_k, cache_v, cache_logf, p_prompt, p_sample,
           g_norm_a, w_in_a, lb_logits, g_out_a, w_out_a, g_kv, w_kv, b_f,
           g_norm_b, w_in_b, w_out_b, w_ple_in, g_ple, w_ple_gate, g_final):
    D = x_prompt.shape[-1]
    H = D // HEAD
    n_a = w_in_a.shape[0]
    lbs = jnp.cumsum(jax.nn.softmax(lb_logits.astype(F32), axis=0), axis=0)
    lbs = lbs - lbs[:1]
    w_f = jnp.pad(w_kv[:, 2 * D:], ((0, 0), (0, LANES - H)))
    def layers(w):
        return [w[n].astype(BF16) for n in range(w.shape[0])]

    W = dict(
        g_norm_a=g_norm_a, w_in_a=layers(w_in_a), lbs=lbs, g_out_a=g_out_a,
        w_out_a=layers(w_out_a), g_kv=g_kv, w_kv=w_kv[:, :2 * D].astype(BF16),
        w_f=w_f.astype(BF16),
        b_f=jnp.pad(b_f, (0, LANES - H)), g_norm_b=g_norm_b, w_in_b=layers(w_in_b),
        w_out_b=layers(w_out_b), w_ple_in=layers(w_ple_in), g_ple=g_ple,
        w_ple_gate=layers(w_ple_gate), g_final=g_final)
    s0_prompt = jnp.zeros((n_a, x_prompt.shape[0], H, HEAD, HEAD), F32)
    y_p, st_p, k_p, v_p, f_p = _trunk(x_prompt, p_prompt, s0_prompt, None, W)
    y_s, st_s, k_s, v_s, f_s = _trunk(x_sample, p_sample, state_hgrn,
                                      (cache_k, cache_v, cache_logf), W)
    return (y_p, y_s, st_p, st_s, k_p, v_p, f_p, k_s, v_s, f_s)
```
